```python
import jax, jax.numpy as jnp
from jax import lax
import numpy as np

D_MODEL = 1024
BATCH = 8
SEQ = 2048
DEPTH = 1
DEC_BATCH = 32
DEC_SEQ = 4
PAST_LEN = 16384
PAGE_SIZE = 128

MLA_HEADS = 8
QK_NOPE = 64
QK_ROPE = 32
V_HEAD = 64
Q_LORA = 256
KV_LORA = 128
FOX_HEADS = 8
FOX_HEAD_DIM = 64
FOX_W = FOX_HEADS * FOX_HEAD_DIM
MIX_WIDTH = MLA_HEADS * V_HEAD + FOX_W
IN_SIZES = (Q_LORA, KV_LORA, QK_ROPE, FOX_W, FOX_W, FOX_W, FOX_HEADS)
N_IN = sum(IN_SIZES)
IN_SPLITS = tuple(int(v) for v in np.cumsum(IN_SIZES)[:-1])
N_GROUPS = 4
EXPERTS_PER_GROUP = 8
TOP_K = 2
D_EXPERT = 256

Q_BLOCK = 128
ROPE_THETA = 10000.0
EPS = 1e-6
NEG = -1e30
MLA_SCALE = (QK_NOPE + QK_ROPE) ** -0.5
FOX_SCALE = FOX_HEAD_DIM ** -0.5

kernel_name = "hymba_mla_fox_hier_moe_step"


def rmsnorm(x, g):
    xf = x.astype(jnp.float32)
    y = xf * lax.rsqrt(jnp.mean(xf * xf, axis=-1, keepdims=True) + EPS)
    return (y * g.astype(jnp.float32)).astype(x.dtype)


def rope(x, pos):
    half = QK_ROPE // 2
    inv = ROPE_THETA ** (-jnp.arange(half, dtype=jnp.float32) / half)
    ang = pos.astype(jnp.float32)[:, None] * inv[None, :]
    shape = (1, pos.shape[0]) + (1,) * (x.ndim - 3) + (half,)
    cos = jnp.cos(ang).reshape(shape)
    sin = jnp.sin(ang).reshape(shape)
    xf = x.astype(jnp.float32)
    x1, x2 = xf[..., :half], xf[..., half:]
    return jnp.concatenate([x1 * cos - x2 * sin, x2 * cos + x1 * sin], axis=-1).astype(x.dtype)


def project_inputs(h, pos, lw):
    b, s, _ = h.shape
    z = jnp.einsum('bsd,dn->bsn', h, lw['w_in'])
    cq, ckv, kr, fq, fk, fv, fl = jnp.split(z, IN_SPLITS, axis=-1)
    cq = rmsnorm(cq, lw['g_q_lora'])
    q = jnp.einsum('bsr,rhd->bshd', cq, lw['w_uq'])
    q_nope, q_rope = q[..., :QK_NOPE], q[..., QK_NOPE:]
    q_abs = jnp.einsum('bshn,hcn->bshc', q_nope, lw['w_uk'])
    q_rope = rope(q_rope, pos)
    c_kv = rmsnorm(ckv, lw['g_kv_lora'])
    k_rope = rope(kr, pos)
    logf = jax.nn.log_sigmoid((fl + lw['b_forget']).astype(jnp.float32))
    return dict(q_abs=q_abs, q_rope=q_rope, c_kv=c_kv, k_rope=k_rope,
                fq=fq.reshape(b, s, FOX_HEADS, FOX_HEAD_DIM),
                fk=fk.reshape(b, s, FOX_HEADS, FOX_HEAD_DIM),
                fv=fv.reshape(b, s, FOX_HEADS, FOX_HEAD_DIM),
                logf=logf)


def mla_prompt(q_abs, q_rope, c_kv, k_rope):
    b, s, h, c = q_abs.shape
    nb = s // Q_BLOCK
    qa = q_abs.reshape(b, nb, Q_BLOCK, h, c).swapaxes(0, 1)
    qr = q_rope.reshape(b, nb, Q_BLOCK, h, QK_ROPE).swapaxes(0, 1)
    kpos = jnp.arange(s)

    def one_block(args):
        qa_b, qr_b, i = args
        sc = (jnp.einsum('bqhc,bkc->bhqk', qa_b, c_kv)
              + jnp.einsum('bqhr,bkr->bhqk', qr_b, k_rope)).astype(jnp.float32) * MLA_SCALE
        qpos = i * Q_BLOCK + jnp.arange(Q_BLOCK)
        sc = jnp.where(kpos[None, :] <= qpos[:, None], sc, NEG)
        p = jax.nn.softmax(sc, axis=-1).astype(c_kv.dtype)
        return jnp.einsum('bhqk,bkc->bqhc', p, c_kv)

    o = lax.map(one_block, (qa, qr, jnp.arange(nb)))
    return o.swapaxes(0, 1).reshape(b, s, h, c)


def fox_prompt(q, k, v, logf):
    b, s, h, d = q.shape
    nb = s // Q_BLOCK
    cum = jnp.cumsum(logf, axis=1).transpose(0, 2, 1)
    qb = q.reshape(b, nb, Q_BLOCK, h, d).swapaxes(0, 1)
    cq = cum.reshape(b, h, nb, Q_BLOCK).transpose(2, 0, 1, 3)
    kpos = jnp.arange(s)

    def one_block(args):
        q_b, cq_b, i = args
        sc = jnp.einsum('bqhd,bkhd->bhqk', q_b, k).astype(jnp.float32) * FOX_SCALE
        sc = sc + cq_b[..., :, None] - cum[:, :, None, :]
        qpos = i * Q_BLOCK + jnp.arange(Q_BLOCK)
        sc = jnp.where(kpos[None, :] <= qpos[:, None], sc, NEG)
        p = jax.nn.softmax(sc, axis=-1).astype(v.dtype)
        return jnp.einsum('bhqk,bkhd->bqhd', p, v)

    o = lax.map(one_block, (qb, cq, jnp.arange(nb)))
    return o.swapaxes(0, 1).reshape(b, s, h, d)


def mla_sample(q_abs, q_rope, c_kv, k_rope, lat_past, kr_past):
    n_new = q_abs.shape[1]
    n_past = lat_past.shape[1]
    s_past = (jnp.einsum('bqhc,bkc->bhqk', q_abs, lat_past)
              + jnp.einsum('bqhr,bkr->bhqk', q_rope, kr_past)).astype(jnp.float32) * MLA_SCALE
    s_new = (jnp.einsum('bqhc,bkc->bhqk', q_abs, c_kv)
             + jnp.einsum('bqhr,bkr->bhqk', q_rope, k_rope)).astype(jnp.float32) * MLA_SCALE
    tri = jnp.tril(jnp.ones((n_new, n_new), dtype=bool))
    s_new = jnp.where(tri, s_new, NEG)
    p = jax.nn.softmax(jnp.concatenate([s_past, s_new], axis=-1), axis=-1).astype(c_kv.dtype)
    return (jnp.einsum('bhqk,bkc->bqhc', p[..., :n_past], lat_past)
            + jnp.einsum('bhqk,bkc->bqhc', p[..., n_past:], c_kv))


def fox_sample(q, k, v, logf, k_past, v_past, logf_past):
    n_new = q.shape[1]
    n_past = k_past.shape[1]
    lp = logf_past.astype(jnp.float32)
    suffix = (lax.cumsum(lp, axis=1, reverse=True) - lp).transpose(0, 2, 1)
    cum_new = jnp.cumsum(logf, axis=1).transpose(0, 2, 1)
    s_past = jnp.einsum('bqhd,bkhd->bhqk', q, k_past).astype(jnp.float32) * FOX_SCALE
    s_past = s_past + cum_new[..., :, None] + suffix[..., None, :]
    s_new = jnp.einsum('bqhd,bkhd->bhqk', q, k).astype(jnp.float32) * FOX_SCALE
    s_new = s_new + cum_new[..., :, None] - cum_new[..., None, :]
    tri = jnp.tril(jnp.ones((n_new, n_new), dtype=bool))
    s_new = jnp.where(tri, s_new, NEG)
    p = jax.nn.softmax(jnp.concatenate([s_past, s_new], axis=-1), axis=-1).astype(v.dtype)
    return (jnp.einsum('bhqk,bkhd->bqhd', p[..., :n_past], v_past)
            + jnp.einsum('bhqk,bkhd->bqhd', p[..., n_past:], v))


def hier_moe(h, lw):
    b, s, d = h.shape
    t = h.reshape(-1, d)
    p_grp = jax.nn.softmax((t @ lw['w_group_router']).astype(jnp.float32) + lw['b_group_router'].astype(jnp.float32), axis=-1)
    g_idx = jnp.argmax(p_grp, axis=-1)
    g_w = jnp.take_along_axis(p_grp, g_idx[:, None], axis=1)[:, 0]
    e_logits = jnp.einsum('td,dge->tge', t, lw['w_expert_router']).astype(jnp.float32) + lw['b_expert_router'].astype(jnp.float32)
    e_sel = jnp.take_along_axis(e_logits, g_idx[:, None, None], axis=1)[:, 0]
    top_w, top_i = lax.top_k(jax.nn.softmax(e_sel, axis=-1), TOP_K)
    top_w = top_w / jnp.sum(top_w, axis=-1, keepdims=True)
    w_e = jnp.sum(jax.nn.one_hot(top_i, EXPERTS_PER_GROUP) * top_w[..., None], axis=1)
    combine = (jax.nn.one_hot(g_idx, N_GROUPS)[:, :, None] * (g_w[:, None] * w_e)[:, None, :]).astype(t.dtype)
    y = jnp.zeros_like(t)
    for g in range(N_GROUPS):
        a = jnp.einsum('td,edf->tef', t, lw['w_gate'][g])
        u = jnp.einsum('td,edf->tef', t, lw['w_up'][g])
        act = jax.nn.silu(a) * u * combine[:, g, :, None]
        y = y + jnp.einsum('tef,efd->td', act, lw['w_down'][g])
    return y.reshape(b, s, d)


def decoder_layer(x, c, pos, attend, lw):
    b, s, d = x.shape
    mod = (jax.nn.silu(c) @ lw['w_ada'] + lw['b_ada']).reshape(b, 6, 1, d)
    shift_m, scale_m, gate_m, shift_f, scale_f, gate_f = [mod[:, i] for i in range(6)]
    h = rmsnorm(x, lw['g_pre_mix']) * (1 + scale_m) + shift_m
    pr = project_inputs(h, pos, lw)
    o_lat, o_fox = attend(pr)
    o_mla = jnp.einsum('bshc,hcv->bshv', o_lat, lw['w_uv'])
    mixed = jnp.concatenate([o_mla.reshape(b, s, -1), o_fox.reshape(b, s, -1)], axis=-1) @ lw['w_o']
    x = x + gate_m * rmsnorm(mixed, lw['g_post_mix'])
    h2 = rmsnorm(x, lw['g_pre_ffn']) * (1 + scale_f) + shift_f
    x = x + gate_f * rmsnorm(hier_moe(h2, lw), lw['g_post_ffn'])
    return x, pr


def gather_pages(pool, page_table, layer):
    rows = pool[page_table, layer]
    return rows.reshape((page_table.shape[0], -1) + pool.shape[3:])


def setup_inputs(seed: int = 0) -> dict:
    key = jax.random.key(seed)
    ks = jax.random.split(key, 40)
    nrm = lambda k, shape, sc: jax.random.normal(k, shape, jnp.float32) * sc
    n_pages = PAST_LEN // PAGE_SIZE
    n_used = DEC_BATCH * n_pages
    n_phys = n_used + max(1, n_used // 4)
    page_table = jax.random.permutation(ks[0], n_phys)[:n_used].reshape(DEC_BATCH, n_pages).astype(jnp.int32)
    L, D, G, E, F, H = DEPTH, D_MODEL, N_GROUPS, EXPERTS_PER_GROUP, D_EXPERT, MLA_HEADS
    return {
        'x_prompt': nrm(ks[1], (BATCH, SEQ, D), 1.0),
        'x_sample': nrm(ks[2], (DEC_BATCH, DEC_SEQ, D), 1.0),
        'cache_mla_latent': nrm(ks[3], (n_phys, L, PAGE_SIZE, KV_LORA), 1.0),
        'cache_mla_krope': nrm(ks[4], (n_phys, L, PAGE_SIZE, QK_ROPE), 1.0),
        'cache_fox_k': nrm(ks[5], (n_phys, L, PAGE_SIZE, FOX_HEADS, FOX_HEAD_DIM), 1.0),
        'cache_fox_v': nrm(ks[6], (n_phys, L, PAGE_SIZE, FOX_HEADS, FOX_HEAD_DIM), 1.0),
        'cache_fox_logf': jax.nn.log_sigmoid(2.0 + nrm(ks[7], (n_phys, L, PAGE_SIZE, FOX_HEADS), 0.5)),
        'page_table': page_table,
        'c_prompt': nrm(ks[8], (BATCH, D), 1.0),
        'c_sample': nrm(ks[9], (DEC_BATCH, D), 1.0),
        'w_ada': nrm(ks[10], (L, D, 6 * D), 0.5 * D ** -0.5),
        'b_ada': nrm(ks[11], (L, 6 * D), 0.02),
        'g_pre_mix': 1.0 + nrm(ks[12], (L, D), 0.05),
        'g_post_mix': 1.0 + nrm(ks[13], (L, D), 0.05),
        'g_pre_ffn': 1.0 + nrm(ks[14], (L, D), 0.05),
        'g_post_ffn': 1.0 + nrm(ks[15], (L, D), 0.05),
        'w_in': nrm(ks[16], (L, D, N_IN), D ** -0.5),
        'b_forget': 2.0 + nrm(ks[17], (L, FOX_HEADS), 0.5),
        'g_q_lora': 1.0 + nrm(ks[18], (L, Q_LORA), 0.05),
        'g_kv_lora': 1.0 + nrm(ks[19], (L, KV_LORA), 0.05),
        'w_uq': nrm(ks[20], (L, Q_LORA, H, QK_NOPE + QK_ROPE), Q_LORA ** -0.5),
        'w_uk': nrm(ks[21], (L, H, KV_LORA, QK_NOPE), KV_LORA ** -0.5),
        'w_uv': nrm(ks[22], (L, H, KV_LORA, V_HEAD), KV_LORA ** -0.5),
        'w_o': nrm(ks[23], (L, MIX_WIDTH, D), MIX_WIDTH ** -0.5),
        'w_group_router': nrm(ks[24], (L, D, G), D ** -0.5),
        'b_group_router': nrm(ks[25], (L, G), 0.01),
        'w_expert_router': nrm(ks[26], (L, D, G, E), D ** -0.5),
        'b_expert_router': nrm(ks[27], (L, G, E), 0.01),
        'w_gate': nrm(ks[28], (L, G, E, D, F), D ** -0.5),
        'w_up': nrm(ks[29], (L, G, E, D, F), D ** -0.5),
        'w_down': nrm(ks[30], (L, G, E, F, D), F ** -0.5),
    }


def reference(x_prompt, x_sample, cache_mla_latent, cache_mla_krope, cache_fox_k, cache_fox_v,
              cache_fox_logf, page_table, c_prompt, c_sample, w_ada, b_ada, g_pre_mix, g_post_mix,
              g_pre_ffn, g_post_ffn, w_in, b_forget, g_q_lora, g_kv_lora, w_uq, w_uk, w_uv, w_o,
              w_group_router, b_group_router, w_expert_router, b_expert_router, w_gate, w_up, w_down):
    seq = x_prompt.shape[1]
    n_new = x_sample.shape[1]
    past_len = page_table.shape[1] * cache_mla_latent.shape[2]
    pos_prompt = jnp.arange(seq)
    pos_sample = past_len + jnp.arange(n_new)
    row_names = ('c_kv', 'k_rope', 'fk', 'fv', 'logf')
    p_rows = {n: [] for n in row_names}
    s_rows = {n: [] for n in row_names}
    xp, xs = x_prompt, x_sample
    for l in range(DEPTH):
        lw = dict(w_ada=w_ada[l], b_ada=b_ada[l], g_pre_mix=g_pre_mix[l], g_post_mix=g_post_mix[l],
                  g_pre_ffn=g_pre_ffn[l], g_post_ffn=g_post_ffn[l], w_in=w_in[l], b_forget=b_forget[l],
                  g_q_lora=g_q_lora[l], g_kv_lora=g_kv_lora[l], w_uq=w_uq[l], w_uk=w_uk[l], w_uv=w_uv[l],
                  w_o=w_o[l], w_group_router=w_group_router[l], b_group_router=b_group_router[l],
                  w_expert_router=w_expert_router[l], b_expert_router=b_expert_router[l],
                  w_gate=w_gate[l], w_up=w_up[l], w_down=w_down[l])

        def attend_prompt(pr):
            return (mla_prompt(pr['q_abs'], pr['q_rope'], pr['c_kv'], pr['k_rope']),
                    fox_prompt(pr['fq'], pr['fk'], pr['fv'], pr['logf']))

        lat_past = gather_pages(cache_mla_latent, page_table, l)
        kr_past = gather_pages(cache_mla_krope, page_table, l)
        fk_past = gather_pages(cache_fox_k, page_table, l)
        fv_past = gather_pages(cache_fox_v, page_table, l)
        lf_past = gather_pages(cache_fox_logf, page_table, l)

        def attend_sample(pr, lat_past=lat_past, kr_past=kr_past, fk_past=fk_past, fv_past=fv_past, lf_past=lf_past):
            return (mla_sample(pr['q_abs'], pr['q_rope'], pr['c_kv'], pr['k_rope'], lat_past, kr_past),
                    fox_sample(pr['fq'], pr['fk'], pr['fv'], pr['logf'], fk_past, fv_past, lf_past))

        xp, prp = decoder_layer(xp, c_prompt, pos_prompt, attend_prompt, lw)
        xs, prs = decoder_layer(xs, c_sample, pos_sample, attend_sample, lw)
        for n in row_names:
            p_rows[n].append(prp[n])
            s_rows[n].append(prs[n])
    st = lambda rows, n: jnp.stack(rows[n], axis=1)
    return (xp, xs,
            st(p_rows, 'c_kv'), st(p_rows, 'k_rope'), st(p_rows, 'fk'), st(p_rows, 'fv'), st(p_rows, 'logf'),
            st(s_rows, 'c_kv'), st(s_rows, 'k_rope'), st(s_rows, 'fk'), st(s_rows, 'fv'), st(s_rows, 'logf'))
```

```python
import functools
import math

import jax
import jax.numpy as jnp
import numpy as np
from jax import lax
from jax.experimental import pallas as pl
from jax.experimental.pallas import tpu as pltpu

F32 = jnp.float32
BF16 = jnp.bfloat16

D_MODEL = 1024
MLA_HEADS = 8
QK_NOPE = 64
QK_ROPE = 32
V_HEAD = 64
Q_LORA = 256
KV_LORA = 128
FOX_HEADS = 8
FOX_HEAD_DIM = 64
FOX_W = FOX_HEADS * FOX_HEAD_DIM
N_GROUPS = 4
EXPERTS_PER_GROUP = 8
N_EXPERTS = N_GROUPS * EXPERTS_PER_GROUP
D_EXPERT = 256
ROPE_THETA = 10000.0
EPS = 1e-6
NEG = -1e30
LOG2E = 1.4426950408889634
MLA_SCALE = (QK_NOPE + QK_ROPE) ** -0.5
FOX_SCALE = FOX_HEAD_DIM ** -0.5

LANES = 128
QHEAD_W = 256
W_ALL_COLS = 2304
VMEM_LIMIT = 56 * 1024 * 1024


def _split(x):
    hi = x.astype(BF16)
    lo = (x - hi.astype(F32)).astype(BF16)
    return hi, lo


def _mm(a, b):
    return jnp.dot(a, b, preferred_element_type=F32)


def _mm_nt(a, b):
    return lax.dot_general(a, b, (((1,), (1,)), ((), ())), preferred_element_type=F32)


def _mm3(a, b_hi, b_lo):
    a_hi, a_lo = _split(a)
    m = a.shape[0]
    top = _mm(jnp.concatenate([a_hi, a_lo], axis=0), b_hi)
    return top[:m] + top[m:] + _mm(a_hi, b_lo)


def _mm3_nt(a, b_hi, b_lo):
    a_hi, a_lo = _split(a)
    m = a.shape[0]
    top = _mm_nt(jnp.concatenate([a_hi, a_lo], axis=0), b_hi)
    return top[:m] + top[m:] + _mm_nt(a_hi, b_lo)


def _rms(x, g):
    return x * lax.rsqrt(jnp.mean(x * x, axis=-1, keepdims=True) + EPS) * g


def _silu(x):
    return x / (1.0 + jnp.exp(-x))


def _log_sigmoid(x):
    return jnp.minimum(x, 0.0) - jnp.log(1.0 + jnp.exp(-jnp.abs(x)))


def _cparams(sem, vmem=None):
    return pltpu.CompilerParams(dimension_semantics=sem, vmem_limit_bytes=vmem)


def _full(shape):
    nd = len(shape)
    return pl.BlockSpec(shape, lambda *_: (0,) * nd)


def _ada_kernel(c_ref, w_ref, b_ref, o_ref):
    a = _silu(c_ref[...])
    w_hi, w_lo = _split(w_ref[...])
    o_ref[...] = _mm3(a, w_hi, w_lo) + b_ref[...]


def _ada(c_all, w_ada, b_ada):
    n, d = c_all.shape
    cols = w_ada.shape[1]
    tn = 1024
    return pl.pallas_call(
        _ada_kernel,
        grid=(cols // tn,),
        in_specs=[_full((n, d)),
                  pl.BlockSpec((d, tn), lambda j: (0, j)),
                  pl.BlockSpec((1, tn), lambda j: (0, j))],
        out_specs=pl.BlockSpec((n, tn), lambda j: (0, j)),
        out_shape=jax.ShapeDtypeStruct((n, cols), F32),
        compiler_params=_cparams(("parallel",)),
        name="ada_mod",
    )(c_all, w_ada, b_ada)


def _fold_kernel(a_ref, b_ref, o_ref):
    b_hi, b_lo = _split(b_ref[0])
    o_ref[0] = _mm3(a_ref[0], b_hi, b_lo)


def _fold(a, b):
    h, m, k = a.shape
    n = b.shape[2]
    return pl.pallas_call(
        _fold_kernel,
        grid=(h,),
        in_specs=[pl.BlockSpec((1, m, k), lambda i: (i, 0, 0)),
                  pl.BlockSpec((1, k, n), lambda i: (i, 0, 0))],
        out_specs=pl.BlockSpec((1, m, n), lambda i: (i, 0, 0)),
        out_shape=jax.ShapeDtypeStruct((h, m, n), F32),
        compiler_params=_cparams(("parallel",)),
        name="fold_weights",
    )(a, b)


def _proj_common(x, mod, g_pre, g_q, g_kv, bfg, tabs, mm_all, mm_qa, mm_qb):
    ctab, stab, kct, kst = tabs
    shift = mod[:, 0:D_MODEL]
    scale = mod[:, D_MODEL:2 * D_MODEL]
    h = _rms(x, g_pre) * (1.0 + scale) + shift
    z = mm_all(h)
    cq = _rms(z[:, 0:Q_LORA], g_q)
    nh = MLA_HEADS
    q = (mm_qa(cq) * jnp.concatenate([ctab] * nh, axis=1)
         + mm_qb(cq) * jnp.concatenate([stab] * nh, axis=1))
    c_kv = _rms(z[:, 256:384], g_kv)
    krp = z[:, 384:512] * kct + z[:, 512:640] * kst
    fq = z[:, 640:1152] * (FOX_SCALE * LOG2E)
    fk = z[:, 1152:1664]
    fv = z[:, 1664:2176]
    logf = _log_sigmoid(z[:, 2176:2304] + bfg)
    return q, c_kv, krp, fq, fk, fv, logf


def _proj_prompt_kernel(x_ref, mod_ref, gpre_ref, gq_ref, gkv_ref, bfg_ref,
                        ct_ref, st_ref, kct_ref, kst_ref,
                        wall_ref, wa_ref, wb_ref,
                        q_ref, kmla_ref, ckv_ref, kr_ref, fq_ref, fk_ref, fk16_ref,
                        fv_ref, fv16_ref, lf_ref, cum_ref, carry_ref, *, tiles_per_batch):
    i = pl.program_id(0)
    tm = x_ref.shape[0]
    q, c_kv, krp, fq, fk, fv, logf = _proj_common(
        x_ref[...], mod_ref[0], gpre_ref[...], gq_ref[...], gkv_ref[...], bfg_ref[...],
        (ct_ref[...], st_ref[...], kct_ref[...], kst_ref[...]),
        lambda h: _mm(h.astype(BF16), wall_ref[...]),
        lambda c: _mm(c.astype(BF16), wa_ref[...]),
        lambda c: _mm(c.astype(BF16), wb_ref[...]))
    q_ref[...] = q.astype(BF16)
    ckv_ref[...] = c_kv
    kr_ref[...] = krp[:, 0:QK_ROPE]
    kmla_ref[...] = jnp.concatenate([c_kv, krp], axis=1).astype(BF16)
    fq_ref[...] = fq.astype(BF16)
    fk_ref[...] = fk
    fk16_ref[...] = fk.astype(BF16)
    fv_ref[...] = fv
    fv16_ref[...] = fv.astype(BF16)
    lf_ref[...] = logf[:, 0:FOX_HEADS]

    @pl.when(i % tiles_per_batch == 0)
    def _():
        carry_ref[...] = jnp.zeros_like(carry_ref)

    row = lax.broadcasted_iota(jnp.int32, (tm, tm), 0)
    col = lax.broadcasted_iota(jnp.int32, (tm, tm), 1)
    tri = (row >= col).astype(BF16)
    p1 = logf.astype(BF16)
    r1 = logf - p1.astype(F32)
    p2 = r1.astype(BF16)
    p3 = (r1 - p2.astype(F32)).astype(BF16)
    cum = _mm(tri, p1) + _mm(tri, p2) + _mm(tri, p3) + carry_ref[0:1, :]
    carry_ref[0:1, :] = cum[tm - 1:tm, :]
    cum_ref[...] = cum * LOG2E


def _proj_prompt(x2d, mod3, gpre, gq, gkv, bfg, tabs, wall, wa, wb, *, seq, tm):
    t = x2d.shape[0]
    tpb = seq // tm
    row = lambda w: pl.BlockSpec((tm, w), lambda i: (i, 0))
    pos = lambda w: pl.BlockSpec((tm, w), lambda i: (i % tpb, 0))
    out_shapes = [
        jax.ShapeDtypeStruct((t, MLA_HEADS * QHEAD_W), BF16),
        jax.ShapeDtypeStruct((t, QHEAD_W), BF16),
        jax.ShapeDtypeStruct((t, KV_LORA), F32),
        jax.ShapeDtypeStruct((t, QK_ROPE), F32),
        jax.ShapeDtypeStruct((t, FOX_W), BF16),
        jax.ShapeDtypeStruct((t, FOX_W), F32),
        jax.ShapeDtypeStruct((t, FOX_W), BF16),
        jax.ShapeDtypeStruct((t, FOX_W), F32),
        jax.ShapeDtypeStruct((t, FOX_W), BF16),
        jax.ShapeDtypeStruct((t, FOX_HEADS), F32),
        jax.ShapeDtypeStruct((t, LANES), F32),
    ]
    out_specs = [row(MLA_HEADS * QHEAD_W), row(QHEAD_W), row(KV_LORA), row(QK_ROPE),
                 row(FOX_W), row(FOX_W), row(FOX_W), row(FOX_W), row(FOX_W),
                 row(FOX_HEADS), row(LANES)]
    return pl.pallas_call(
        functools.partial(_proj_prompt_kernel, tiles_per_batch=tpb),
        grid=(t // tm,),
        in_specs=[row(D_MODEL),
                  pl.BlockSpec((1, 1, 6 * D_MODEL), lambda i: (i // tpb, 0, 0)),
                  _full(gpre.shape), _full(gq.shape), _full(gkv.shape), _full(bfg.shape),
                  pos(QHEAD_W), pos(QHEAD_W), pos(LANES), pos(LANES),
                  _full(wall.shape), _full(wa.shape), _full(wb.shape)],
        out_specs=out_specs,
        out_shape=out_shapes,
        scratch_shapes=[pltpu.VMEM((8, LANES), F32)],
        compiler_params=_cparams(("arbitrary",), VMEM_LIMIT),
        name="proj_prompt",
    )(x2d, mod3, gpre, gq, gkv, bfg, *tabs, wall, wa, wb)


def _proj_sample_kernel(x_ref, mod_ref, gpre_ref, gq_ref, gkv_ref, bfg_ref,
                        ct_ref, st_ref, kct_ref, kst_ref,
                        wall_hi, wall_lo, wa_hi, wa_lo, wb_hi, wb_lo,
                        q_ref, ckv_ref, krp_ref, fq_ref, fk_ref, fv_ref, lf_ref):
    q, c_kv, krp, fq, fk, fv, logf = _proj_common(
        x_ref[...], mod_ref[...], gpre_ref[...], gq_ref[...], gkv_ref[...], bfg_ref[...],
        (ct_ref[...], st_ref[...], kct_ref[...], kst_ref[...]),
        lambda h: _mm3(h, wall_hi[...], wall_lo[...]),
        lambda c: _mm3(c, wa_hi[...], wa_lo[...]),
        lambda c: _mm3(c, wb_hi[...], wb_lo[...]))
    q_ref[...] = q
    ckv_ref[...] = c_kv
    krp_ref[...] = krp
    fq_ref[...] = fq
    fk_ref[...] = fk
    fv_ref[...] = fv
    lf_ref[...] = logf


def _proj_sample(x2d, mod_tok, gpre, gq, gkv, bfg, tabs, wall, wa, wb):
    t = x2d.shape[0]
    ins = [x2d, mod_tok, gpre, gq, gkv, bfg, *tabs, *wall, *wa, *wb]
    shapes = [(t, MLA_HEADS * QHEAD_W), (t, KV_LORA), (t, LANES), (t, FOX_W), (t, FOX_W),
              (t, FOX_W), (t, LANES)]
    return pl.pallas_call(
        _proj_sample_kernel,
        grid=(1,),
        in_specs=[_full(a.shape) for a in ins],
        out_specs=[_full(s) for s in shapes],
        out_shape=[jax.ShapeDtypeStruct(s, F32) for s in shapes],
        compiler_params=_cparams(("arbitrary",), VMEM_LIMIT),
        name="proj_sample",
    )(*ins)


def _flash_update(s, v, m_ref, l_ref, acc_ref):
    m_prev = m_ref[...]
    m_new = jnp.maximum(m_prev, jnp.max(s, axis=1, keepdims=True))
    alpha = jnp.exp2(m_prev - m_new)
    p = jnp.exp2(s - m_new)
    l_ref[...] = alpha * l_ref[...] + jnp.sum(p, axis=1, keepdims=True)
    acc_ref[...] = alpha * acc_ref[...] + _mm(p.astype(BF16), v)
    m_ref[...] = m_new


def _causal_bounds(i, tq, tk):
    n_full = (i * tq + 1) // tk
    n_all = (i * tq + tq + tk - 1) // tk
    return n_full, n_all


def _mla_prompt_kernel(q_ref, k_ref, o_ref, m_ref, l_ref, acc_ref, *, tq, tk):
    i = pl.program_id(1)
    nh = MLA_HEADS
    qs = jnp.concatenate([q_ref[:, h * QHEAD_W:(h + 1) * QHEAD_W] for h in range(nh)], axis=0)
    m_ref[...] = jnp.full(m_ref.shape, NEG, F32)
    l_ref[...] = jnp.zeros(l_ref.shape, F32)
    acc_ref[...] = jnp.zeros(acc_ref.shape, F32)

    def step(j, masked):
        kb = k_ref[pl.ds(pl.multiple_of(j * tk, tk), tk), :]
        s = _mm_nt(qs, kb)
        if masked:
            row = lax.broadcasted_iota(jnp.int32, s.shape, 0)
            col = lax.broadcasted_iota(jnp.int32, s.shape, 1)
            s = jnp.where(j * tk + col <= i * tq + (row & (tq - 1)), s, NEG)
        _flash_update(s, kb[:, 0:KV_LORA], m_ref, l_ref, acc_ref)

    n_full, n_all = _causal_bounds(i, tq, tk)
    lax.fori_loop(0, n_full, lambda j, c: (step(j, False), c)[1], 0)
    lax.fori_loop(n_full, n_all, lambda j, c: (step(j, True), c)[1], 0)
    o = acc_ref[...] / l_ref[...]
    o_ref[...] = jnp.concatenate([o[h * tq:(h + 1) * tq] for h in range(nh)], axis=1).astype(BF16)


def _mla_prompt(q, kmla, *, batch, seq, tq, tk):
    t = q.shape[0]
    nq = seq // tq
    rows = MLA_HEADS * tq
    return pl.pallas_call(
        functools.partial(_mla_prompt_kernel, tq=tq, tk=tk),
        grid=(batch, nq),
        in_specs=[pl.BlockSpec((tq, MLA_HEADS * QHEAD_W), lambda b, i: (b * nq + i, 0)),
                  pl.BlockSpec((seq, QHEAD_W), lambda b, i: (b, 0))],
        out_specs=pl.BlockSpec((tq, MLA_HEADS * KV_LORA), lambda b, i: (b * nq + i, 0)),
        out_shape=jax.ShapeDtypeStruct((t, MLA_HEADS * KV_LORA), BF16),
        scratch_shapes=[pltpu.VMEM((rows, 1), F32), pltpu.VMEM((rows, 1), F32),
                        pltpu.VMEM((rows, KV_LORA), F32)],
        compiler_params=_cparams(("parallel", "parallel"), VMEM_LIMIT),
        name="mla_prompt",
    )(q, kmla)


def _fox_prompt_kernel(q_ref, k_ref, v_ref, ck_ref, o_ref, m_ref, l_ref, acc_ref, *, tq, tk):
    i = pl.program_id(1)
    lane = lax.broadcasted_iota(jnp.int32, (tq, LANES), 1)
    n_full, n_all = _causal_bounds(i, tq, tk)
    for pair in range(FOX_HEADS // 2):
        cols = slice(pair * LANES, (pair + 1) * LANES)
        qp = q_ref[:, cols]
        o_pair = jnp.zeros((tq, LANES), F32)
        for half in range(2):
            h = 2 * pair + half
            sel = (lane >= FOX_HEAD_DIM) if half else (lane < FOX_HEAD_DIM)
            qm = jnp.where(sel, qp, jnp.zeros_like(qp))
            m_ref[...] = jnp.full(m_ref.shape, NEG, F32)
            l_ref[...] = jnp.zeros(l_ref.shape, F32)
            acc_ref[...] = jnp.zeros(acc_ref.shape, F32)

            def step(j, masked, h=h, qm=qm, cols=cols):
                rows = pl.ds(pl.multiple_of(j * tk, tk), tk)
                kb = k_ref[rows, cols]
                vb = v_ref[rows, cols]
                s = _mm_nt(qm, kb) - ck_ref[0, h, pl.ds(j, 1), :]
                if masked:
                    row = lax.broadcasted_iota(jnp.int32, s.shape, 0)
                    col = lax.broadcasted_iota(jnp.int32, s.shape, 1)
                    s = jnp.where(j * tk + col <= i * tq + row, s, NEG)
                _flash_update(s, vb, m_ref, l_ref, acc_ref)

            lax.fori_loop(0, n_full, lambda j, c, step=step: (step(j, False), c)[1], 0)
            lax.fori_loop(n_full, n_all, lambda j, c, step=step: (step(j, True), c)[1], 0)
            o_pair = jnp.where(sel, acc_ref[...] / l_ref[...], o_pair)
        o_ref[:, cols] = o_pair.astype(BF16)


def _fox_prompt(fq, fk, fv, ck4, *, batch, seq, tq, tk):
    t = fq.shape[0]
    nq = seq // tq
    nk = seq // tk
    return pl.pallas_call(
        functools.partial(_fox_prompt_kernel, tq=tq, tk=tk),
        grid=(batch, nq),
        in_specs=[pl.BlockSpec((tq, FOX_W), lambda b, i: (b * nq + i, 0)),
                  pl.BlockSpec((seq, FOX_W), lambda b, i: (b, 0)),
                  pl.BlockSpec((seq, FOX_W), lambda b, i: (b, 0)),
                  pl.BlockSpec((1, FOX_HEADS, nk, tk), lambda b, i: (b, 0, 0, 0))],
        out_specs=pl.BlockSpec((tq, FOX_W), lambda b, i: (b * nq + i, 0)),
        out_shape=jax.ShapeDtypeStruct((t, FOX_W), BF16),
        scratch_shapes=[pltpu.VMEM((tq, 1), F32), pltpu.VMEM((tq, 1), F32),
                        pltpu.VMEM((tq, LANES), F32)],
        compiler_params=_cparams(("parallel", "parallel"), VMEM_LIMIT),
        name="fox_prompt",
    )(fq, fk, fv, ck4)


def _route(r):
    lane = lax.broadcasted_iota(jnp.int32, r.shape, 1)
    lanef = lane.astype(F32)
    big = jnp.float32(1e9)
    gmask = lane < N_GROUPS
    gl = jnp.where(gmask, r, -jnp.inf)
    gm = jnp.max(gl, axis=1, keepdims=True)
    gs = jnp.sum(jnp.where(gmask, jnp.exp(r - gm), 0.0), axis=1, keepdims=True)
    g_w = 1.0 / gs
    g_idx = jnp.min(jnp.where(gl == gm, lanef, big), axis=1, keepdims=True)
    egrp = ((lane - N_GROUPS) >> 3).astype(F32)
    emask = (lane >= N_GROUPS) & (lane < N_GROUPS + N_EXPERTS) & (egrp == g_idx)
    el = jnp.where(emask, r, -jnp.inf)
    em = jnp.max(el, axis=1, keepdims=True)
    ee = jnp.where(emask, jnp.exp(r - em), 0.0)
    p = ee / jnp.sum(ee, axis=1, keepdims=True)
    pm = jnp.where(emask, p, -1.0)
    p1 = jnp.max(pm, axis=1, keepdims=True)
    i1 = jnp.min(jnp.where(pm == p1, lanef, big), axis=1, keepdims=True)
    pm2 = jnp.where(lanef == i1, -1.0, pm)
    p2 = jnp.max(pm2, axis=1, keepdims=True)
    i2 = jnp.min(jnp.where(pm2 == p2, lanef, big), axis=1, keepdims=True)
    den = p1 + p2
    c1 = g_w * (p1 / den)
    c2 = g_w * (p2 / den)
    out = jnp.where(lane == 0, i1 - N_GROUPS,
                    jnp.where(lane == 1, i2 - N_GROUPS,
                              jnp.where(lane == 2, c1, jnp.where(lane == 3, c2, 0.0))))
    return out


def _post_common(x, mod, mixed, gpost, gffn, wr_hi, wr_lo, br):
    gate_m = mod[:, 2 * D_MODEL:3 * D_MODEL]
    shift_f = mod[:, 3 * D_MODEL:4 * D_MODEL]
    scale_f = mod[:, 4 * D_MODEL:5 * D_MODEL]
    x1 = x + gate_m * _rms(mixed, gpost)
    h2 = _rms(x1, gffn) * (1.0 + scale_f) + shift_f
    r = _mm3(h2, wr_hi, wr_lo) + br
    return x1, h2, _route(r)


def _post_prompt_kernel(x_ref, mod_ref, ol_ref, of_ref, wov_ref, wof_ref, gpost_ref, gffn_ref,
                        wrh_ref, wrl_ref, br_ref, x1_ref, h2_ref, rt_ref):
    mixed = _mm(ol_ref[...], wov_ref[...]) + _mm(of_ref[...], wof_ref[...])
    x1, h2, rt = _post_common(x_ref[...], mod_ref[0], mixed, gpost_ref[...], gffn_ref[...],
                              wrh_ref[...], wrl_ref[...], br_ref[...])
    x1_ref[...] = x1
    h2_ref[...] = h2.astype(BF16)
    rt_ref[...] = rt


def _post_prompt(x2d, mod3, o_lat, o_fox, wov, wof, gpost, gffn, wr, br, *, seq, tm):
    t = x2d.shape[0]
    tpb = seq // tm
    row = lambda w: pl.BlockSpec((tm, w), lambda i: (i, 0))
    return pl.pallas_call(
        _post_prompt_kernel,
        grid=(t // tm,),
        in_specs=[row(D_MODEL),
                  pl.BlockSpec((1, 1, 6 * D_MODEL), lambda i: (i // tpb, 0, 0)),
                  row(MLA_HEADS * KV_LORA), row(FOX_W),
                  _full(wov.shape), _full(wof.shape), _full(gpost.shape), _full(gffn.shape),
                  _full(wr[0].shape), _full(wr[1].shape), _full(br.shape)],
        out_specs=[row(D_MODEL), row(D_MODEL), row(LANES)],
        out_shape=[jax.ShapeDtypeStruct((t, D_MODEL), F32),
                   jax.ShapeDtypeStruct((t, D_MODEL), BF16),
                   jax.ShapeDtypeStruct((t, LANES), F32)],
        compiler_params=_cparams(("parallel",), VMEM_LIMIT),
        name="post_prompt",
    )(x2d, mod3, o_lat, o_fox, wov, wof, gpost, gffn, wr[0], wr[1], br)


def _post_sample_kernel(x_ref, mod_ref, ol_ref, of_ref, wovh_ref, wovl_ref, wofh_ref, wofl_ref,
                        gpost_ref, gffn_ref, wrh_ref, wrl_ref, br_ref, x1_ref, h2_ref, rt_ref):
    mixed = (_mm3(ol_ref[...], wovh_ref[...], wovl_ref[...])
             + _mm3(of_ref[...], wofh_ref[...], wofl_ref[...]))
    x1, h2, rt = _post_common(x_ref[...], mod_ref[...], mixed, gpost_ref[...], gffn_ref[...],
                              wrh_ref[...], wrl_ref[...], br_ref[...])
    x1_ref[...] = x1
    h2_ref[...] = h2
    rt_ref[...] = rt


def _post_sample(x2d, mod_tok, o_lat, o_fox, wov, wof, gpost, gffn, wr, br):
    t = x2d.shape[0]
    ins = [x2d, mod_tok, o_lat, o_fox, *wov, *wof, gpost, gffn, *wr, br]
    shapes = [(t, D_MODEL), (t, D_MODEL), (t, LANES)]
    return pl.pallas_call(
        _post_sample_kernel,
        grid=(1,),
        in_specs=[_full(a.shape) for a in ins],
        out_specs=[_full(s) for s in shapes],
        out_shape=[jax.ShapeDtypeStruct(s, F32) for s in shapes],
        compiler_params=_cparams(("arbitrary",), VMEM_LIMIT),
        name="post_sample",
    )(*ins)


def _gmm_kernel(te_ref, nu_ref, x_ref, wg_ref, wu_ref, wd_ref, o_ref):
    i = pl.program_id(0)

    @pl.when(i < nu_ref[0])
    def _():
        x = x_ref[...]
        a = _mm(x, wg_ref[0].astype(BF16))
        u = _mm(x, wu_ref[0].astype(BF16))
        act = (_silu(a) * u).astype(BF16)
        o_ref[...] = _mm(act, wd_ref[0].astype(BF16))

    @pl.when(i >= nu_ref[0])
    def _():
        o_ref[...] = jnp.zeros_like(o_ref)


def _gmm(tile_expert, n_used, xs, wg, wu, wd, *, tm):
    nr = xs.shape[0]
    grid_spec = pltpu.PrefetchScalarGridSpec(
        num_scalar_prefetch=2,
        grid=(nr // tm,),
        in_specs=[pl.BlockSpec((tm, D_MODEL), lambda i, te, nu: (i, 0)),
                  pl.BlockSpec((1, D_MODEL, D_EXPERT), lambda i, te, nu: (te[i], 0, 0)),
                  pl.BlockSpec((1, D_MODEL, D_EXPERT), lambda i, te, nu: (te[i], 0, 0)),
                  pl.BlockSpec((1, D_EXPERT, D_MODEL), lambda i, te, nu: (te[i], 0, 0))],
        out_specs=pl.BlockSpec((tm, D_MODEL), lambda i, te, nu: (i, 0)),
    )
    return pl.pallas_call(
        _gmm_kernel,
        grid_spec=grid_spec,
        out_shape=jax.ShapeDtypeStruct((nr, D_MODEL), F32),
        compiler_params=_cparams(("arbitrary",), VMEM_LIMIT),
        name="moe_gmm",
    )(tile_expert, n_used, xs, wg, wu, wd)


def _final_prompt_kernel(x1_ref, mod_ref, y_ref, rt_ref, g_ref, o_ref):
    gate_f = mod_ref[0][:, 5 * D_MODEL:6 * D_MODEL]
    rt = rt_ref[...]
    y = rt[:, 2:3] * y_ref[:, 0:D_MODEL] + rt[:, 3:4] * y_ref[:, D_MODEL:2 * D_MODEL]
    o_ref[...] = x1_ref[...] + gate_f * _rms(y, g_ref[...])


def _final_prompt(x1, mod3, y2, route, gpost, *, seq, tm):
    t = x1.shape[0]
    tpb = seq // tm
    row = lambda w: pl.BlockSpec((tm, w), lambda i: (i, 0))
    return pl.pallas_call(
        _final_prompt_kernel,
        grid=(t // tm,),
        in_specs=[row(D_MODEL),
                  pl.BlockSpec((1, 1, 6 * D_MODEL), lambda i: (i // tpb, 0, 0)),
                  row(2 * D_MODEL), row(LANES), _full(gpost.shape)],
        out_specs=row(D_MODEL),
        out_shape=jax.ShapeDtypeStruct((t, D_MODEL), F32),
        compiler_params=_cparams(("parallel",), VMEM_LIMIT),
        name="final_prompt",
    )(x1, mod3, y2, route, gpost)


def _moe_sample_kernel(h2_ref, rt_ref, x1_ref, mod_ref, g_ref, wg_ref, wu_ref, wd_ref,
                       o_ref, acc_ref):
    e = pl.program_id(0)

    @pl.when(e == 0)
    def _():
        acc_ref[...] = jnp.zeros_like(acc_ref)

    h2 = h2_ref[...]
    rt = rt_ref[...]
    ef = e.astype(F32)
    cw = (jnp.where(rt[:, 0:1] == ef, rt[:, 2:3], 0.0)
          + jnp.where(rt[:, 1:2] == ef, rt[:, 3:4], 0.0))
    a = _mm3(h2, *_split(wg_ref[0]))
    u = _mm3(h2, *_split(wu_ref[0]))
    act = _silu(a) * u * cw
    acc_ref[...] += _mm3(act, *_split(wd_ref[0]))

    @pl.when(e == pl.num_programs(0) - 1)
    def _():
        gate_f = mod_ref[:, 5 * D_MODEL:6 * D_MODEL]
        o_ref[...] = x1_ref[...] + gate_f * _rms(acc_ref[...], g_ref[...])


def _moe_sample(h2, route, x1, mod_tok, gpost, wg, wu, wd):
    t = h2.shape[0]
    return pl.pallas_call(
        _moe_sample_kernel,
        grid=(N_EXPERTS,),
        in_specs=[_full(h2.shape), _full(route.shape), _full(x1.shape), _full(mod_tok.shape),
                  _full(gpost.shape),
                  pl.BlockSpec((1, D_MODEL, D_EXPERT), lambda e: (e, 0, 0)),
                  pl.BlockSpec((1, D_MODEL, D_EXPERT), lambda e: (e, 0, 0)),
                  pl.BlockSpec((1, D_EXPERT, D_MODEL), lambda e: (e, 0, 0))],
        out_specs=_full((t, D_MODEL)),
        out_shape=jax.ShapeDtypeStruct((t, D_MODEL), F32),
        scratch_shapes=[pltpu.VMEM((t, D_MODEL), F32)],
        compiler_params=_cparams(("arbitrary",), VMEM_LIMIT),
        name="moe_sample",
    )(h2, route, x1, mod_tok, gpost, wg, wu, wd)


def _rev_cumsum_lanes(x):
    lane = lax.broadcasted_iota(jnp.int32, x.shape, 1)
    sh = 1
    while sh < LANES:
        x = x + jnp.where(lane + sh < LANES, pltpu.roll(x, LANES - sh, 1), 0.0)
        sh *= 2
    return x


def _fwd_cumsum_lanes(x):
    lane = lax.broadcasted_iota(jnp.int32, x.shape, 1)
    sh = 1
    while sh < LANES:
        x = x + jnp.where(lane >= sh, pltpu.roll(x, sh, 1), 0.0)
        sh *= 2
    return x


def _seg_update(s, m_ref, l_ref):
    m_prev = m_ref[...]
    m_new = jnp.maximum(m_prev, jnp.max(s, axis=1, keepdims=True))
    alpha = jnp.exp2(m_prev - m_new)
    p = jnp.exp2(s - m_new)
    l_ref[...] = alpha * l_ref[...] + jnp.sum(p, axis=1, keepdims=True)
    m_ref[...] = m_new
    return p, alpha


def _col_to_lanes(col, fill=0.0):
    n = col.shape[0]
    row = lax.broadcasted_iota(jnp.int32, (n, LANES), 0)
    lane = lax.broadcasted_iota(jnp.int32, (n, LANES), 1)
    vec = jnp.sum(jnp.where(row == lane, col, 0.0), axis=0, keepdims=True)
    return jnp.where(lane[0:1, :] < n, vec, fill)


def _sample_attn_kernel(pt_ref, qa_ref, qr_ref, qf_ref, latn_ref, krn_ref, fkn_ref, fvn_ref,
                        lfn_ref, *rest, pages, n_chunks):
    npg = pages
    lat_refs = rest[0:npg]
    kr_refs = rest[npg:2 * npg]
    fk_refs = rest[2 * npg:3 * npg]
    fv_refs = rest[3 * npg:4 * npg]
    lf_refs = rest[4 * npg:5 * npg]
    om_ref, of_ref = rest[5 * npg:5 * npg + 2]
    mm_ref, lm_ref, accm_ref, mf_ref, lfs_ref, accf_ref, carry_ref = rest[5 * npg + 2:]
    c = pl.program_id(1)
    nrow = qa_ref.shape[1]

    @pl.when(c == 0)
    def _():
        mm_ref[...] = jnp.full(mm_ref.shape, NEG, F32)
        mf_ref[...] = jnp.full(mf_ref.shape, NEG, F32)
        lm_ref[...] = jnp.zeros(lm_ref.shape, F32)
        lfs_ref[...] = jnp.zeros(lfs_ref.shape, F32)
        accm_ref[...] = jnp.zeros(accm_ref.shape, F32)
        accf_ref[...] = jnp.zeros(accf_ref.shape, F32)
        carry_ref[...] = jnp.zeros(carry_ref.shape, F32)

    qa = qa_ref[0]
    qr = qr_ref[0][:, 0:QK_ROPE]
    qf = qf_ref[0]

    def attend(lat, krt, fkt, fvt, bias_f, bias_m):
        lat_hi, lat_lo = _split(lat)
        s_m = _mm3_nt(qa, lat_hi, lat_lo) + _mm3(qr, *_split(krt))
        if bias_m is not None:
            s_m = s_m + bias_m
        p_m, a_m = _seg_update(s_m, mm_ref, lm_ref)
        accm_ref[...] = a_m * accm_ref[...] + _mm3(p_m, lat_hi, lat_lo)
        s_f = _mm3(qf, *_split(fkt)) + bias_f
        p_f, a_f = _seg_update(s_f, mf_ref, lfs_ref)
        p_hi, p_lo = _split(p_f)
        fv_hi, fv_lo = _split(fvt)
        upd = _mm_nt(fv_hi, p_hi) + _mm_nt(fv_hi, p_lo) + _mm_nt(fv_lo, p_hi)
        pad = jnp.zeros((upd.shape[0], LANES - nrow), F32)
        accf_ref[...] = (_col_to_lanes(a_f) * accf_ref[...]
                         + jnp.concatenate([upd, pad], axis=1))

    sufs = [None] * npg
    carry = carry_ref[...]
    for p in range(npg - 1, -1, -1):
        lf = lf_refs[p][0, 0]
        inc = _rev_cumsum_lanes(lf)
        sufs[p] = (inc - lf + carry) * LOG2E
        carry = carry + inc[:, 0:1]
    carry_ref[...] = jnp.broadcast_to(carry[:, 0:1], carry_ref.shape)
    suf = jnp.concatenate(sufs, axis=1)
    bias_f = jnp.concatenate([suf] * (nrow // FOX_HEADS), axis=0)
    lat = jnp.concatenate([r[0, 0] for r in lat_refs], axis=0)
    krt = jnp.concatenate([r[0, 0] for r in kr_refs], axis=1)
    fkt = jnp.concatenate([r[0, 0].reshape(FOX_W, LANES) for r in fk_refs], axis=1)
    fvt = jnp.concatenate([r[0, 0].reshape(FOX_W, LANES) for r in fv_refs], axis=1)
    attend(lat, krt, fkt, fvt, bias_f, None)

    @pl.when(c == n_chunks - 1)
    def _():
        row = lax.broadcasted_iota(jnp.int32, (nrow, LANES), 0)
        lane = lax.broadcasted_iota(jnp.int32, (nrow, LANES), 1)
        ok = lane <= (row >> 3)
        cum_new = _fwd_cumsum_lanes(lfn_ref[0]) * LOG2E
        b_f = jnp.where(ok, -jnp.concatenate([cum_new] * (nrow // FOX_HEADS), axis=0), NEG)
        b_m = jnp.where(ok, 0.0, NEG)
        attend(latn_ref[0], krn_ref[0], fkn_ref[0], fvn_ref[0], b_f, b_m)
        om_ref[0] = accm_ref[...] / lm_ref[...]
        of_ref[0] = accf_ref[...] / _col_to_lanes(lfs_ref[...], 1.0)


def _sample_attn(page_table, qa, qr, qf, latn, krn, fkn, fvn, lfn,
                 c_lat, c_kr, c_fk, c_fv, c_lf, *, pages):
    nb, n_pages = page_table.shape
    n_chunks = n_pages // pages
    nrow = qa.shape[1]

    def per_b(shape):
        nd = len(shape)
        return pl.BlockSpec((1,) + shape[1:], lambda b, c, pt: (b,) + (0,) * (nd - 1))

    def page_spec(arr, p):
        nd = arr.ndim
        blk = (1, 1) + arr.shape[2:]

        def imap(b, c, pt, p=p):
            return (pt[b, (n_chunks - 1 - c) * pages + p],) + (0,) * (nd - 1)
        return pl.BlockSpec(blk, imap)

    small = [qa, qr, qf, latn, krn, fkn, fvn, lfn]
    in_specs = [per_b(a.shape) for a in small]
    operands = list(small)
    for arr in (c_lat, c_kr, c_fk, c_fv, c_lf):
        for p in range(pages):
            in_specs.append(page_spec(arr, p))
            operands.append(arr)
    grid_spec = pltpu.PrefetchScalarGridSpec(
        num_scalar_prefetch=1,
        grid=(nb, n_chunks),
        in_specs=in_specs,
        out_specs=[pl.BlockSpec((1, nrow, KV_LORA), lambda b, c, pt: (b, 0, 0)),
                   pl.BlockSpec((1, FOX_W, LANES), lambda b, c, pt: (b, 0, 0))],
        scratch_shapes=[pltpu.VMEM((nrow, 1), F32), pltpu.VMEM((nrow, 1), F32),
                        pltpu.VMEM((nrow, KV_LORA), F32),
                        pltpu.VMEM((nrow, 1), F32), pltpu.VMEM((nrow, 1), F32),
                        pltpu.VMEM((FOX_W, LANES), F32),
                        pltpu.VMEM((FOX_HEADS, LANES), F32)],
    )
    return pl.pallas_call(
        functools.partial(_sample_attn_kernel, pages=pages, n_chunks=n_chunks),
        grid_spec=grid_spec,
        out_shape=[jax.ShapeDtypeStruct((nb, nrow, KV_LORA), F32),
                   jax.ShapeDtypeStruct((nb, FOX_W, LANES), F32)],
        compiler_params=_cparams(("arbitrary", "arbitrary"), VMEM_LIMIT),
        name="sample_attn",
    )(page_table, *operands)


def _hl(w):
    hi = w.astype(BF16)
    return hi, (w - hi.astype(F32)).astype(BF16)


def _rope_tables(pos):
    half = QK_ROPE // 2
    inv = ROPE_THETA ** (-jnp.arange(half, dtype=F32) / half)
    ang = pos.astype(F32)[:, None] * inv[None, :]
    cc = jnp.concatenate([jnp.cos(ang), jnp.cos(ang)], axis=1)
    ss = jnp.concatenate([jnp.sin(ang), jnp.sin(ang)], axis=1)
    n = pos.shape[0]
    qs = MLA_SCALE * LOG2E
    ctab = jnp.concatenate([jnp.full((n, KV_LORA), qs, F32), cc * qs,
                            jnp.zeros((n, QHEAD_W - KV_LORA - QK_ROPE), F32)], axis=1)
    stab = jnp.concatenate([jnp.zeros((n, KV_LORA), F32), ss * qs,
                            jnp.zeros((n, QHEAD_W - KV_LORA - QK_ROPE), F32)], axis=1)
    zpad = jnp.zeros((n, LANES - QK_ROPE), F32)
    return ctab, stab, jnp.concatenate([cc, zpad], axis=1), jnp.concatenate([ss, zpad], axis=1)


def _swap_halves(w):
    half = QK_ROPE // 2
    return jnp.concatenate([-w[..., half:], w[..., :half]], axis=-1)


def kernel(x_prompt, x_sample, cache_mla_latent, cache_mla_krope, cache_fox_k, cache_fox_v,
           cache_fox_logf, page_table, c_prompt, c_sample, w_ada, b_ada, g_pre_mix, g_post_mix,
           g_pre_ffn, g_post_ffn, w_in, b_forget, g_q_lora, g_kv_lora, w_uq, w_uk, w_uv, w_o,
           w_group_router, b_group_router, w_expert_router, b_expert_router, w_gate, w_up, w_down):
    assert w_ada.shape[0] == 1, "single-layer trunk"
    batch, seq, d = x_prompt.shape
    dec_batch, n_new, _ = x_sample.shape
    n_pages = page_table.shape[1]
    page = cache_mla_latent.shape[2]
    past_len = n_pages * page
    tp = batch * seq
    ts = dec_batch * n_new

    w_in0 = w_in[0]
    s0, s1, s2, s3, s4, s5 = 256, 384, 416, 928, 1440, 1952
    w_kr = w_in0[:, s1:s2]
    z = lambda n: jnp.zeros((d, n), F32)
    w_all = jnp.concatenate([
        w_in0[:, 0:s0],
        w_in0[:, s0:s1], w_kr, z(96),
        _swap_halves(w_kr), z(96),
        w_in0[:, s2:s3], w_in0[:, s3:s4], w_in0[:, s4:s5],
        w_in0[:, s5:], z(120)], axis=1)
    w_uq0 = w_uq[0]
    w_qabs = _fold(jnp.transpose(w_uq0[:, :, :QK_NOPE], (1, 0, 2)),
                   jnp.transpose(w_uk[0], (0, 2, 1)))
    w_qr = jnp.transpose(w_uq0[:, :, QK_NOPE:], (1, 0, 2))
    zq = lambda n: jnp.zeros((MLA_HEADS, Q_LORA, n), F32)
    w_a = jnp.concatenate([w_qabs, w_qr, zq(96)], axis=2)
    w_b = jnp.concatenate([zq(128), _swap_halves(w_qr), zq(96)], axis=2)
    w_a = jnp.transpose(w_a, (1, 0, 2)).reshape(Q_LORA, MLA_HEADS * QHEAD_W)
    w_b = jnp.transpose(w_b, (1, 0, 2)).reshape(Q_LORA, MLA_HEADS * QHEAD_W)
    w_o0 = w_o[0]
    w_ov = _fold(w_uv[0], w_o0[:MLA_HEADS * V_HEAD].reshape(MLA_HEADS, V_HEAD, d))
    w_ov = w_ov.reshape(MLA_HEADS * KV_LORA, d)
    w_of = w_o0[MLA_HEADS * V_HEAD:]
    w_r = jnp.concatenate([w_group_router[0], w_expert_router[0].reshape(d, N_EXPERTS),
                           jnp.zeros((d, LANES - N_GROUPS - N_EXPERTS), F32)], axis=1)
    b_r = jnp.concatenate([b_group_router[0], b_expert_router[0].reshape(N_EXPERTS),
                           jnp.zeros((LANES - N_GROUPS - N_EXPERTS,), F32)])[None, :]
    bfg = jnp.concatenate([b_forget[0], jnp.zeros((LANES - FOX_HEADS,), F32)])[None, :]
    w_all_hl, w_a_hl, w_b_hl = _hl(w_all), _hl(w_a), _hl(w_b)
    w_ov_hl, w_of_hl, w_r_hl = _hl(w_ov), _hl(w_of), _hl(w_r)
    wg = w_gate[0].reshape(N_EXPERTS, d, D_EXPERT)
    wu = w_up[0].reshape(N_EXPERTS, d, D_EXPERT)
    wd = w_down[0].reshape(N_EXPERTS, D_EXPERT, d)

    c_all = jnp.concatenate([c_prompt, c_sample], axis=0)
    c_all = jnp.pad(c_all, ((0, -c_all.shape[0] % 64), (0, 0)))
    mod = _ada(c_all, w_ada[0], b_ada)
    mod_p = mod[:batch].reshape(batch, 1, 6 * d)
    mod_s = jnp.repeat(mod[batch:batch + dec_batch], n_new, axis=0)

    tm = 256
    xp2 = x_prompt.reshape(tp, d)
    tabs_p = _rope_tables(jnp.arange(seq))
    (q_p, kmla_p, ckv_p, kr_p, fq_p, fk_p, fk16_p, fv_p, fv16_p, lf_p, cum_p) = _proj_prompt(
        xp2, mod_p, g_pre_mix, g_q_lora, g_kv_lora, bfg, tabs_p,
        w_all_hl[0], w_a_hl[0], w_b_hl[0], seq=seq, tm=tm)
    tk = 512
    o_lat_p = _mla_prompt(q_p, kmla_p, batch=batch, seq=seq, tq=128, tk=tk)
    ck4 = jnp.transpose(cum_p[:, :FOX_HEADS].reshape(batch, seq, FOX_HEADS), (0, 2, 1))
    ck4 = ck4.reshape(batch, FOX_HEADS, seq // tk, tk)
    o_fox_p = _fox_prompt(fq_p, fk16_p, fv16_p, ck4, batch=batch, seq=seq, tq=512, tk=tk)
    x1_p, h2_p, rt_p = _post_prompt(xp2, mod_p, o_lat_p, o_fox_p, w_ov_hl[0], w_of_hl[0],
                                    g_post_mix, g_pre_ffn, w_r_hl, b_r, seq=seq, tm=tm)

    tme = 256
    n_asg = 2 * tp
    n_rows = n_asg + N_EXPERTS * tme
    e_flat = rt_p[:, 0:2].astype(jnp.int32).reshape(n_asg)
    order = jnp.argsort(e_flat, stable=True).astype(jnp.int32)
    cnt = jnp.sum(e_flat[:, None] == jnp.arange(N_EXPERTS, dtype=jnp.int32)[None, :], axis=0,
                  dtype=jnp.int32)
    pcnt = ((cnt + tme - 1) // tme) * tme
    offs = jnp.cumsum(pcnt) - pcnt
    cstart = jnp.cumsum(cnt) - cnt
    tile_start = jnp.arange(n_rows // tme, dtype=jnp.int32) * tme
    tile_e = jnp.sum(tile_start[:, None] >= (offs + pcnt)[None, :], axis=1, dtype=jnp.int32)
    n_used = (jnp.sum(pcnt) // tme).astype(jnp.int32)[None]
    tile_e = jnp.minimum(tile_e, N_EXPERTS - 1)
    r = jnp.arange(n_rows, dtype=jnp.int32)
    e_r = tile_e[r // tme]
    k_r = r - offs[e_r]
    valid = (k_r < cnt[e_r]) & (r < jnp.sum(pcnt))
    src_asg = order[jnp.clip(cstart[e_r] + k_r, 0, n_asg - 1)]
    src_tok = jnp.where(valid, src_asg // 2, 0)
    sorted_idx = jnp.arange(n_asg, dtype=jnp.int32)
    e_sorted = e_flat[order]
    pos_sorted = offs[e_sorted] + (sorted_idx - cstart[e_sorted])
    pos = jnp.zeros((n_asg,), jnp.int32).at[order].set(pos_sorted)
    xs = jnp.take(h2_p, src_tok, axis=0)
    ys = _gmm(tile_e, n_used, xs, wg, wu, wd, tm=tme)
    y2 = jnp.take(ys, pos, axis=0).reshape(tp, 2 * d)
    y_prompt = _final_prompt(x1_p, mod_p, y2, rt_p, g_post_ffn, seq=seq, tm=tm)

    xs2 = x_sample.reshape(ts, d)
    pos_s = jnp.tile(past_len + jnp.arange(n_new), dec_batch)
    tabs_s = _rope_tables(pos_s)
    q_s, ckv_s, krp_s, fq_s, fk_s, fv_s, lfp_s = _proj_sample(
        xs2, mod_s, g_pre_mix, g_q_lora, g_kv_lora, bfg, tabs_s, w_all_hl, w_a_hl, w_b_hl)
    q4 = q_s.reshape(dec_batch, n_new, MLA_HEADS, QHEAD_W)
    nrow = n_new * MLA_HEADS
    qa = q4[..., :KV_LORA].reshape(dec_batch, nrow, KV_LORA)
    qr = q4[..., KV_LORA:].reshape(dec_batch, nrow, QHEAD_W - KV_LORA)
    head_mask = (jnp.arange(FOX_W)[None, :] // FOX_HEAD_DIM == jnp.arange(FOX_HEADS)[:, None])
    qf = (fq_s.reshape(dec_batch, n_new, 1, FOX_W) * head_mask[None, None].astype(F32))
    qf = qf.reshape(dec_batch, nrow, FOX_W)
    padk = page - n_new
    latn = jnp.pad(ckv_s.reshape(dec_batch, n_new, KV_LORA), ((0, 0), (0, padk), (0, 0)))
    krn = jnp.pad(jnp.transpose(krp_s[:, :QK_ROPE].reshape(dec_batch, n_new, QK_ROPE), (0, 2, 1)),
                  ((0, 0), (0, 0), (0, padk)))
    fkn = jnp.pad(jnp.transpose(fk_s.reshape(dec_batch, n_new, FOX_W), (0, 2, 1)),
                  ((0, 0), (0, 0), (0, padk)))
    fvn = jnp.pad(jnp.transpose(fv_s.reshape(dec_batch, n_new, FOX_W), (0, 2, 1)),
                  ((0, 0), (0, 0), (0, padk)))
    lfn = jnp.pad(jnp.transpose(lfp_s[:, :FOX_HEADS].reshape(dec_batch, n_new, FOX_HEADS),
                                (0, 2, 1)), ((0, 0), (0, 0), (0, padk)))
    c_lat = cache_mla_latent
    c_kr = jnp.transpose(cache_mla_krope, (0, 1, 3, 2))
    c_fk = jnp.transpose(cache_fox_k, (0, 1, 3, 4, 2))
    c_fv = jnp.transpose(cache_fox_v, (0, 1, 3, 4, 2))
    c_lf = jnp.transpose(cache_fox_logf, (0, 1, 3, 2))
    om, oft = _sample_attn(page_table, qa, qr, qf, latn, krn, fkn, fvn, lfn,
                           c_lat, c_kr, c_fk, c_fv, c_lf, pages=8)
    o_lat_s = om.reshape(dec_batch, n_new, MLA_HEADS * KV_LORA).reshape(ts, MLA_HEADS * KV_LORA)
    oft4 = oft[:, :, :nrow].reshape(dec_batch, FOX_HEADS, FOX_HEAD_DIM, n_new, FOX_HEADS)
    o_fox_s = jnp.einsum('bhdqh->bqhd', oft4).reshape(ts, FOX_W)
    x1_s, h2_s, rt_s = _post_sample(xs2, mod_s, o_lat_s, o_fox_s, w_ov_hl, w_of_hl,
                                    g_post_mix, g_pre_ffn, w_r_hl, b_r)
    y_sample = _moe_sample(h2_s, rt_s, x1_s, mod_s, g_post_ffn, wg, wu, wd)

    return (y_prompt.reshape(batch, seq, d), y_sample.reshape(dec_batch, n_new, d),
            ckv_p.reshape(batch, 1, seq, KV_LORA), kr_p.reshape(batch, 1, seq, QK_ROPE),
            fk_p.reshape(batch, 1, seq, FOX_HEADS, FOX_HEAD_DIM),
            fv_p.reshape(batch, 1, seq, FOX_HEADS, FOX_HEAD_DIM),
            lf_p.reshape(batch, 1, seq, FOX_HEADS),
            ckv_s.reshape(dec_batch, 1, n_new, KV_LORA),
            krp_s[:, :QK_ROPE].reshape(dec_batch, 1, n_new, QK_ROPE),
            fk_s.reshape(dec_batch, 1, n_new, FOX_HEADS, FOX_HEAD_DIM),
            fv_s.reshape(dec_batch, 1, n_new, FOX_HEADS, FOX_HEAD_DIM),
            lfp_s[:, :FOX_HEADS].reshape(dec_batch, 1, n_new, FOX_HEADS))
```

```python
import functools
import math

import jax
import jax.numpy as jnp
import numpy as np
from jax import lax
from jax.experimental import pallas as pl
from jax.experimental.pallas import tpu as pltpu

F32 = jnp.float32
BF16 = jnp.bfloat16

D_MODEL = 1024
MLA_HEADS = 8
QK_NOPE = 64
QK_ROPE = 32
V_HEAD = 64
Q_LORA = 256
KV_LORA = 128
FOX_HEADS = 8
FOX_HEAD_DIM = 64
FOX_W = FOX_HEADS * FOX_HEAD_DIM
N_GROUPS = 4
EXPERTS_PER_GROUP = 8
N_EXPERTS = N_GROUPS * EXPERTS_PER_GROUP
D_EXPERT = 256
ROPE_THETA = 10000.0
EPS = 1e-6
NEG = -1e30
LOG2E = 1.4426950408889634
MLA_SCALE = (QK_NOPE + QK_ROPE) ** -0.5
FOX_SCALE = FOX_HEAD_DIM ** -0.5

LANES = 128
QHEAD_W = 256
W_ALL_COLS = 2304
VMEM_LIMIT = 56 * 1024 * 1024


def _split(x):
    hi = x.astype(BF16)
    lo = (x - hi.astype(F32)).astype(BF16)
    return hi, lo


def _mm(a, b):
    return jnp.dot(a, b, preferred_element_type=F32)


def _mm_nt(a, b):
    return lax.dot_general(a, b, (((1,), (1,)), ((), ())), preferred_element_type=F32)


def _mm3(a, b_hi, b_lo):
    a_hi, a_lo = _split(a)
    m = a.shape[0]
    top = _mm(jnp.concatenate([a_hi, a_lo], axis=0), b_hi)
    return top[:m] + top[m:] + _mm(a_hi, b_lo)


def _mm3_nt(a, b_hi, b_lo):
    a_hi, a_lo = _split(a)
    m = a.shape[0]
    top = _mm_nt(jnp.concatenate([a_hi, a_lo], axis=0), b_hi)
    return top[:m] + top[m:] + _mm_nt(a_hi, b_lo)


def _rms(x, g):
    return x * lax.rsqrt(jnp.mean(x * x, axis=-1, keepdims=True) + EPS) * g


def _silu(x):
    return x / (1.0 + jnp.exp(-x))


def _log_sigmoid(x):
    return jnp.minimum(x, 0.0) - jnp.log(1.0 + jnp.exp(-jnp.abs(x)))


def _cparams(sem, vmem=None):
    return pltpu.CompilerParams(dimension_semantics=sem, vmem_limit_bytes=vmem)


def _full(shape):
    nd = len(shape)
    return pl.BlockSpec(shape, lambda *_: (0,) * nd)


def _ada_kernel(c_ref, w_ref, b_ref, o_ref):
    a = _silu(c_ref[...])
    w_hi, w_lo = _split(w_ref[...])
    o_ref[...] = _mm3(a, w_hi, w_lo) + b_ref[...]


def _ada(c_all, w_ada, b_ada):
    n, d = c_all.shape
    cols = w_ada.shape[1]
    tn = 1024
    return pl.pallas_call(
        _ada_kernel,
        grid=(cols // tn,),
        in_specs=[_full((n, d)),
                  pl.BlockSpec((d, tn), lambda j: (0, j)),
                  pl.BlockSpec((1, tn), lambda j: (0, j))],
        out_specs=pl.BlockSpec((n, tn), lambda j: (0, j)),
        out_shape=jax.ShapeDtypeStruct((n, cols), F32),
        compiler_params=_cparams(("parallel",)),
        name="ada_mod",
    )(c_all, w_ada, b_ada)


def _fold_kernel(a_ref, b_ref, o_ref):
    b_hi, b_lo = _split(b_ref[0])
    o_ref[0] = _mm3(a_ref[0], b_hi, b_lo)


def _fold(a, b):
    h, m, k = a.shape
    n = b.shape[2]
    return pl.pallas_call(
        _fold_kernel,
        grid=(h,),
        in_specs=[pl.BlockSpec((1, m, k), lambda i: (i, 0, 0)),
                  pl.BlockSpec((1, k, n), lambda i: (i, 0, 0))],
        out_specs=pl.BlockSpec((1, m, n), lambda i: (i, 0, 0)),
        out_shape=jax.ShapeDtypeStruct((h, m, n), F32),
        compiler_params=_cparams(("parallel",)),
        name="fold_weights",
    )(a, b)


def _proj_common(x, mod, g_pre, g_q, g_kv, bfg, tabs, mm_all, mm_qa, mm_qb):
    ctab, stab, kct, kst = tabs
    shift = mod[:, 0:D_MODEL]
    scale = mod[:, D_MODEL:2 * D_MODEL]
    h = _rms(x, g_pre) * (1.0 + scale) + shift
    z = mm_all(h)
    cq = _rms(z[:, 0:Q_LORA], g_q)
    nh = MLA_HEADS
    q = (mm_qa(cq) * jnp.concatenate([ctab] * nh, axis=1)
         + mm_qb(cq) * jnp.concatenate([stab] * nh, axis=1))
    c_kv = _rms(z[:, 256:384], g_kv)
    krp = z[:, 384:512] * kct + z[:, 512:640] * kst
    fq = z[:, 640:1152] * (FOX_SCALE * LOG2E)
    fk = z[:, 1152:1664]
    fv = z[:, 1664:2176]
    logf = _log_sigmoid(z[:, 2176:2304] + bfg)
    return q, c_kv, krp, fq, fk, fv, logf


def _proj_prompt_kernel(x_ref, mod_ref, gpre_ref, gq_ref, gkv_ref, bfg_ref,
                        ct_ref, st_ref, kct_ref, kst_ref,
                        wall_ref, wa_ref, wb_ref,
                        q_ref, kmla_ref, ckv_ref, kr_ref, fq_ref, fk_ref, fk16_ref,
                        fv_ref, fv16_ref, lf_ref, cum_ref, carry_ref, *, tiles_per_batch):
    i = pl.program_id(0)
    tm = x_ref.shape[0]
    q, c_kv, krp, fq, fk, fv, logf = _proj_common(
        x_ref[...], mod_ref[0], gpre_ref[...], gq_ref[...], gkv_ref[...], bfg_ref[...],
        (ct_ref[...], st_ref[...], kct_ref[...], kst_ref[...]),
        lambda h: _mm(h.astype(BF16), wall_ref[...]),
        lambda c: _mm(c.astype(BF16), wa_ref[...]),
        lambda c: _mm(c.astype(BF16), wb_ref[...]))
    q_ref[...] = q.astype(BF16)
    ckv_ref[...] = c_kv
    kr_ref[...] = krp[:, 0:QK_ROPE]
    kmla_ref[...] = jnp.concatenate([c_kv, krp], axis=1).astype(BF16)
    fq_ref[...] = fq.astype(BF16)
    fk_ref[...] = fk
    fk16_ref[...] = fk.astype(BF16)
    fv_ref[...] = fv
    fv16_ref[...] = fv.astype(BF16)
    lf_ref[...] = logf[:, 0:FOX_HEADS]

    @pl.when(i % tiles_per_batch == 0)
    def _():
        carry_ref[...] = jnp.zeros_like(carry_ref)

    row = lax.broadcasted_iota(jnp.int32, (tm, tm), 0)
    col = lax.broadcasted_iota(jnp.int32, (tm, tm), 1)
    tri = (row >= col).astype(BF16)
    p1 = logf.astype(BF16)
    r1 = logf - p1.astype(F32)
    p2 = r1.astype(BF16)
    p3 = (r1 - p2.astype(F32)).astype(BF16)
    cum = _mm(tri, p1) + _mm(tri, p2) + _mm(tri, p3) + carry_ref[0:1, :]
    carry_ref[0:1, :] = cum[tm - 1:tm, :]
    cum_ref[...] = cum * LOG2E


def _proj_prompt(x2d, mod3, gpre, gq, gkv, bfg, tabs, wall, wa, wb, *, seq, tm):
    t = x2d.shape[0]
    tpb = seq // tm
    row = lambda w: pl.BlockSpec((tm, w), lambda i: (i, 0))
    pos = lambda w: pl.BlockSpec((tm, w), lambda i: (i % tpb, 0))
    out_shapes = [
        jax.ShapeDtypeStruct((t, MLA_HEADS * QHEAD_W), BF16),
        jax.ShapeDtypeStruct((t, QHEAD_W), BF16),
        jax.ShapeDtypeStruct((t, KV_LORA), F32),
        jax.ShapeDtypeStruct((t, QK_ROPE), F32),
        jax.ShapeDtypeStruct((t, FOX_W), BF16),
        jax.ShapeDtypeStruct((t, FOX_W), F32),
        jax.ShapeDtypeStruct((t, FOX_W), BF16),
        jax.ShapeDtypeStruct((t, FOX_W), F32),
        jax.ShapeDtypeStruct((t, FOX_W), BF16),
        jax.ShapeDtypeStruct((t, FOX_HEADS), F32),
        jax.ShapeDtypeStruct((t, LANES), F32),
    ]
    out_specs = [row(MLA_HEADS * QHEAD_W), row(QHEAD_W), row(KV_LORA), row(QK_ROPE),
                 row(FOX_W), row(FOX_W), row(FOX_W), row(FOX_W), row(FOX_W),
                 row(FOX_HEADS), row(LANES)]
    return pl.pallas_call(
        functools.partial(_proj_prompt_kernel, tiles_per_batch=tpb),
        grid=(t // tm,),
        in_specs=[row(D_MODEL),
                  pl.BlockSpec((1, 1, 6 * D_MODEL), lambda i: (i // tpb, 0, 0)),
                  _full(gpre.shape), _full(gq.shape), _full(gkv.shape), _full(bfg.shape),
                  pos(QHEAD_W), pos(QHEAD_W), pos(LANES), pos(LANES),
                  _full(wall.shape), _full(wa.shape), _full(wb.shape)],
        out_specs=out_specs,
        out_shape=out_shapes,
        scratch_shapes=[pltpu.VMEM((8, LANES), F32)],
        compiler_params=_cparams(("arbitrary",), VMEM_LIMIT),
        name="proj_prompt",
    )(x2d, mod3, gpre, gq, gkv, bfg, *tabs, wall, wa, wb)


def _proj_sample_kernel(x_ref, mod_ref, gpre_ref, gq_ref, gkv_ref, bfg_ref,
                        ct_ref, st_ref, kct_ref, kst_ref,
                        wall_hi, wall_lo, wa_hi, wa_lo, wb_hi, wb_lo,
                        q_ref, ckv_ref, krp_ref, fq_ref, fk_ref, fv_ref, lf_ref):
    q, c_kv, krp, fq, fk, fv, logf = _proj_common(
        x_ref[...], mod_ref[...], gpre_ref[...], gq_ref[...], gkv_ref[...], bfg_ref[...],
        (ct_ref[...], st_ref[...], kct_ref[...], kst_ref[...]),
        lambda h: _mm3(h, wall_hi[...], wall_lo[...]),
        lambda c: _mm3(c, wa_hi[...], wa_lo[...]),
        lambda c: _mm3(c, wb_hi[...], wb_lo[...]))
    q_ref[...] = q
    ckv_ref[...] = c_kv
    krp_ref[...] = krp
    fq_ref[...] = fq
    fk_ref[...] = fk
    fv_ref[...] = fv
    lf_ref[...] = logf


def _proj_sample(x2d, mod_tok, gpre, gq, gkv, bfg, tabs, wall, wa, wb):
    t = x2d.shape[0]
    ins = [x2d, mod_tok, gpre, gq, gkv, bfg, *tabs, *wall, *wa, *wb]
    shapes = [(t, MLA_HEADS * QHEAD_W), (t, KV_LORA), (t, LANES), (t, FOX_W), (t, FOX_W),
              (t, FOX_W), (t, LANES)]
    return pl.pallas_call(
        _proj_sample_kernel,
        grid=(1,),
        in_specs=[_full(a.shape) for a in ins],
        out_specs=[_full(s) for s in shapes],
        out_shape=[jax.ShapeDtypeStruct(s, F32) for s in shapes],
        compiler_params=_cparams(("arbitrary",), VMEM_LIMIT),
        name="proj_sample",
    )(*ins)


def _flash_update(s, v1, m_ref, acc_ref):
    m_prev = m_ref[...]
    m_new = jnp.maximum(m_prev, jnp.max(s, axis=1, keepdims=True))
    alpha = jnp.exp2(m_prev - m_new)
    p = jnp.exp2(s - jnp.concatenate([m_new] * (s.shape[1] // LANES), axis=1))
    acc_ref[...] = (jnp.concatenate([alpha, alpha], axis=1) * acc_ref[...]
                    + _mm(p.astype(BF16), v1))
    m_ref[...] = m_new


def _with_ones(v):
    return jnp.concatenate([v, jnp.ones(v.shape, v.dtype)], axis=1)


def _causal_bounds(i, tq, tk):
    n_full = (i * tq + 1) // tk
    n_all = (i * tq + tq + tk - 1) // tk
    return n_full, n_all


def _mla_prompt_kernel(q_ref, k_ref, o_ref, m_ref, acc_ref, *, tq, tk, chunk):
    i = pl.program_id(1)
    nh = MLA_HEADS
    qs = jnp.concatenate([q_ref[:, h * QHEAD_W:(h + 1) * QHEAD_W] for h in range(nh)], axis=0)
    m_ref[...] = jnp.full(m_ref.shape, NEG, F32)
    acc_ref[...] = jnp.zeros(acc_ref.shape, F32)

    def step(j, masked):
        kb = k_ref[pl.ds(pl.multiple_of(j * tk, tk), tk), :]
        vb = _with_ones(kb[:, 0:KV_LORA])
        s_all = _mm_nt(qs, kb)
        for c in range(nh * tq // chunk):
            rows = slice(c * chunk, (c + 1) * chunk)
            s = s_all[rows]
            if masked:
                row = lax.broadcasted_iota(jnp.int32, s.shape, 0)
                col = lax.broadcasted_iota(jnp.int32, s.shape, 1)
                s = jnp.where(j * tk + col <= i * tq + ((c * chunk + row) & (tq - 1)), s, NEG)
            _flash_update(s, vb, m_ref.at[rows], acc_ref.at[rows])

    n_full, n_all = _causal_bounds(i, tq, tk)
    lax.fori_loop(0, n_full, lambda j, c: (step(j, False), c)[1], 0)
    lax.fori_loop(n_full, n_all, lambda j, c: (step(j, True), c)[1], 0)
    o = acc_ref[:, 0:KV_LORA] / acc_ref[:, KV_LORA:2 * KV_LORA]
    o_ref[...] = jnp.concatenate([o[h * tq:(h + 1) * tq] for h in range(nh)], axis=1).astype(BF16)


def _mla_prompt(q, kmla, *, batch, seq, tq, tk):
    t = q.shape[0]
    nq = seq // tq
    rows = MLA_HEADS * tq
    return pl.pallas_call(
        functools.partial(_mla_prompt_kernel, tq=tq, tk=tk, chunk=128),
        grid=(batch, nq),
        in_specs=[pl.BlockSpec((tq, MLA_HEADS * QHEAD_W), lambda b, i: (b * nq + i, 0)),
                  pl.BlockSpec((seq, QHEAD_W), lambda b, i: (b, 0))],
        out_specs=pl.BlockSpec((tq, MLA_HEADS * KV_LORA), lambda b, i: (b * nq + i, 0)),
        out_shape=jax.ShapeDtypeStruct((t, MLA_HEADS * KV_LORA), BF16),
        scratch_shapes=[pltpu.VMEM((rows, LANES), F32), pltpu.VMEM((rows, 2 * KV_LORA), F32)],
        compiler_params=_cparams(("parallel", "parallel"), VMEM_LIMIT),
        name="mla_prompt",
    )(q, kmla)


def _fox_prompt_kernel(q_ref, k_ref, v_ref, ck_ref, o_ref, m_ref, acc_ref, *, tq, tk, chunk):
    i = pl.program_id(1)
    lane = lax.broadcasted_iota(jnp.int32, (tq, LANES), 1)
    n_full, n_all = _causal_bounds(i, tq, tk)
    for pair in range(FOX_HEADS // 2):
        cols = slice(pair * LANES, (pair + 1) * LANES)
        qp = q_ref[:, cols]
        o_pair = jnp.zeros((tq, LANES), F32)
        for half in range(2):
            h = 2 * pair + half
            sel = (lane >= FOX_HEAD_DIM) if half else (lane < FOX_HEAD_DIM)
            qm = jnp.where(sel, qp, jnp.zeros_like(qp))
            m_ref[...] = jnp.full(m_ref.shape, NEG, F32)
            acc_ref[...] = jnp.zeros(acc_ref.shape, F32)

            def step(j, masked, h=h, qm=qm, cols=cols):
                keys = pl.ds(pl.multiple_of(j * tk, tk), tk)
                kb = k_ref[keys, cols]
                vb = _with_ones(v_ref[keys, cols])
                ck = ck_ref[0, h, pl.ds(j, 1), :]
                s_all = _mm_nt(qm, kb)
                for c in range(tq // chunk):
                    rows = slice(c * chunk, (c + 1) * chunk)
                    s = s_all[rows] - ck
                    if masked:
                        row = lax.broadcasted_iota(jnp.int32, s.shape, 0)
                        col = lax.broadcasted_iota(jnp.int32, s.shape, 1)
                        s = jnp.where(j * tk + col <= i * tq + c * chunk + row, s, NEG)
                    _flash_update(s, vb, m_ref.at[rows], acc_ref.at[rows])

            lax.fori_loop(0, n_full, lambda j, c, step=step: (step(j, False), c)[1], 0)
            lax.fori_loop(n_full, n_all, lambda j, c, step=step: (step(j, True), c)[1], 0)
            o_pair = jnp.where(sel, acc_ref[:, 0:LANES] / acc_ref[:, LANES:2 * LANES], o_pair)
        o_ref[:, cols] = o_pair.astype(BF16)


def _fox_prompt(fq, fk, fv, ck4, *, batch, seq, tq, tk):
    t = fq.shape[0]
    nq = seq // tq
    nk = seq // tk
    return pl.pallas_call(
        functools.partial(_fox_prompt_kernel, tq=tq, tk=tk, chunk=128),
        grid=(batch, nq),
        in_specs=[pl.BlockSpec((tq, FOX_W), lambda b, i: (b * nq + i, 0)),
                  pl.BlockSpec((seq, FOX_W), lambda b, i: (b, 0)),
                  pl.BlockSpec((seq, FOX_W), lambda b, i: (b, 0)),
                  pl.BlockSpec((1, FOX_HEADS, nk, tk), lambda b, i: (b, 0, 0, 0))],
        out_specs=pl.BlockSpec((tq, FOX_W), lambda b, i: (b * nq + i, 0)),
        out_shape=jax.ShapeDtypeStruct((t, FOX_W), BF16),
        scratch_shapes=[pltpu.VMEM((tq, LANES), F32), pltpu.VMEM((tq, 2 * LANES), F32)],
        compiler_params=_cparams(("parallel", "parallel"), VMEM_LIMIT),
        name="fox_prompt",
    )(fq, fk, fv, ck4)


def _route(r):
    lane = lax.broadcasted_iota(jnp.int32, r.shape, 1)
    lanef = lane.astype(F32)
    big = jnp.float32(1e9)
    gmask = lane < N_GROUPS
    gl = jnp.where(gmask, r, -jnp.inf)
    gm = jnp.max(gl, axis=1, keepdims=True)
    gs = jnp.sum(jnp.where(gmask, jnp.exp(r - gm), 0.0), axis=1, keepdims=True)
    g_w = 1.0 / gs
    g_idx = jnp.min(jnp.where(gl == gm, lanef, big), axis=1, keepdims=True)
    egrp = ((lane - N_GROUPS) >> 3).astype(F32)
    emask = (lane >= N_GROUPS) & (lane < N_GROUPS + N_EXPERTS) & (egrp == g_idx)
    el = jnp.where(emask, r, -jnp.inf)
    em = jnp.max(el, axis=1, keepdims=True)
    ee = jnp.where(emask, jnp.exp(r - em), 0.0)
    p = ee / jnp.sum(ee, axis=1, keepdims=True)
    pm = jnp.where(emask, p, -1.0)
    p1 = jnp.max(pm, axis=1, keepdims=True)
    i1 = jnp.min(jnp.where(pm == p1, lanef, big), axis=1, keepdims=True)
    pm2 = jnp.where(lanef == i1, -1.0, pm)
    p2 = jnp.max(pm2, axis=1, keepdims=True)
    i2 = jnp.min(jnp.where(pm2 == p2, lanef, big), axis=1, keepdims=True)
    den = p1 + p2
    c1 = g_w * (p1 / den)
    c2 = g_w * (p2 / den)
    out = jnp.where(lane == 0, i1 - N_GROUPS,
                    jnp.where(lane == 1, i2 - N_GROUPS,
                              jnp.where(lane == 2, c1, jnp.where(lane == 3, c2, 0.0))))
    return out


def _post_common(x, mod, mixed, gpost, gffn, wr_hi, wr_lo, br):
    gate_m = mod[:, 2 * D_MODEL:3 * D_MODEL]
    shift_f = mod[:, 3 * D_MODEL:4 * D_MODEL]
    scale_f = mod[:, 4 * D_MODEL:5 * D_MODEL]
    x1 = x + gate_m * _rms(mixed, gpost)
    h2 = _rms(x1, gffn) * (1.0 + scale_f) + shift_f
    r = _mm3(h2, wr_hi, wr_lo) + br
    return x1, h2, _route(r)


def _post_prompt_kernel(x_ref, mod_ref, ol_ref, of_ref, wov_ref, wof_ref, gpost_ref, gffn_ref,
                        wrh_ref, wrl_ref, br_ref, x1_ref, h2_ref, rt_ref, cnt_ref, carry_ref):
    i = pl.program_id(0)
    tm = x_ref.shape[0]
    mixed = _mm(ol_ref[...], wov_ref[...]) + _mm(of_ref[...], wof_ref[...])
    x1, h2, rt = _post_common(x_ref[...], mod_ref[0], mixed, gpost_ref[...], gffn_ref[...],
                              wrh_ref[...], wrl_ref[...], br_ref[...])
    x1_ref[...] = x1
    h2_ref[...] = h2

    @pl.when(i == 0)
    def _():
        carry_ref[...] = jnp.zeros_like(carry_ref)

    lane = lax.broadcasted_iota(jnp.int32, rt.shape, 1)
    lanef = lane.astype(F32)
    oh1 = (lanef == rt[:, 0:1]).astype(F32)
    oh2 = (lanef == rt[:, 1:2]).astype(F32)
    row = lax.broadcasted_iota(jnp.int32, (tm, tm), 0)
    col = lax.broadcasted_iota(jnp.int32, (tm, tm), 1)
    before = _mm((row > col).astype(BF16), (oh1 + oh2).astype(BF16)) + carry_ref[0:1, :]
    rank1 = jnp.sum(oh1 * before, axis=1, keepdims=True)
    rank2 = jnp.sum(oh2 * (before + oh1), axis=1, keepdims=True)
    total = before[tm - 1:tm, :] + oh1[tm - 1:tm, :] + oh2[tm - 1:tm, :]
    carry_ref[0:1, :] = total
    cnt_ref[...] = jnp.broadcast_to(total, cnt_ref.shape)
    rt_ref[...] = jnp.where(lane == 4, rank1, jnp.where(lane == 5, rank2, rt))


def _post_prompt(x2d, mod3, o_lat, o_fox, wov, wof, gpost, gffn, wr, br, *, seq, tm):
    t = x2d.shape[0]
    tpb = seq // tm
    row = lambda w: pl.BlockSpec((tm, w), lambda i: (i, 0))
    return pl.pallas_call(
        _post_prompt_kernel,
        grid=(t // tm,),
        in_specs=[row(D_MODEL),
                  pl.BlockSpec((1, 1, 6 * D_MODEL), lambda i: (i // tpb, 0, 0)),
                  row(MLA_HEADS * KV_LORA), row(FOX_W),
                  _full(wov.shape), _full(wof.shape), _full(gpost.shape), _full(gffn.shape),
                  _full(wr[0].shape), _full(wr[1].shape), _full(br.shape)],
        out_specs=[row(D_MODEL), row(D_MODEL), row(LANES), _full((8, LANES))],
        out_shape=[jax.ShapeDtypeStruct((t, D_MODEL), F32),
                   jax.ShapeDtypeStruct((t, D_MODEL), F32),
                   jax.ShapeDtypeStruct((t, LANES), F32),
                   jax.ShapeDtypeStruct((8, LANES), F32)],
        scratch_shapes=[pltpu.VMEM((8, LANES), F32)],
        compiler_params=_cparams(("arbitrary",), VMEM_LIMIT),
        name="post_prompt",
    )(x2d, mod3, o_lat, o_fox, wov, wof, gpost, gffn, wr[0], wr[1], br)


def _post_sample_kernel(x_ref, mod_ref, ol_ref, of_ref, wovh_ref, wovl_ref, wofh_ref, wofl_ref,
                        gpost_ref, gffn_ref, wrh_ref, wrl_ref, br_ref, x1_ref, h2_ref, rt_ref):
    mixed = (_mm3(ol_ref[...], wovh_ref[...], wovl_ref[...])
             + _mm3(of_ref[...], wofh_ref[...], wofl_ref[...]))
    x1, h2, rt = _post_common(x_ref[...], mod_ref[...], mixed, gpost_ref[...], gffn_ref[...],
                              wrh_ref[...], wrl_ref[...], br_ref[...])
    x1_ref[...] = x1
    h2_ref[...] = h2
    rt_ref[...] = rt


def _post_sample(x2d, mod_tok, o_lat, o_fox, wov, wof, gpost, gffn, wr, br):
    t = x2d.shape[0]
    ins = [x2d, mod_tok, o_lat, o_fox, *wov, *wof, gpost, gffn, *wr, br]
    shapes = [(t, D_MODEL), (t, D_MODEL), (t, LANES)]
    return pl.pallas_call(
        _post_sample_kernel,
        grid=(1,),
        in_specs=[_full(a.shape) for a in ins],
        out_specs=[_full(s) for s in shapes],
        out_shape=[jax.ShapeDtypeStruct(s, F32) for s in shapes],
        compiler_params=_cparams(("arbitrary",), VMEM_LIMIT),
        name="post_sample",
    )(*ins)


def _dispatch_kernel(pos_ref, h2_ref, zero_ref, xs_ref, sem):
    del zero_ref
    i = pl.program_id(0)
    tm = h2_ref.shape[0]
    base = i * (2 * tm)

    def body(t, c):
        for k in range(2):
            p = pos_ref[base + 2 * t + k]
            pltpu.make_async_copy(h2_ref.at[pl.ds(t, 1)], xs_ref.at[pl.ds(p, 1)], sem).start()
        return c

    lax.fori_loop(0, tm, body, 0, unroll=8)
    for _ in range(2):
        pltpu.make_async_copy(h2_ref, xs_ref.at[pl.ds(0, tm)], sem).wait()


def _dispatch(pos, h2, n_rows, *, tm):
    t = h2.shape[0]
    grid_spec = pltpu.PrefetchScalarGridSpec(
        num_scalar_prefetch=1,
        grid=(t // tm,),
        in_specs=[pl.BlockSpec((tm, D_MODEL), lambda i, pos: (i, 0)),
                  pl.BlockSpec(memory_space=pl.ANY)],
        out_specs=pl.BlockSpec(memory_space=pl.ANY),
        scratch_shapes=[pltpu.SemaphoreType.DMA],
    )
    return pl.pallas_call(
        _dispatch_kernel,
        grid_spec=grid_spec,
        out_shape=jax.ShapeDtypeStruct((n_rows, D_MODEL), F32),
        input_output_aliases={2: 0},
        compiler_params=_cparams(("arbitrary",), VMEM_LIMIT),
        name="moe_dispatch",
    )(pos, h2, jnp.zeros((n_rows, D_MODEL), F32))


def _gmm_kernel(te_ref, nu_ref, x_ref, wg_ref, wu_ref, wd_ref, o_ref):
    i = pl.program_id(0)

    @pl.when(i < nu_ref[0])
    def _():
        x = x_ref[...].astype(BF16)
        a = _mm(x, wg_ref[0].astype(BF16))
        u = _mm(x, wu_ref[0].astype(BF16))
        act = (_silu(a) * u).astype(BF16)
        o_ref[...] = _mm(act, wd_ref[0].astype(BF16))

    @pl.when(i >= nu_ref[0])
    def _():
        o_ref[...] = jnp.zeros_like(o_ref)


def _gmm(tile_expert, n_used, xs, wg, wu, wd, *, tm):
    nr = xs.shape[0]
    grid_spec = pltpu.PrefetchScalarGridSpec(
        num_scalar_prefetch=2,
        grid=(nr // tm,),
        in_specs=[pl.BlockSpec((tm, D_MODEL), lambda i, te, nu: (i, 0)),
                  pl.BlockSpec((1, D_MODEL, D_EXPERT), lambda i, te, nu: (te[i], 0, 0)),
                  pl.BlockSpec((1, D_MODEL, D_EXPERT), lambda i, te, nu: (te[i], 0, 0)),
                  pl.BlockSpec((1, D_EXPERT, D_MODEL), lambda i, te, nu: (te[i], 0, 0))],
        out_specs=pl.BlockSpec((tm, D_MODEL), lambda i, te, nu: (i, 0)),
    )
    return pl.pallas_call(
        _gmm_kernel,
        grid_spec=grid_spec,
        out_shape=jax.ShapeDtypeStruct((nr, D_MODEL), F32),
        compiler_params=_cparams(("arbitrary",), VMEM_LIMIT),
        name="moe_gmm",
    )(tile_expert, n_used, xs, wg, wu, wd)


def _combine_kernel(pos_ref, x1_ref, mod_ref, rt_ref, g_ref, ys_ref, o_ref, buf_ref, sem):
    i = pl.program_id(0)
    n = pl.num_programs(0)
    tm = x1_ref.shape[0]

    def issue(tile, slot):
        base = tile * (2 * tm)

        def body(t, c):
            for k in range(2):
                p = pos_ref[base + 2 * t + k]
                pltpu.make_async_copy(ys_ref.at[pl.ds(p, 1)], buf_ref.at[slot, k, pl.ds(t, 1)],
                                      sem.at[slot]).start()
            return c

        lax.fori_loop(0, tm, body, 0, unroll=8)

    @pl.when(i == 0)
    def _():
        issue(0, 0)

    @pl.when(i + 1 < n)
    def _():
        issue(i + 1, (i + 1) % 2)

    slot = i % 2
    for k in range(2):
        pltpu.make_async_copy(ys_ref.at[pl.ds(0, tm)], buf_ref.at[slot, k], sem.at[slot]).wait()
    gate_f = mod_ref[0][:, 5 * D_MODEL:6 * D_MODEL]
    rt = rt_ref[...]
    y = rt[:, 2:3] * buf_ref[slot, 0] + rt[:, 3:4] * buf_ref[slot, 1]
    o_ref[...] = x1_ref[...] + gate_f * _rms(y, g_ref[...])


def _combine(pos, x1, mod3, route, gpost, ys, *, seq, tm):
    t = x1.shape[0]
    tpb = seq // tm
    row = lambda w: pl.BlockSpec((tm, w), lambda i, pos: (i, 0))
    grid_spec = pltpu.PrefetchScalarGridSpec(
        num_scalar_prefetch=1,
        grid=(t // tm,),
        in_specs=[row(D_MODEL),
                  pl.BlockSpec((1, 1, 6 * D_MODEL), lambda i, pos: (i // tpb, 0, 0)),
                  row(LANES),
                  pl.BlockSpec(gpost.shape, lambda i, pos: (0, 0)),
                  pl.BlockSpec(memory_space=pl.ANY)],
        out_specs=row(D_MODEL),
        scratch_shapes=[pltpu.VMEM((2, 2, tm, D_MODEL), F32), pltpu.SemaphoreType.DMA((2,))],
    )
    return pl.pallas_call(
        _combine_kernel,
        grid_spec=grid_spec,
        out_shape=jax.ShapeDtypeStruct((t, D_MODEL), F32),
        compiler_params=_cparams(("arbitrary",), VMEM_LIMIT),
        name="moe_combine",
    )(pos, x1, mod3, route, gpost, ys)


def _moe_sample_kernel(h2_ref, rt_ref, x1_ref, mod_ref, g_ref, wg_ref, wu_ref, wd_ref,
                       o_ref, acc_ref):
    e = pl.program_id(0)

    @pl.when(e == 0)
    def _():
        acc_ref[...] = jnp.zeros_like(acc_ref)

    h2 = h2_ref[...]
    rt = rt_ref[...]
    ef = e.astype(F32)
    cw = (jnp.where(rt[:, 0:1] == ef, rt[:, 2:3], 0.0)
          + jnp.where(rt[:, 1:2] == ef, rt[:, 3:4], 0.0))
    a = _mm3(h2, *_split(wg_ref[0]))
    u = _mm3(h2, *_split(wu_ref[0]))
    act = _silu(a) * u * cw
    acc_ref[...] += _mm3(act, *_split(wd_ref[0]))

    @pl.when(e == pl.num_programs(0) - 1)
    def _():
        gate_f = mod_ref[:, 5 * D_MODEL:6 * D_MODEL]
        o_ref[...] = x1_ref[...] + gate_f * _rms(acc_ref[...], g_ref[...])


def _moe_sample(h2, route, x1, mod_tok, gpost, wg, wu, wd):
    t = h2.shape[0]
    return pl.pallas_call(
        _moe_sample_kernel,
        grid=(N_EXPERTS,),
        in_specs=[_full(h2.shape), _full(route.shape), _full(x1.shape), _full(mod_tok.shape),
                  _full(gpost.shape),
                  pl.BlockSpec((1, D_MODEL, D_EXPERT), lambda e: (e, 0, 0)),
                  pl.BlockSpec((1, D_MODEL, D_EXPERT), lambda e: (e, 0, 0)),
                  pl.BlockSpec((1, D_EXPERT, D_MODEL), lambda e: (e, 0, 0))],
        out_specs=_full((t, D_MODEL)),
        out_shape=jax.ShapeDtypeStruct((t, D_MODEL), F32),
        scratch_shapes=[pltpu.VMEM((t, D_MODEL), F32)],
        compiler_params=_cparams(("arbitrary",), VMEM_LIMIT),
        name="moe_sample",
    )(h2, route, x1, mod_tok, gpost, wg, wu, wd)


def _rev_cumsum_lanes(x):
    lane = lax.broadcasted_iota(jnp.int32, x.shape, 1)
    sh = 1
    while sh < LANES:
        x = x + jnp.where(lane + sh < LANES, pltpu.roll(x, LANES - sh, 1), 0.0)
        sh *= 2
    return x


def _fwd_cumsum_lanes(x):
    lane = lax.broadcasted_iota(jnp.int32, x.shape, 1)
    sh = 1
    while sh < LANES:
        x = x + jnp.where(lane >= sh, pltpu.roll(x, sh, 1), 0.0)
        sh *= 2
    return x


def _seg_update(s, m_ref, l_ref):
    m_prev = m_ref[...]
    m_new = jnp.maximum(m_prev, jnp.max(s, axis=1, keepdims=True))
    alpha = jnp.exp2(m_prev - m_new)
    p = jnp.exp2(s - m_new)
    l_ref[...] = alpha * l_ref[...] + jnp.sum(p, axis=1, keepdims=True)
    m_ref[...] = m_new
    return p, alpha


def _col_to_lanes(col, fill=0.0):
    n = col.shape[0]
    row = lax.broadcasted_iota(jnp.int32, (n, LANES), 0)
    lane = lax.broadcasted_iota(jnp.int32, (n, LANES), 1)
    vec = jnp.sum(jnp.where(row == lane, col, 0.0), axis=0, keepdims=True)
    return jnp.where(lane[0:1, :] < n, vec, fill)


def _sample_attn_kernel(pt_ref, qa_ref, qr_ref, qf_ref, latn_ref, krn_ref, fkn_ref, fvn_ref,
                        lfn_ref, *rest, pages, n_chunks):
    npg = pages
    lat_refs = rest[0:npg]
    kr_refs = rest[npg:2 * npg]
    fk_refs = rest[2 * npg:3 * npg]
    fv_refs = rest[3 * npg:4 * npg]
    lf_refs = rest[4 * npg:5 * npg]
    om_ref, of_ref = rest[5 * npg:5 * npg + 2]
    mm_ref, lm_ref, accm_ref, mf_ref, lfs_ref, accf_ref, carry_ref = rest[5 * npg + 2:]
    c = pl.program_id(1)
    nrow = qa_ref.shape[1]

    @pl.when(c == 0)
    def _():
        mm_ref[...] = jnp.full(mm_ref.shape, NEG, F32)
        mf_ref[...] = jnp.full(mf_ref.shape, NEG, F32)
        lm_ref[...] = jnp.zeros(lm_ref.shape, F32)
        lfs_ref[...] = jnp.zeros(lfs_ref.shape, F32)
        accm_ref[...] = jnp.zeros(accm_ref.shape, F32)
        accf_ref[...] = jnp.zeros(accf_ref.shape, F32)
        carry_ref[...] = jnp.zeros(carry_ref.shape, F32)

    qa = qa_ref[0]
    qr = qr_ref[0][:, 0:QK_ROPE]
    qf = qf_ref[0]

    def attend(lat, krt, fkt, fvt, bias_f, bias_m):
        lat_hi, lat_lo = _split(lat)
        s_m = _mm3_nt(qa, lat_hi, lat_lo) + _mm3(qr, *_split(krt))
        if bias_m is not None:
            s_m = s_m + bias_m
        p_m, a_m = _seg_update(s_m, mm_ref, lm_ref)
        accm_ref[...] = a_m * accm_ref[...] + _mm3(p_m, lat_hi, lat_lo)
        s_f = _mm3(qf, *_split(fkt)) + bias_f
        p_f, a_f = _seg_update(s_f, mf_ref, lfs_ref)
        p_hi, p_lo = _split(p_f)
        fv_hi, fv_lo = _split(fvt)
        upd = _mm_nt(fv_hi, p_hi) + _mm_nt(fv_hi, p_lo) + _mm_nt(fv_lo, p_hi)
        pad = jnp.zeros((upd.shape[0], LANES - nrow), F32)
        accf_ref[...] = (_col_to_lanes(a_f) * accf_ref[...]
                         + jnp.concatenate([upd, pad], axis=1))

    sufs = [None] * npg
    carry = carry_ref[...]
    for p in range(npg - 1, -1, -1):
        lf = lf_refs[p][0, 0]
        inc = _rev_cumsum_lanes(lf)
        sufs[p] = (inc - lf + carry) * LOG2E
        carry = carry + inc[:, 0:1]
    carry_ref[...] = jnp.broadcast_to(carry[:, 0:1], carry_ref.shape)
    suf = jnp.concatenate(sufs, axis=1)
    bias_f = jnp.concatenate([suf] * (nrow // FOX_HEADS), axis=0)
    lat = jnp.concatenate([r[0, 0] for r in lat_refs], axis=0)
    krt = jnp.concatenate([r[0, 0] for r in kr_refs], axis=1)
    fkt = jnp.concatenate([r[0, 0].reshape(FOX_W, LANES) for r in fk_refs], axis=1)
    fvt = jnp.concatenate([r[0, 0].reshape(FOX_W, LANES) for r in fv_refs], axis=1)
    attend(lat, krt, fkt, fvt, bias_f, None)

    @pl.when(c == n_chunks - 1)
    def _():
        row = lax.broadcasted_iota(jnp.int32, (nrow, LANES), 0)
        lane = lax.broadcasted_iota(jnp.int32, (nrow, LANES), 1)
        ok = lane <= (row >> 3)
        cum_new = _fwd_cumsum_lanes(lfn_ref[0]) * LOG2E
        b_f = jnp.where(ok, -jnp.concatenate([cum_new] * (nrow // FOX_HEADS), axis=0), NEG)
        b_m = jnp.where(ok, 0.0, NEG)
        attend(latn_ref[0], krn_ref[0], fkn_ref[0], fvn_ref[0], b_f, b_m)
        om_ref[0] = accm_ref[...] / lm_ref[...]
        of_ref[0] = accf_ref[...] / _col_to_lanes(lfs_ref[...], 1.0)


def _sample_attn(page_table, qa, qr, qf, latn, krn, fkn, fvn, lfn,
                 c_lat, c_kr, c_fk, c_fv, c_lf, *, pages):
    nb, n_pages = page_table.shape
    n_chunks = n_pages // pages
    nrow = qa.shape[1]

    def per_b(shape):
        nd = len(shape)
        return pl.BlockSpec((1,) + shape[1:], lambda b, c, pt: (b,) + (0,) * (nd - 1))

    def page_spec(arr, p):
        nd = arr.ndim
        blk = (1, 1) + arr.shape[2:]

        def imap(b, c, pt, p=p):
            return (pt[b, (n_chunks - 1 - c) * pages + p],) + (0,) * (nd - 1)
        return pl.BlockSpec(blk, imap)

    small = [qa, qr, qf, latn, krn, fkn, fvn, lfn]
    in_specs = [per_b(a.shape) for a in small]
    operands = list(small)
    for arr in (c_lat, c_kr, c_fk, c_fv, c_lf):
        for p in range(pages):
            in_specs.append(page_spec(arr, p))
            operands.append(arr)
    grid_spec = pltpu.PrefetchScalarGridSpec(
        num_scalar_prefetch=1,
        grid=(nb, n_chunks),
        in_specs=in_specs,
        out_specs=[pl.BlockSpec((1, nrow, KV_LORA), lambda b, c, pt: (b, 0, 0)),
                   pl.BlockSpec((1, FOX_W, LANES), lambda b, c, pt: (b, 0, 0))],
        scratch_shapes=[pltpu.VMEM((nrow, 1), F32), pltpu.VMEM((nrow, 1), F32),
                        pltpu.VMEM((nrow, KV_LORA), F32),
                        pltpu.VMEM((nrow, 1), F32), pltpu.VMEM((nrow, 1), F32),
                        pltpu.VMEM((FOX_W, LANES), F32),
                        pltpu.VMEM((FOX_HEADS, LANES), F32)],
    )
    return pl.pallas_call(
        functools.partial(_sample_attn_kernel, pages=pages, n_chunks=n_chunks),
        grid_spec=grid_spec,
        out_shape=[jax.ShapeDtypeStruct((nb, nrow, KV_LORA), F32),
                   jax.ShapeDtypeStruct((nb, FOX_W, LANES), F32)],
        compiler_params=_cparams(("arbitrary", "arbitrary"), VMEM_LIMIT),
        name="sample_attn",
    )(page_table, *operands)


def _hl(w):
    hi = w.astype(BF16)
    return hi, (w - hi.astype(F32)).astype(BF16)


def _rope_tables(pos):
    half = QK_ROPE // 2
    inv = ROPE_THETA ** (-jnp.arange(half, dtype=F32) / half)
    ang = pos.astype(F32)[:, None] * inv[None, :]
    cc = jnp.concatenate([jnp.cos(ang), jnp.cos(ang)], axis=1)
    ss = jnp.concatenate([jnp.sin(ang), jnp.sin(ang)], axis=1)
    n = pos.shape[0]
    qs = MLA_SCALE * LOG2E
    ctab = jnp.concatenate([jnp.full((n, KV_LORA), qs, F32), cc * qs,
                            jnp.zeros((n, QHEAD_W - KV_LORA - QK_ROPE), F32)], axis=1)
    stab = jnp.concatenate([jnp.zeros((n, KV_LORA), F32), ss * qs,
                            jnp.zeros((n, QHEAD_W - KV_LORA - QK_ROPE), F32)], axis=1)
    zpad = jnp.zeros((n, LANES - QK_ROPE), F32)
    return ctab, stab, jnp.concatenate([cc, zpad], axis=1), jnp.concatenate([ss, zpad], axis=1)


def _swap_halves(w):
    half = QK_ROPE // 2
    return jnp.concatenate([-w[..., half:], w[..., :half]], axis=-1)


def kernel(x_prompt, x_sample, cache_mla_latent, cache_mla_krope, cache_fox_k, cache_fox_v,
           cache_fox_logf, page_table, c_prompt, c_sample, w_ada, b_ada, g_pre_mix, g_post_mix,
           g_pre_ffn, g_post_ffn, w_in, b_forget, g_q_lora, g_kv_lora, w_uq, w_uk, w_uv, w_o,
           w_group_router, b_group_router, w_expert_router, b_expert_router, w_gate, w_up, w_down):
    assert w_ada.shape[0] == 1, "single-layer trunk"
    batch, seq, d = x_prompt.shape
    dec_batch, n_new, _ = x_sample.shape
    n_pages = page_table.shape[1]
    page = cache_mla_latent.shape[2]
    past_len = n_pages * page
    tp = batch * seq
    ts = dec_batch * n_new

    w_in0 = w_in[0]
    s0, s1, s2, s3, s4, s5 = 256, 384, 416, 928, 1440, 1952
    w_kr = w_in0[:, s1:s2]
    z = lambda n: jnp.zeros((d, n), F32)
    w_all = jnp.concatenate([
        w_in0[:, 0:s0],
        w_in0[:, s0:s1], w_kr, z(96),
        _swap_halves(w_kr), z(96),
        w_in0[:, s2:s3], w_in0[:, s3:s4], w_in0[:, s4:s5],
        w_in0[:, s5:], z(120)], axis=1)
    w_uq0 = w_uq[0]
    w_qabs = _fold(jnp.transpose(w_uq0[:, :, :QK_NOPE], (1, 0, 2)),
                   jnp.transpose(w_uk[0], (0, 2, 1)))
    w_qr = jnp.transpose(w_uq0[:, :, QK_NOPE:], (1, 0, 2))
    zq = lambda n: jnp.zeros((MLA_HEADS, Q_LORA, n), F32)
    w_a = jnp.concatenate([w_qabs, w_qr, zq(96)], axis=2)
    w_b = jnp.concatenate([zq(128), _swap_halves(w_qr), zq(96)], axis=2)
    w_a = jnp.transpose(w_a, (1, 0, 2)).reshape(Q_LORA, MLA_HEADS * QHEAD_W)
    w_b = jnp.transpose(w_b, (1, 0, 2)).reshape(Q_LORA, MLA_HEADS * QHEAD_W)
    w_o0 = w_o[0]
    w_ov = _fold(w_uv[0], w_o0[:MLA_HEADS * V_HEAD].reshape(MLA_HEADS, V_HEAD, d))
    w_ov = w_ov.reshape(MLA_HEADS * KV_LORA, d)
    w_of = w_o0[MLA_HEADS * V_HEAD:]
    w_r = jnp.concatenate([w_group_router[0], w_expert_router[0].reshape(d, N_EXPERTS),
                           jnp.zeros((d, LANES - N_GROUPS - N_EXPERTS), F32)], axis=1)
    b_r = jnp.concatenate([b_group_router[0], b_expert_router[0].reshape(N_EXPERTS),
                           jnp.zeros((LANES - N_GROUPS - N_EXPERTS,), F32)])[None, :]
    bfg = jnp.concatenate([b_forget[0], jnp.zeros((LANES - FOX_HEADS,), F32)])[None, :]
    w_all_hl, w_a_hl, w_b_hl = _hl(w_all), _hl(w_a), _hl(w_b)
    w_ov_hl, w_of_hl, w_r_hl = _hl(w_ov), _hl(w_of), _hl(w_r)
    wg = w_gate[0].reshape(N_EXPERTS, d, D_EXPERT)
    wu = w_up[0].reshape(N_EXPERTS, d, D_EXPERT)
    wd = w_down[0].reshape(N_EXPERTS, D_EXPERT, d)

    c_all = jnp.concatenate([c_prompt, c_sample], axis=0)
    c_all = jnp.pad(c_all, ((0, -c_all.shape[0] % 64), (0, 0)))
    mod = _ada(c_all, w_ada[0], b_ada)
    mod_p = mod[:batch].reshape(batch, 1, 6 * d)
    mod_s = jnp.repeat(mod[batch:batch + dec_batch], n_new, axis=0)

    tm = 256
    xp2 = x_prompt.reshape(tp, d)
    tabs_p = _rope_tables(jnp.arange(seq))
    (q_p, kmla_p, ckv_p, kr_p, fq_p, fk_p, fk16_p, fv_p, fv16_p, lf_p, cum_p) = _proj_prompt(
        xp2, mod_p, g_pre_mix, g_q_lora, g_kv_lora, bfg, tabs_p,
        w_all_hl[0], w_a_hl[0], w_b_hl[0], seq=seq, tm=tm)
    tk = 512
    o_lat_p = _mla_prompt(q_p, kmla_p, batch=batch, seq=seq, tq=128, tk=tk)
    ck4 = jnp.transpose(cum_p[:, :FOX_HEADS].reshape(batch, seq, FOX_HEADS), (0, 2, 1))
    ck4 = ck4.reshape(batch, FOX_HEADS, seq // tk, tk)
    o_fox_p = _fox_prompt(fq_p, fk16_p, fv16_p, ck4, batch=batch, seq=seq, tq=512, tk=tk)
    x1_p, h2_p, rt_p, cnt_p = _post_prompt(xp2, mod_p, o_lat_p, o_fox_p, w_ov_hl[0], w_of_hl[0],
                                           g_post_mix, g_pre_ffn, w_r_hl, b_r, seq=seq, tm=tm)

    tme = 256
    n_asg = 2 * tp
    n_rows = n_asg + N_EXPERTS * tme
    cnt = cnt_p[0, :N_EXPERTS].astype(jnp.int32)
    pend = jnp.cumsum(((cnt + tme - 1) // tme) * tme)
    offs = pend - ((cnt + tme - 1) // tme) * tme
    tile_start = jnp.arange(n_rows // tme, dtype=jnp.int32) * tme
    tile_e = jnp.sum(tile_start[:, None] >= pend[None, :], axis=1, dtype=jnp.int32)
    tile_e = jnp.minimum(tile_e, N_EXPERTS - 1)
    n_used = (pend[-1] // tme).astype(jnp.int32)[None]
    e_ids = rt_p[:, 0:2]
    hit = e_ids[:, :, None] == jnp.arange(N_EXPERTS, dtype=F32)[None, None, :]
    offs_tok = jnp.sum(jnp.where(hit, offs[None, None, :], 0), axis=-1, dtype=jnp.int32)
    pos = (offs_tok + rt_p[:, 4:6].astype(jnp.int32)).reshape(n_asg)
    xs = _dispatch(pos, h2_p, n_rows, tm=tm)
    ys = _gmm(tile_e, n_used, xs, wg, wu, wd, tm=tme)
    y_prompt = _combine(pos, x1_p, mod_p, rt_p, g_post_ffn, ys, seq=seq, tm=tm)

    xs2 = x_sample.reshape(ts, d)
    pos_s = jnp.tile(past_len + jnp.arange(n_new), dec_batch)
    tabs_s = _rope_tables(pos_s)
    q_s, ckv_s, krp_s, fq_s, fk_s, fv_s, lfp_s = _proj_sample(
        xs2, mod_s, g_pre_mix, g_q_lora, g_kv_lora, bfg, tabs_s, w_all_hl, w_a_hl, w_b_hl)
    q4 = q_s.reshape(dec_batch, n_new, MLA_HEADS, QHEAD_W)
    nrow = n_new * MLA_HEADS
    qa = q4[..., :KV_LORA].reshape(dec_batch, nrow, KV_LORA)
    qr = q4[..., KV_LORA:].reshape(dec_batch, nrow, QHEAD_W - KV_LORA)
    head_mask = (jnp.arange(FOX_W)[None, :] // FOX_HEAD_DIM == jnp.arange(FOX_HEADS)[:, None])
    qf = (fq_s.reshape(dec_batch, n_new, 1, FOX_W) * head_mask[None, None].astype(F32))
    qf = qf.reshape(dec_batch, nrow, FOX_W)
    padk = page - n_new
    latn = jnp.pad(ckv_s.reshape(dec_batch, n_new, KV_LORA), ((0, 0), (0, padk), (0, 0)))
    krn = jnp.pad(jnp.transpose(krp_s[:, :QK_ROPE].reshape(dec_batch, n_new, QK_ROPE), (0, 2, 1)),
                  ((0, 0), (0, 0), (0, padk)))
    fkn = jnp.pad(jnp.transpose(fk_s.reshape(dec_batch, n_new, FOX_W), (0, 2, 1)),
                  ((0, 0), (0, 0), (0, padk)))
    fvn = jnp.pad(jnp.transpose(fv_s.reshape(dec_batch, n_new, FOX_W), (0, 2, 1)),
                  ((0, 0), (0, 0), (0, padk)))
    lfn = jnp.pad(jnp.transpose(lfp_s[:, :FOX_HEADS].reshape(dec_batch, n_new, FOX_HEADS),
                                (0, 2, 1)), ((0, 0), (0, 0), (0, padk)))
    c_lat = cache_mla_latent
    c_kr = jnp.transpose(cache_mla_krope, (0, 1, 3, 2))
    c_fk = jnp.transpose(cache_fox_k, (0, 1, 3, 4, 2))
    c_fv = jnp.transpose(cache_fox_v, (0, 1, 3, 4, 2))
    c_lf = jnp.transpose(cache_fox_logf, (0, 1, 3, 2))
    om, oft = _sample_attn(page_table, qa, qr, qf, latn, krn, fkn, fvn, lfn,
                           c_lat, c_kr, c_fk, c_fv, c_lf, pages=8)
    o_lat_s = om.reshape(dec_batch, n_new, MLA_HEADS * KV_LORA).reshape(ts, MLA_HEADS * KV_LORA)
    oft4 = oft[:, :, :nrow].reshape(dec_batch, FOX_HEADS, FOX_HEAD_DIM, n_new, FOX_HEADS)
    o_fox_s = jnp.einsum('bhdqh->bqhd', oft4).reshape(ts, FOX_W)
    x1_s, h2_s, rt_s = _post_sample(xs2, mod_s, o_lat_s, o_fox_s, w_ov_hl, w_of_hl,
                                    g_post_mix, g_pre_ffn, w_r_hl, b_r)
    y_sample = _moe_sample(h2_s, rt_s, x1_s, mod_s, g_post_ffn, wg, wu, wd)

    return (y_prompt.reshape(batch, seq, d), y_sample.reshape(dec_batch, n_new, d),
            ckv_p.reshape(batch, 1, seq, KV_LORA), kr_p.reshape(batch, 1, seq, QK_ROPE),
            fk_p.reshape(batch, 1, seq, FOX_HEADS, FOX_HEAD_DIM),
            fv_p.reshape(batch, 1, seq, FOX_HEADS, FOX_HEAD_DIM),
            lf_p.reshape(batch, 1, seq, FOX_HEADS),
            ckv_s.reshape(dec_batch, 1, n_new, KV_LORA),
            krp_s[:, :QK_ROPE].reshape(dec_batch, 1, n_new, QK_ROPE),
            fk_s.reshape(dec_batch, 1, n_new, FOX_HEADS, FOX_HEAD_DIM),
            fv_s.reshape(dec_batch, 1, n_new, FOX_HEADS, FOX_HEAD_DIM),
            lfp_s[:, :FOX_HEADS].reshape(dec_batch, 1, n_new, FOX_HEADS))
```

```python
import functools
import math

import jax
import jax.numpy as jnp
import numpy as np
from jax import lax
from jax.experimental import pallas as pl
from jax.experimental.pallas import tpu as pltpu

F32 = jnp.float32
BF16 = jnp.bfloat16

D_MODEL = 1024
MLA_HEADS = 8
QK_NOPE = 64
QK_ROPE = 32
V_HEAD = 64
Q_LORA = 256
KV_LORA = 128
FOX_HEADS = 8
FOX_HEAD_DIM = 64
FOX_W = FOX_HEADS * FOX_HEAD_DIM
N_GROUPS = 4
EXPERTS_PER_GROUP = 8
N_EXPERTS = N_GROUPS * EXPERTS_PER_GROUP
D_EXPERT = 256
ROPE_THETA = 10000.0
EPS = 1e-6
NEG = -1e30
LOG2E = 1.4426950408889634
MLA_SCALE = (QK_NOPE + QK_ROPE) ** -0.5
FOX_SCALE = FOX_HEAD_DIM ** -0.5

LANES = 128
QHEAD_W = 256
W_ALL_COLS = 2304
VMEM_LIMIT = 56 * 1024 * 1024


def _split(x):
    hi = x.astype(BF16)
    lo = (x - hi.astype(F32)).astype(BF16)
    return hi, lo


def _mm(a, b):
    return jnp.dot(a, b, preferred_element_type=F32)


def _mm_nt(a, b):
    return lax.dot_general(a, b, (((1,), (1,)), ((), ())), preferred_element_type=F32)


def _mm3(a, b_hi, b_lo):
    a_hi, a_lo = _split(a)
    m = a.shape[0]
    top = _mm(jnp.concatenate([a_hi, a_lo], axis=0), b_hi)
    return top[:m] + top[m:] + _mm(a_hi, b_lo)


def _mm3_nt(a, b_hi, b_lo):
    a_hi, a_lo = _split(a)
    m = a.shape[0]
    top = _mm_nt(jnp.concatenate([a_hi, a_lo], axis=0), b_hi)
    return top[:m] + top[m:] + _mm_nt(a_hi, b_lo)


def _rms(x, g):
    return x * lax.rsqrt(jnp.mean(x * x, axis=-1, keepdims=True) + EPS) * g


def _silu(x):
    return x / (1.0 + jnp.exp(-x))


def _log_sigmoid(x):
    return jnp.minimum(x, 0.0) - jnp.log(1.0 + jnp.exp(-jnp.abs(x)))


def _cparams(sem, vmem=None):
    return pltpu.CompilerParams(dimension_semantics=sem, vmem_limit_bytes=vmem)


def _full(shape):
    nd = len(shape)
    return pl.BlockSpec(shape, lambda *_: (0,) * nd)


def _ada_kernel(c_ref, w_ref, b_ref, o_ref):
    a = _silu(c_ref[...])
    w_hi, w_lo = _split(w_ref[...])
    o_ref[...] = _mm3(a, w_hi, w_lo) + b_ref[...]


def _ada(c_all, w_ada, b_ada):
    n, d = c_all.shape
    cols = w_ada.shape[1]
    tn = 1024
    return pl.pallas_call(
        _ada_kernel,
        grid=(cols // tn,),
        in_specs=[_full((n, d)),
                  pl.BlockSpec((d, tn), lambda j: (0, j)),
                  pl.BlockSpec((1, tn), lambda j: (0, j))],
        out_specs=pl.BlockSpec((n, tn), lambda j: (0, j)),
        out_shape=jax.ShapeDtypeStruct((n, cols), F32),
        compiler_params=_cparams(("parallel",)),
        name="ada_mod",
    )(c_all, w_ada, b_ada)


def _fold_kernel(a_ref, b_ref, o_ref):
    b_hi, b_lo = _split(b_ref[0])
    o_ref[0] = _mm3(a_ref[0], b_hi, b_lo)


def _fold(a, b):
    h, m, k = a.shape
    n = b.shape[2]
    return pl.pallas_call(
        _fold_kernel,
        grid=(h,),
        in_specs=[pl.BlockSpec((1, m, k), lambda i: (i, 0, 0)),
                  pl.BlockSpec((1, k, n), lambda i: (i, 0, 0))],
        out_specs=pl.BlockSpec((1, m, n), lambda i: (i, 0, 0)),
        out_shape=jax.ShapeDtypeStruct((h, m, n), F32),
        compiler_params=_cparams(("parallel",)),
        name="fold_weights",
    )(a, b)


def _proj_common(x, mod, g_pre, g_q, g_kv, bfg, tabs, mm_all, mm_qa, mm_qb):
    ctab, stab, kct, kst = tabs
    shift = mod[:, 0:D_MODEL]
    scale = mod[:, D_MODEL:2 * D_MODEL]
    h = _rms(x, g_pre) * (1.0 + scale) + shift
    z = mm_all(h)
    cq = _rms(z[:, 0:Q_LORA], g_q)
    nh = MLA_HEADS
    q = (mm_qa(cq) * jnp.concatenate([ctab] * nh, axis=1)
         + mm_qb(cq) * jnp.concatenate([stab] * nh, axis=1))
    c_kv = _rms(z[:, 256:384], g_kv)
    krp = z[:, 384:512] * kct + z[:, 512:640] * kst
    fq = z[:, 640:1152] * (FOX_SCALE * LOG2E)
    fk = z[:, 1152:1664]
    fv = z[:, 1664:2176]
    logf = _log_sigmoid(z[:, 2176:2304] + bfg)
    return q, c_kv, krp, fq, fk, fv, logf


def _proj_prompt_kernel(x_ref, mod_ref, gpre_ref, gq_ref, gkv_ref, bfg_ref,
                        ct_ref, st_ref, kct_ref, kst_ref,
                        wall_ref, wa_ref, wb_ref,
                        q_ref, kmla_ref, ckv_ref, kr_ref, fq_ref, fk_ref, fk16_ref,
                        fv_ref, fv16_ref, lf_ref, cum_ref, carry_ref, *, tiles_per_batch):
    i = pl.program_id(0)
    tm = x_ref.shape[0]
    q, c_kv, krp, fq, fk, fv, logf = _proj_common(
        x_ref[...], mod_ref[0], gpre_ref[...], gq_ref[...], gkv_ref[...], bfg_ref[...],
        (ct_ref[...], st_ref[...], kct_ref[...], kst_ref[...]),
        lambda h: _mm(h.astype(BF16), wall_ref[...]),
        lambda c: _mm(c.astype(BF16), wa_ref[...]),
        lambda c: _mm(c.astype(BF16), wb_ref[...]))
    q_ref[...] = q.astype(BF16)
    ckv_ref[...] = c_kv
    kr_ref[0] = jnp.transpose(krp)[0:QK_ROPE, :]
    kmla_ref[...] = jnp.concatenate([c_kv, krp], axis=1).astype(BF16)
    fq_ref[...] = fq.astype(BF16)
    fk_ref[0] = jnp.transpose(fk)
    fk16_ref[...] = fk.astype(BF16)
    fv_ref[0] = jnp.transpose(fv)
    fv16_ref[...] = fv.astype(BF16)
    lf_ref[0] = jnp.transpose(logf)[0:FOX_HEADS, :]

    @pl.when(i % tiles_per_batch == 0)
    def _():
        carry_ref[...] = jnp.zeros_like(carry_ref)

    row = lax.broadcasted_iota(jnp.int32, (tm, tm), 0)
    col = lax.broadcasted_iota(jnp.int32, (tm, tm), 1)
    tri = (row >= col).astype(BF16)
    p1 = logf.astype(BF16)
    r1 = logf - p1.astype(F32)
    p2 = r1.astype(BF16)
    p3 = (r1 - p2.astype(F32)).astype(BF16)
    cum = _mm(tri, p1) + _mm(tri, p2) + _mm(tri, p3) + carry_ref[0:1, :]
    carry_ref[0:1, :] = cum[tm - 1:tm, :]
    cum_ref[0] = jnp.transpose(cum * LOG2E)[0:FOX_HEADS, :]


def _proj_prompt(x2d, mod3, gpre, gq, gkv, bfg, tabs, wall, wa, wb, *, seq, tm):
    t = x2d.shape[0]
    tpb = seq // tm
    row = lambda w: pl.BlockSpec((tm, w), lambda i: (i, 0))
    pos = lambda w: pl.BlockSpec((tm, w), lambda i: (i % tpb, 0))
    nb = t // seq
    keym = lambda w: pl.BlockSpec((1, w, tm), lambda i: (i // tpb, 0, i % tpb))
    out_shapes = [
        jax.ShapeDtypeStruct((t, MLA_HEADS * QHEAD_W), BF16),
        jax.ShapeDtypeStruct((t, QHEAD_W), BF16),
        jax.ShapeDtypeStruct((t, KV_LORA), F32),
        jax.ShapeDtypeStruct((nb, QK_ROPE, seq), F32),
        jax.ShapeDtypeStruct((t, FOX_W), BF16),
        jax.ShapeDtypeStruct((nb, FOX_W, seq), F32),
        jax.ShapeDtypeStruct((t, FOX_W), BF16),
        jax.ShapeDtypeStruct((nb, FOX_W, seq), F32),
        jax.ShapeDtypeStruct((t, FOX_W), BF16),
        jax.ShapeDtypeStruct((nb, FOX_HEADS, seq), F32),
        jax.ShapeDtypeStruct((nb, FOX_HEADS, seq), F32),
    ]
    out_specs = [row(MLA_HEADS * QHEAD_W), row(QHEAD_W), row(KV_LORA), keym(QK_ROPE),
                 row(FOX_W), keym(FOX_W), row(FOX_W), keym(FOX_W), row(FOX_W),
                 keym(FOX_HEADS), keym(FOX_HEADS)]
    return pl.pallas_call(
        functools.partial(_proj_prompt_kernel, tiles_per_batch=tpb),
        grid=(t // tm,),
        in_specs=[row(D_MODEL),
                  pl.BlockSpec((1, 1, 6 * D_MODEL), lambda i: (i // tpb, 0, 0)),
                  _full(gpre.shape), _full(gq.shape), _full(gkv.shape), _full(bfg.shape),
                  pos(QHEAD_W), pos(QHEAD_W), pos(LANES), pos(LANES),
                  _full(wall.shape), _full(wa.shape), _full(wb.shape)],
        out_specs=out_specs,
        out_shape=out_shapes,
        scratch_shapes=[pltpu.VMEM((8, LANES), F32)],
        compiler_params=_cparams(("arbitrary",), VMEM_LIMIT),
        name="proj_prompt",
    )(x2d, mod3, gpre, gq, gkv, bfg, *tabs, wall, wa, wb)


def _proj_sample_kernel(x_ref, mod_ref, gpre_ref, gq_ref, gkv_ref, bfg_ref,
                        ct_ref, st_ref, kct_ref, kst_ref,
                        wall_hi, wall_lo, wa_hi, wa_lo, wb_hi, wb_lo,
                        q_ref, ckv_ref, krp_ref, fq_ref, fk_ref, fv_ref, lf_ref):
    q, c_kv, krp, fq, fk, fv, logf = _proj_common(
        x_ref[...], mod_ref[...], gpre_ref[...], gq_ref[...], gkv_ref[...], bfg_ref[...],
        (ct_ref[...], st_ref[...], kct_ref[...], kst_ref[...]),
        lambda h: _mm3(h, wall_hi[...], wall_lo[...]),
        lambda c: _mm3(c, wa_hi[...], wa_lo[...]),
        lambda c: _mm3(c, wb_hi[...], wb_lo[...]))
    q_ref[...] = q
    ckv_ref[...] = c_kv
    krp_ref[...] = krp
    fq_ref[...] = fq
    fk_ref[...] = fk
    fv_ref[...] = fv
    lf_ref[...] = logf


def _proj_sample(x2d, mod_tok, gpre, gq, gkv, bfg, tabs, wall, wa, wb):
    t = x2d.shape[0]
    ins = [x2d, mod_tok, gpre, gq, gkv, bfg, *tabs, *wall, *wa, *wb]
    shapes = [(t, MLA_HEADS * QHEAD_W), (t, KV_LORA), (t, LANES), (t, FOX_W), (t, FOX_W),
              (t, FOX_W), (t, LANES)]
    return pl.pallas_call(
        _proj_sample_kernel,
        grid=(1,),
        in_specs=[_full(a.shape) for a in ins],
        out_specs=[_full(s) for s in shapes],
        out_shape=[jax.ShapeDtypeStruct(s, F32) for s in shapes],
        compiler_params=_cparams(("arbitrary",), VMEM_LIMIT),
        name="proj_sample",
    )(*ins)


def _flash_update(s, v1, m_ref, acc_ref):
    m_prev = m_ref[...]
    m_new = jnp.maximum(m_prev, jnp.max(s, axis=1, keepdims=True))
    alpha = jnp.exp2(m_prev - m_new)
    p = jnp.exp2(s - jnp.concatenate([m_new] * (s.shape[1] // LANES), axis=1))
    acc_ref[...] = (jnp.concatenate([alpha, alpha], axis=1) * acc_ref[...]
                    + _mm(p.astype(BF16), v1))
    m_ref[...] = m_new


def _with_ones(v):
    return jnp.concatenate([v, jnp.ones(v.shape, v.dtype)], axis=1)


def _causal_bounds(i, tq, tk):
    n_full = (i * tq + 1) // tk
    n_all = (i * tq + tq + tk - 1) // tk
    return n_full, n_all


def _mla_prompt_kernel(q_ref, k_ref, o_ref, m_ref, acc_ref, *, tq, tk, chunk):
    i = pl.program_id(1)
    nh = MLA_HEADS
    qs = jnp.concatenate([q_ref[:, h * QHEAD_W:(h + 1) * QHEAD_W] for h in range(nh)], axis=0)
    m_ref[...] = jnp.full(m_ref.shape, NEG, F32)
    acc_ref[...] = jnp.zeros(acc_ref.shape, F32)

    def step(j, masked):
        kb = k_ref[pl.ds(pl.multiple_of(j * tk, tk), tk), :]
        vb = _with_ones(kb[:, 0:KV_LORA])
        s_all = _mm_nt(qs, kb)
        for c in range(nh * tq // chunk):
            rows = slice(c * chunk, (c + 1) * chunk)
            s = s_all[rows]
            if masked:
                row = lax.broadcasted_iota(jnp.int32, s.shape, 0)
                col = lax.broadcasted_iota(jnp.int32, s.shape, 1)
                s = jnp.where(j * tk + col <= i * tq + ((c * chunk + row) & (tq - 1)), s, NEG)
            _flash_update(s, vb, m_ref.at[rows], acc_ref.at[rows])

    n_full, n_all = _causal_bounds(i, tq, tk)
    lax.fori_loop(0, n_full, lambda j, c: (step(j, False), c)[1], 0)
    lax.fori_loop(n_full, n_all, lambda j, c: (step(j, True), c)[1], 0)
    o = acc_ref[:, 0:KV_LORA] / acc_ref[:, KV_LORA:2 * KV_LORA]
    o_ref[...] = jnp.concatenate([o[h * tq:(h + 1) * tq] for h in range(nh)], axis=1).astype(BF16)


def _mla_prompt(q, kmla, *, batch, seq, tq, tk):
    t = q.shape[0]
    nq = seq // tq
    rows = MLA_HEADS * tq
    return pl.pallas_call(
        functools.partial(_mla_prompt_kernel, tq=tq, tk=tk, chunk=128),
        grid=(batch, nq),
        in_specs=[pl.BlockSpec((tq, MLA_HEADS * QHEAD_W), lambda b, i: (b * nq + i, 0)),
                  pl.BlockSpec((seq, QHEAD_W), lambda b, i: (b, 0))],
        out_specs=pl.BlockSpec((tq, MLA_HEADS * KV_LORA), lambda b, i: (b * nq + i, 0)),
        out_shape=jax.ShapeDtypeStruct((t, MLA_HEADS * KV_LORA), BF16),
        scratch_shapes=[pltpu.VMEM((rows, LANES), F32), pltpu.VMEM((rows, 2 * KV_LORA), F32)],
        compiler_params=_cparams(("parallel", "parallel"), VMEM_LIMIT),
        name="mla_prompt",
    )(q, kmla)


def _fox_prompt_kernel(q_ref, k_ref, v_ref, ck_ref, o_ref, m_ref, acc_ref, *, tq, tk, chunk):
    i = pl.program_id(1)
    lane = lax.broadcasted_iota(jnp.int32, (tq, LANES), 1)
    n_full, n_all = _causal_bounds(i, tq, tk)
    for pair in range(FOX_HEADS // 2):
        cols = slice(pair * LANES, (pair + 1) * LANES)
        qp = q_ref[:, cols]
        o_pair = jnp.zeros((tq, LANES), F32)
        for half in range(2):
            h = 2 * pair + half
            sel = (lane >= FOX_HEAD_DIM) if half else (lane < FOX_HEAD_DIM)
            qm = jnp.where(sel, qp, jnp.zeros_like(qp))
            m_ref[...] = jnp.full(m_ref.shape, NEG, F32)
            acc_ref[...] = jnp.zeros(acc_ref.shape, F32)

            def step(j, masked, h=h, qm=qm, cols=cols):
                keys = pl.ds(pl.multiple_of(j * tk, tk), tk)
                kb = k_ref[keys, cols]
                vb = _with_ones(v_ref[keys, cols])
                ck = ck_ref[0, h, pl.ds(j, 1), :]
                s_all = _mm_nt(qm, kb)
                for c in range(tq // chunk):
                    rows = slice(c * chunk, (c + 1) * chunk)
                    s = s_all[rows] - ck
                    if masked:
                        row = lax.broadcasted_iota(jnp.int32, s.shape, 0)
                        col = lax.broadcasted_iota(jnp.int32, s.shape, 1)
                        s = jnp.where(j * tk + col <= i * tq + c * chunk + row, s, NEG)
                    _flash_update(s, vb, m_ref.at[rows], acc_ref.at[rows])

            lax.fori_loop(0, n_full, lambda j, c, step=step: (step(j, False), c)[1], 0)
            lax.fori_loop(n_full, n_all, lambda j, c, step=step: (step(j, True), c)[1], 0)
            o_pair = jnp.where(sel, acc_ref[:, 0:LANES] / acc_ref[:, LANES:2 * LANES], o_pair)
        o_ref[:, cols] = o_pair.astype(BF16)


def _fox_prompt(fq, fk, fv, ck4, *, batch, seq, tq, tk):
    t = fq.shape[0]
    nq = seq // tq
    nk = seq // tk
    return pl.pallas_call(
        functools.partial(_fox_prompt_kernel, tq=tq, tk=tk, chunk=128),
        grid=(batch, nq),
        in_specs=[pl.BlockSpec((tq, FOX_W), lambda b, i: (b * nq + i, 0)),
                  pl.BlockSpec((seq, FOX_W), lambda b, i: (b, 0)),
                  pl.BlockSpec((seq, FOX_W), lambda b, i: (b, 0)),
                  pl.BlockSpec((1, FOX_HEADS, nk, tk), lambda b, i: (b, 0, 0, 0))],
        out_specs=pl.BlockSpec((tq, FOX_W), lambda b, i: (b * nq + i, 0)),
        out_shape=jax.ShapeDtypeStruct((t, FOX_W), BF16),
        scratch_shapes=[pltpu.VMEM((tq, LANES), F32), pltpu.VMEM((tq, 2 * LANES), F32)],
        compiler_params=_cparams(("parallel", "parallel"), VMEM_LIMIT),
        name="fox_prompt",
    )(fq, fk, fv, ck4)


def _route(r):
    lane = lax.broadcasted_iota(jnp.int32, r.shape, 1)
    lanef = lane.astype(F32)
    big = jnp.float32(1e9)
    gmask = lane < N_GROUPS
    gl = jnp.where(gmask, r, -jnp.inf)
    gm = jnp.max(gl, axis=1, keepdims=True)
    gs = jnp.sum(jnp.where(gmask, jnp.exp(r - gm), 0.0), axis=1, keepdims=True)
    g_w = 1.0 / gs
    g_idx = jnp.min(jnp.where(gl == gm, lanef, big), axis=1, keepdims=True)
    egrp = ((lane - N_GROUPS) >> 3).astype(F32)
    emask = (lane >= N_GROUPS) & (lane < N_GROUPS + N_EXPERTS) & (egrp == g_idx)
    el = jnp.where(emask, r, -jnp.inf)
    em = jnp.max(el, axis=1, keepdims=True)
    ee = jnp.where(emask, jnp.exp(r - em), 0.0)
    p = ee / jnp.sum(ee, axis=1, keepdims=True)
    pm = jnp.where(emask, p, -1.0)
    p1 = jnp.max(pm, axis=1, keepdims=True)
    i1 = jnp.min(jnp.where(pm == p1, lanef, big), axis=1, keepdims=True)
    pm2 = jnp.where(lanef == i1, -1.0, pm)
    p2 = jnp.max(pm2, axis=1, keepdims=True)
    i2 = jnp.min(jnp.where(pm2 == p2, lanef, big), axis=1, keepdims=True)
    den = p1 + p2
    c1 = g_w * (p1 / den)
    c2 = g_w * (p2 / den)
    out = jnp.where(lane == 0, i1 - N_GROUPS,
                    jnp.where(lane == 1, i2 - N_GROUPS,
                              jnp.where(lane == 2, c1, jnp.where(lane == 3, c2, 0.0))))
    return out


def _post_common(x, mod, mixed, gpost, gffn, wr_hi, wr_lo, br):
    gate_m = mod[:, 2 * D_MODEL:3 * D_MODEL]
    shift_f = mod[:, 3 * D_MODEL:4 * D_MODEL]
    scale_f = mod[:, 4 * D_MODEL:5 * D_MODEL]
    x1 = x + gate_m * _rms(mixed, gpost)
    h2 = _rms(x1, gffn) * (1.0 + scale_f) + shift_f
    r = _mm3(h2, wr_hi, wr_lo) + br
    return x1, h2, _route(r)


def _post_prompt_kernel(x_ref, mod_ref, ol_ref, of_ref, wov_ref, wof_ref, gpost_ref, gffn_ref,
                        wrh_ref, wrl_ref, br_ref, x1_ref, h2_ref, rt_ref, cnt_ref, carry_ref):
    i = pl.program_id(0)
    tm = x_ref.shape[0]
    mixed = _mm(ol_ref[...], wov_ref[...]) + _mm(of_ref[...], wof_ref[...])
    x1, h2, rt = _post_common(x_ref[...], mod_ref[0], mixed, gpost_ref[...], gffn_ref[...],
                              wrh_ref[...], wrl_ref[...], br_ref[...])
    x1_ref[...] = x1
    h2_ref[...] = h2

    @pl.when(i == 0)
    def _():
        carry_ref[...] = jnp.zeros_like(carry_ref)

    lane = lax.broadcasted_iota(jnp.int32, rt.shape, 1)
    lanef = lane.astype(F32)
    oh1 = (lanef == rt[:, 0:1]).astype(F32)
    oh2 = (lanef == rt[:, 1:2]).astype(F32)
    row = lax.broadcasted_iota(jnp.int32, (tm, tm), 0)
    col = lax.broadcasted_iota(jnp.int32, (tm, tm), 1)
    before = _mm((row > col).astype(BF16), (oh1 + oh2).astype(BF16)) + carry_ref[0:1, :]
    rank1 = jnp.sum(oh1 * before, axis=1, keepdims=True)
    rank2 = jnp.sum(oh2 * (before + oh1), axis=1, keepdims=True)
    total = before[tm - 1:tm, :] + oh1[tm - 1:tm, :] + oh2[tm - 1:tm, :]
    carry_ref[0:1, :] = total
    cnt_ref[...] = jnp.broadcast_to(total, cnt_ref.shape)
    rt_ref[...] = jnp.where(lane == 4, rank1, jnp.where(lane == 5, rank2, rt))


def _post_prompt(x2d, mod3, o_lat, o_fox, wov, wof, gpost, gffn, wr, br, *, seq, tm):
    t = x2d.shape[0]
    tpb = seq // tm
    row = lambda w: pl.BlockSpec((tm, w), lambda i: (i, 0))
    return pl.pallas_call(
        _post_prompt_kernel,
        grid=(t // tm,),
        in_specs=[row(D_MODEL),
                  pl.BlockSpec((1, 1, 6 * D_MODEL), lambda i: (i // tpb, 0, 0)),
                  row(MLA_HEADS * KV_LORA), row(FOX_W),
                  _full(wov.shape), _full(wof.shape), _full(gpost.shape), _full(gffn.shape),
                  _full(wr[0].shape), _full(wr[1].shape), _full(br.shape)],
        out_specs=[row(D_MODEL), row(D_MODEL), row(LANES), _full((8, LANES))],
        out_shape=[jax.ShapeDtypeStruct((t, D_MODEL), F32),
                   jax.ShapeDtypeStruct((t, D_MODEL), F32),
                   jax.ShapeDtypeStruct((t, LANES), F32),
                   jax.ShapeDtypeStruct((8, LANES), F32)],
        scratch_shapes=[pltpu.VMEM((8, LANES), F32)],
        compiler_params=_cparams(("arbitrary",), VMEM_LIMIT),
        name="post_prompt",
    )(x2d, mod3, o_lat, o_fox, wov, wof, gpost, gffn, wr[0], wr[1], br)


def _post_sample_kernel(x_ref, mod_ref, ol_ref, of_ref, wovh_ref, wovl_ref, wofh_ref, wofl_ref,
                        gpost_ref, gffn_ref, wrh_ref, wrl_ref, br_ref, x1_ref, h2_ref, rt_ref):
    mixed = (_mm3(ol_ref[...], wovh_ref[...], wovl_ref[...])
             + _mm3(of_ref[...], wofh_ref[...], wofl_ref[...]))
    x1, h2, rt = _post_common(x_ref[...], mod_ref[...], mixed, gpost_ref[...], gffn_ref[...],
                              wrh_ref[...], wrl_ref[...], br_ref[...])
    x1_ref[...] = x1
    h2_ref[...] = h2
    rt_ref[...] = rt


def _post_sample(x2d, mod_tok, o_lat, o_fox, wov, wof, gpost, gffn, wr, br):
    t = x2d.shape[0]
    ins = [x2d, mod_tok, o_lat, o_fox, *wov, *wof, gpost, gffn, *wr, br]
    shapes = [(t, D_MODEL), (t, D_MODEL), (t, LANES)]
    return pl.pallas_call(
        _post_sample_kernel,
        grid=(1,),
        in_specs=[_full(a.shape) for a in ins],
        out_specs=[_full(s) for s in shapes],
        out_shape=[jax.ShapeDtypeStruct(s, F32) for s in shapes],
        compiler_params=_cparams(("arbitrary",), VMEM_LIMIT),
        name="post_sample",
    )(*ins)


def _dispatch_kernel(pos_ref, h2_ref, zero_ref, xs_ref, sem):
    del zero_ref
    i = pl.program_id(0)
    tm = h2_ref.shape[0]
    base = i * (2 * tm)

    def body(t, c):
        for k in range(2):
            p = pos_ref[base + 2 * t + k]
            pltpu.make_async_copy(h2_ref.at[pl.ds(t, 1)], xs_ref.at[pl.ds(p, 1)], sem).start()
        return c

    lax.fori_loop(0, tm, body, 0, unroll=8)
    for _ in range(2):
        pltpu.make_async_copy(h2_ref, xs_ref.at[pl.ds(0, tm)], sem).wait()


def _dispatch(pos, h2, n_rows, *, tm):
    t = h2.shape[0]
    grid_spec = pltpu.PrefetchScalarGridSpec(
        num_scalar_prefetch=1,
        grid=(t // tm,),
        in_specs=[pl.BlockSpec((tm, D_MODEL), lambda i, pos: (i, 0)),
                  pl.BlockSpec(memory_space=pl.ANY)],
        out_specs=pl.BlockSpec(memory_space=pl.ANY),
        scratch_shapes=[pltpu.SemaphoreType.DMA],
    )
    return pl.pallas_call(
        _dispatch_kernel,
        grid_spec=grid_spec,
        out_shape=jax.ShapeDtypeStruct((n_rows, D_MODEL), F32),
        input_output_aliases={2: 0},
        compiler_params=_cparams(("arbitrary",), VMEM_LIMIT),
        name="moe_dispatch",
    )(pos, h2, jnp.zeros((n_rows, D_MODEL), F32))


def _gmm_kernel(te_ref, nu_ref, x_ref, wg_ref, wu_ref, wd_ref, o_ref):
    i = pl.program_id(0)

    @pl.when(i < nu_ref[0])
    def _():
        x = x_ref[...].astype(BF16)
        a = _mm(x, wg_ref[0].astype(BF16))
        u = _mm(x, wu_ref[0].astype(BF16))
        act = (_silu(a) * u).astype(BF16)
        o_ref[...] = _mm(act, wd_ref[0].astype(BF16))

    @pl.when(i >= nu_ref[0])
    def _():
        o_ref[...] = jnp.zeros_like(o_ref)


def _gmm(tile_expert, n_used, xs, wg, wu, wd, *, tm):
    nr = xs.shape[0]
    grid_spec = pltpu.PrefetchScalarGridSpec(
        num_scalar_prefetch=2,
        grid=(nr // tm,),
        in_specs=[pl.BlockSpec((tm, D_MODEL), lambda i, te, nu: (i, 0)),
                  pl.BlockSpec((1, D_MODEL, D_EXPERT), lambda i, te, nu: (te[i], 0, 0)),
                  pl.BlockSpec((1, D_MODEL, D_EXPERT), lambda i, te, nu: (te[i], 0, 0)),
                  pl.BlockSpec((1, D_EXPERT, D_MODEL), lambda i, te, nu: (te[i], 0, 0))],
        out_specs=pl.BlockSpec((tm, D_MODEL), lambda i, te, nu: (i, 0)),
    )
    return pl.pallas_call(
        _gmm_kernel,
        grid_spec=grid_spec,
        out_shape=jax.ShapeDtypeStruct((nr, D_MODEL), F32),
        compiler_params=_cparams(("arbitrary",), VMEM_LIMIT),
        name="moe_gmm",
    )(tile_expert, n_used, xs, wg, wu, wd)


def _combine_kernel(pos_ref, x1_ref, mod_ref, rt_ref, g_ref, ys_ref, o_ref, buf_ref, sem):
    i = pl.program_id(0)
    n = pl.num_programs(0)
    tm = x1_ref.shape[0]

    def issue(tile, slot):
        base = tile * (2 * tm)

        def body(t, c):
            for k in range(2):
                p = pos_ref[base + 2 * t + k]
                pltpu.make_async_copy(ys_ref.at[pl.ds(p, 1)], buf_ref.at[slot, k, pl.ds(t, 1)],
                                      sem.at[slot]).start()
            return c

        lax.fori_loop(0, tm, body, 0, unroll=8)

    @pl.when(i == 0)
    def _():
        issue(0, 0)

    @pl.when(i + 1 < n)
    def _():
        issue(i + 1, (i + 1) % 2)

    slot = i % 2
    for k in range(2):
        pltpu.make_async_copy(ys_ref.at[pl.ds(0, tm)], buf_ref.at[slot, k], sem.at[slot]).wait()
    gate_f = mod_ref[0][:, 5 * D_MODEL:6 * D_MODEL]
    rt = rt_ref[...]
    y = rt[:, 2:3] * buf_ref[slot, 0] + rt[:, 3:4] * buf_ref[slot, 1]
    o_ref[...] = x1_ref[...] + gate_f * _rms(y, g_ref[...])


def _combine(pos, x1, mod3, route, gpost, ys, *, seq, tm):
    t = x1.shape[0]
    tpb = seq // tm
    row = lambda w: pl.BlockSpec((tm, w), lambda i, pos: (i, 0))
    grid_spec = pltpu.PrefetchScalarGridSpec(
        num_scalar_prefetch=1,
        grid=(t // tm,),
        in_specs=[row(D_MODEL),
                  pl.BlockSpec((1, 1, 6 * D_MODEL), lambda i, pos: (i // tpb, 0, 0)),
                  row(LANES),
                  pl.BlockSpec(gpost.shape, lambda i, pos: (0, 0)),
                  pl.BlockSpec(memory_space=pl.ANY)],
        out_specs=row(D_MODEL),
        scratch_shapes=[pltpu.VMEM((2, 2, tm, D_MODEL), F32), pltpu.SemaphoreType.DMA((2,))],
    )
    return pl.pallas_call(
        _combine_kernel,
        grid_spec=grid_spec,
        out_shape=jax.ShapeDtypeStruct((t, D_MODEL), F32),
        compiler_params=_cparams(("arbitrary",), VMEM_LIMIT),
        name="moe_combine",
    )(pos, x1, mod3, route, gpost, ys)


def _moe_sample_kernel(h2_ref, rt_ref, x1_ref, mod_ref, g_ref, wg_ref, wu_ref, wd_ref,
                       o_ref, acc_ref):
    e = pl.program_id(0)

    @pl.when(e == 0)
    def _():
        acc_ref[...] = jnp.zeros_like(acc_ref)

    h2 = h2_ref[...]
    rt = rt_ref[...]
    ef = e.astype(F32)
    cw = (jnp.where(rt[:, 0:1] == ef, rt[:, 2:3], 0.0)
          + jnp.where(rt[:, 1:2] == ef, rt[:, 3:4], 0.0))
    a = _mm3(h2, *_split(wg_ref[0]))
    u = _mm3(h2, *_split(wu_ref[0]))
    act = _silu(a) * u * cw
    acc_ref[...] += _mm3(act, *_split(wd_ref[0]))

    @pl.when(e == pl.num_programs(0) - 1)
    def _():
        gate_f = mod_ref[:, 5 * D_MODEL:6 * D_MODEL]
        o_ref[...] = x1_ref[...] + gate_f * _rms(acc_ref[...], g_ref[...])


def _moe_sample(h2, route, x1, mod_tok, gpost, wg, wu, wd):
    t = h2.shape[0]
    return pl.pallas_call(
        _moe_sample_kernel,
        grid=(N_EXPERTS,),
        in_specs=[_full(h2.shape), _full(route.shape), _full(x1.shape), _full(mod_tok.shape),
                  _full(gpost.shape),
                  pl.BlockSpec((1, D_MODEL, D_EXPERT), lambda e: (e, 0, 0)),
                  pl.BlockSpec((1, D_MODEL, D_EXPERT), lambda e: (e, 0, 0)),
                  pl.BlockSpec((1, D_EXPERT, D_MODEL), lambda e: (e, 0, 0))],
        out_specs=_full((t, D_MODEL)),
        out_shape=jax.ShapeDtypeStruct((t, D_MODEL), F32),
        scratch_shapes=[pltpu.VMEM((t, D_MODEL), F32)],
        compiler_params=_cparams(("arbitrary",), VMEM_LIMIT),
        name="moe_sample",
    )(h2, route, x1, mod_tok, gpost, wg, wu, wd)


def _rev_cumsum_lanes(x):
    lane = lax.broadcasted_iota(jnp.int32, x.shape, 1)
    sh = 1
    while sh < LANES:
        x = x + jnp.where(lane + sh < LANES, pltpu.roll(x, LANES - sh, 1), 0.0)
        sh *= 2
    return x


def _fwd_cumsum_lanes(x):
    lane = lax.broadcasted_iota(jnp.int32, x.shape, 1)
    sh = 1
    while sh < LANES:
        x = x + jnp.where(lane >= sh, pltpu.roll(x, sh, 1), 0.0)
        sh *= 2
    return x


def _seg_update(s, m_ref, l_ref):
    m_prev = m_ref[...]
    m_new = jnp.maximum(m_prev, jnp.max(s, axis=1, keepdims=True))
    alpha = jnp.exp2(m_prev - m_new)
    p = jnp.exp2(s - m_new)
    l_ref[...] = alpha * l_ref[...] + jnp.sum(p, axis=1, keepdims=True)
    m_ref[...] = m_new
    return p, alpha


def _col_to_lanes(col, fill=0.0):
    n = col.shape[0]
    row = lax.broadcasted_iota(jnp.int32, (n, LANES), 0)
    lane = lax.broadcasted_iota(jnp.int32, (n, LANES), 1)
    vec = jnp.sum(jnp.where(row == lane, col, 0.0), axis=0, keepdims=True)
    return jnp.where(lane[0:1, :] < n, vec, fill)


def _sample_attn_kernel(pt_ref, qa_ref, qr_ref, qf_ref, latn_ref, krn_ref, fkn_ref, fvn_ref,
                        lfn_ref, *rest, pages, n_chunks):
    npg = pages
    lat_refs = rest[0:npg]
    kr_refs = rest[npg:2 * npg]
    fk_refs = rest[2 * npg:3 * npg]
    fv_refs = rest[3 * npg:4 * npg]
    lf_refs = rest[4 * npg:5 * npg]
    om_ref, of_ref = rest[5 * npg:5 * npg + 2]
    mm_ref, lm_ref, accm_ref, mf_ref, lfs_ref, accf_ref, carry_ref = rest[5 * npg + 2:]
    c = pl.program_id(1)
    nrow = qa_ref.shape[1]

    @pl.when(c == 0)
    def _():
        mm_ref[...] = jnp.full(mm_ref.shape, NEG, F32)
        mf_ref[...] = jnp.full(mf_ref.shape, NEG, F32)
        lm_ref[...] = jnp.zeros(lm_ref.shape, F32)
        lfs_ref[...] = jnp.zeros(lfs_ref.shape, F32)
        accm_ref[...] = jnp.zeros(accm_ref.shape, F32)
        accf_ref[...] = jnp.zeros(accf_ref.shape, F32)
        carry_ref[...] = jnp.zeros(carry_ref.shape, F32)

    def stack(x):
        hi, lo = _split(x)
        return hi, jnp.concatenate([hi, lo], axis=0)

    qa_hi, qa_cat = stack(qa_ref[0])
    qr_hi, qr_cat = stack(qr_ref[0][:, 0:QK_ROPE])
    qf_hi, qf_cat = stack(qf_ref[0])

    def attend(lats, krts, fkts, fvts, bias_f, bias_m):
        ngrp = 2 if len(lats) >= 2 else 1
        per = len(lats) // ngrp
        width = per * LANES
        groups = []
        for g in range(ngrp):
            sl = slice(g * per, (g + 1) * per)
            lh, ll = _split(jnp.concatenate(lats[sl], axis=0))
            kh, kl = _split(jnp.concatenate(krts[sl], axis=1))
            fh, fl = _split(jnp.concatenate(fkts[sl], axis=1))
            top = _mm_nt(qa_cat, lh) + _mm(qr_cat, kh)
            s_m = top[:nrow] + top[nrow:] + _mm_nt(qa_hi, ll) + _mm(qr_hi, kl)
            top = _mm(qf_cat, fh)
            s_f = (top[:nrow] + top[nrow:] + _mm(qf_hi, fl)
                   + bias_f[:, g * width:(g + 1) * width])
            if bias_m is not None:
                s_m = s_m + bias_m
            groups.append((s_m, s_f, lh, ll))
        for g in range(ngrp):
            s_m, s_f, lh, ll = groups[g]
            sl = slice(g * per, (g + 1) * per)
            p_m, a_m = _seg_update(s_m, mm_ref, lm_ref)
            p_f, a_f = _seg_update(s_f, mf_ref, lfs_ref)
            pm_hi, pm_lo = _split(p_m)
            pf_hi, pf_lo = _split(p_f)
            zrow = jnp.zeros(pm_hi.shape, BF16)
            pm_cat = jnp.concatenate([pm_hi, pm_lo], axis=0)
            pm_hi0 = jnp.concatenate([pm_hi, zrow], axis=0)
            pf_cat = jnp.concatenate([pf_hi, pf_lo, zrow, zrow], axis=0)
            pf_hi0 = jnp.concatenate([pf_hi, zrow, zrow, zrow], axis=0)
            vh, vl = _split(jnp.concatenate(fvts[sl], axis=1))
            upd_m = _mm(pm_cat, lh) + _mm(pm_hi0, ll)
            upd_f = _mm_nt(vh, pf_cat) + _mm_nt(vl, pf_hi0)
            accm_ref[...] = jnp.concatenate([a_m, a_m], axis=0) * accm_ref[...] + upd_m
            a_lanes = _col_to_lanes(a_f)
            accf_ref[...] = (a_lanes + pltpu.roll(a_lanes, nrow, 1)) * accf_ref[...] + upd_f

    sufs = [None] * npg
    carry = carry_ref[...]
    for p in range(npg - 1, -1, -1):
        lf = lf_refs[p][0, 0]
        inc = _rev_cumsum_lanes(lf)
        sufs[p] = (inc - lf + carry) * LOG2E
        carry = carry + inc[:, 0:1]
    carry_ref[...] = jnp.broadcast_to(carry[:, 0:1], carry_ref.shape)
    suf = jnp.concatenate(sufs, axis=1)
    bias_f = jnp.concatenate([suf] * (nrow // FOX_HEADS), axis=0)
    attend([r[0, 0] for r in lat_refs], [r[0, 0] for r in kr_refs],
           [r[0, 0].reshape(FOX_W, LANES) for r in fk_refs],
           [r[0, 0].reshape(FOX_W, LANES) for r in fv_refs], bias_f, None)

    @pl.when(c == n_chunks - 1)
    def _():
        row = lax.broadcasted_iota(jnp.int32, (nrow, LANES), 0)
        lane = lax.broadcasted_iota(jnp.int32, (nrow, LANES), 1)
        ok = lane <= (row >> 3)
        cum_new = _fwd_cumsum_lanes(lfn_ref[0]) * LOG2E
        b_f = jnp.where(ok, -jnp.concatenate([cum_new] * (nrow // FOX_HEADS), axis=0), NEG)
        b_m = jnp.where(ok, 0.0, NEG)
        attend([latn_ref[0]], [krn_ref[0]], [fkn_ref[0]], [fvn_ref[0]], b_f, b_m)
        accm = accm_ref[...]
        om_ref[0] = (accm[:nrow] + accm[nrow:]) / lm_ref[...]
        accf = accf_ref[...]
        of_ref[0] = (accf + pltpu.roll(accf, LANES - nrow, 1)) / _col_to_lanes(lfs_ref[...], 1.0)


def _sample_attn(page_table, qa, qr, qf, latn, krn, fkn, fvn, lfn,
                 c_lat, c_kr, c_fk, c_fv, c_lf, *, pages):
    nb, n_pages = page_table.shape
    n_chunks = n_pages // pages
    nrow = qa.shape[1]

    def per_b(shape):
        nd = len(shape)
        return pl.BlockSpec((1,) + shape[1:], lambda b, c, pt: (b,) + (0,) * (nd - 1))

    def page_spec(arr, p):
        nd = arr.ndim
        blk = (1, 1) + arr.shape[2:]

        def imap(b, c, pt, p=p):
            return (pt[b, (n_chunks - 1 - c) * pages + p],) + (0,) * (nd - 1)
        return pl.BlockSpec(blk, imap)

    small = [qa, qr, qf, latn, krn, fkn, fvn, lfn]
    in_specs = [per_b(a.shape) for a in small]
    operands = list(small)
    for arr in (c_lat, c_kr, c_fk, c_fv, c_lf):
        for p in range(pages):
            in_specs.append(page_spec(arr, p))
            operands.append(arr)
    grid_spec = pltpu.PrefetchScalarGridSpec(
        num_scalar_prefetch=1,
        grid=(nb, n_chunks),
        in_specs=in_specs,
        out_specs=[pl.BlockSpec((1, nrow, KV_LORA), lambda b, c, pt: (b, 0, 0)),
                   pl.BlockSpec((1, FOX_W, LANES), lambda b, c, pt: (b, 0, 0))],
        scratch_shapes=[pltpu.VMEM((nrow, 1), F32), pltpu.VMEM((nrow, 1), F32),
                        pltpu.VMEM((2 * nrow, KV_LORA), F32),
                        pltpu.VMEM((nrow, 1), F32), pltpu.VMEM((nrow, 1), F32),
                        pltpu.VMEM((FOX_W, LANES), F32),
                        pltpu.VMEM((FOX_HEADS, LANES), F32)],
    )
    return pl.pallas_call(
        functools.partial(_sample_attn_kernel, pages=pages, n_chunks=n_chunks),
        grid_spec=grid_spec,
        out_shape=[jax.ShapeDtypeStruct((nb, nrow, KV_LORA), F32),
                   jax.ShapeDtypeStruct((nb, FOX_W, LANES), F32)],
        compiler_params=_cparams(("arbitrary", "arbitrary"), VMEM_LIMIT),
        name="sample_attn",
    )(page_table, *operands)


def _hl(w):
    hi = w.astype(BF16)
    return hi, (w - hi.astype(F32)).astype(BF16)


def _rope_tables(pos):
    half = QK_ROPE // 2
    inv = ROPE_THETA ** (-jnp.arange(half, dtype=F32) / half)
    ang = pos.astype(F32)[:, None] * inv[None, :]
    cc = jnp.concatenate([jnp.cos(ang), jnp.cos(ang)], axis=1)
    ss = jnp.concatenate([jnp.sin(ang), jnp.sin(ang)], axis=1)
    n = pos.shape[0]
    qs = MLA_SCALE * LOG2E
    ctab = jnp.concatenate([jnp.full((n, KV_LORA), qs, F32), cc * qs,
                            jnp.zeros((n, QHEAD_W - KV_LORA - QK_ROPE), F32)], axis=1)
    stab = jnp.concatenate([jnp.zeros((n, KV_LORA), F32), ss * qs,
                            jnp.zeros((n, QHEAD_W - KV_LORA - QK_ROPE), F32)], axis=1)
    zpad = jnp.zeros((n, LANES - QK_ROPE), F32)
    return ctab, stab, jnp.concatenate([cc, zpad], axis=1), jnp.concatenate([ss, zpad], axis=1)


def _swap_halves(w):
    half = QK_ROPE // 2
    return jnp.concatenate([-w[..., half:], w[..., :half]], axis=-1)


def kernel(x_prompt, x_sample, cache_mla_latent, cache_mla_krope, cache_fox_k, cache_fox_v,
           cache_fox_logf, page_table, c_prompt, c_sample, w_ada, b_ada, g_pre_mix, g_post_mix,
           g_pre_ffn, g_post_ffn, w_in, b_forget, g_q_lora, g_kv_lora, w_uq, w_uk, w_uv, w_o,
           w_group_router, b_group_router, w_expert_router, b_expert_router, w_gate, w_up, w_down):
    assert w_ada.shape[0] == 1, "single-layer trunk"
    batch, seq, d = x_prompt.shape
    dec_batch, n_new, _ = x_sample.shape
    n_pages = page_table.shape[1]
    page = cache_mla_latent.shape[2]
    past_len = n_pages * page
    tp = batch * seq
    ts = dec_batch * n_new

    w_in0 = w_in[0]
    s0, s1, s2, s3, s4, s5 = 256, 384, 416, 928, 1440, 1952
    w_kr = w_in0[:, s1:s2]
    z = lambda n: jnp.zeros((d, n), F32)
    w_all = jnp.concatenate([
        w_in0[:, 0:s0],
        w_in0[:, s0:s1], w_kr, z(96),
        _swap_halves(w_kr), z(96),
        w_in0[:, s2:s3], w_in0[:, s3:s4], w_in0[:, s4:s5],
        w_in0[:, s5:], z(120)], axis=1)
    w_uq0 = w_uq[0]
    w_qabs = _fold(jnp.transpose(w_uq0[:, :, :QK_NOPE], (1, 0, 2)),
                   jnp.transpose(w_uk[0], (0, 2, 1)))
    w_qr = jnp.transpose(w_uq0[:, :, QK_NOPE:], (1, 0, 2))
    zq = lambda n: jnp.zeros((MLA_HEADS, Q_LORA, n), F32)
    w_a = jnp.concatenate([w_qabs, w_qr, zq(96)], axis=2)
    w_b = jnp.concatenate([zq(128), _swap_halves(w_qr), zq(96)], axis=2)
    w_a = jnp.transpose(w_a, (1, 0, 2)).reshape(Q_LORA, MLA_HEADS * QHEAD_W)
    w_b = jnp.transpose(w_b, (1, 0, 2)).reshape(Q_LORA, MLA_HEADS * QHEAD_W)
    w_o0 = w_o[0]
    w_ov = _fold(w_uv[0], w_o0[:MLA_HEADS * V_HEAD].reshape(MLA_HEADS, V_HEAD, d))
    w_ov = w_ov.reshape(MLA_HEADS * KV_LORA, d)
    w_of = w_o0[MLA_HEADS * V_HEAD:]
    w_r = jnp.concatenate([w_group_router[0], w_expert_router[0].reshape(d, N_EXPERTS),
                           jnp.zeros((d, LANES - N_GROUPS - N_EXPERTS), F32)], axis=1)
    b_r = jnp.concatenate([b_group_router[0], b_expert_router[0].reshape(N_EXPERTS),
                           jnp.zeros((LANES - N_GROUPS - N_EXPERTS,), F32)])[None, :]
    bfg = jnp.concatenate([b_forget[0], jnp.zeros((LANES - FOX_HEADS,), F32)])[None, :]
    w_all_hl, w_a_hl, w_b_hl = _hl(w_all), _hl(w_a), _hl(w_b)
    w_ov_hl, w_of_hl, w_r_hl = _hl(w_ov), _hl(w_of), _hl(w_r)
    wg = w_gate[0].reshape(N_EXPERTS, d, D_EXPERT)
    wu = w_up[0].reshape(N_EXPERTS, d, D_EXPERT)
    wd = w_down[0].reshape(N_EXPERTS, D_EXPERT, d)

    c_all = jnp.concatenate([c_prompt, c_sample], axis=0)
    c_all = jnp.pad(c_all, ((0, -c_all.shape[0] % 64), (0, 0)))
    mod = _ada(c_all, w_ada[0], b_ada)
    mod_p = mod[:batch].reshape(batch, 1, 6 * d)
    mod_s = jnp.repeat(mod[batch:batch + dec_batch], n_new, axis=0)

    tm = 512
    xp2 = x_prompt.reshape(tp, d)
    tabs_p = _rope_tables(jnp.arange(seq))
    (q_p, kmla_p, ckv_p, kr_p, fq_p, fk_p, fk16_p, fv_p, fv16_p, lf_p, cum_p) = _proj_prompt(
        xp2, mod_p, g_pre_mix, g_q_lora, g_kv_lora, bfg, tabs_p,
        w_all_hl[0], w_a_hl[0], w_b_hl[0], seq=seq, tm=tm)
    tk = 512
    o_lat_p = _mla_prompt(q_p, kmla_p, batch=batch, seq=seq, tq=128, tk=tk)
    ck4 = cum_p.reshape(batch, FOX_HEADS, seq // tk, tk)
    o_fox_p = _fox_prompt(fq_p, fk16_p, fv16_p, ck4, batch=batch, seq=seq, tq=512, tk=tk)
    x1_p, h2_p, rt_p, cnt_p = _post_prompt(xp2, mod_p, o_lat_p, o_fox_p, w_ov_hl[0], w_of_hl[0],
                                           g_post_mix, g_pre_ffn, w_r_hl, b_r, seq=seq, tm=tm)

    tme = 256
    n_asg = 2 * tp
    n_rows = n_asg + N_EXPERTS * tme
    cnt = cnt_p[0, :N_EXPERTS].astype(jnp.int32)
    pend = jnp.cumsum(((cnt + tme - 1) // tme) * tme)
    offs = pend - ((cnt + tme - 1) // tme) * tme
    tile_start = jnp.arange(n_rows // tme, dtype=jnp.int32) * tme
    tile_e = jnp.sum(tile_start[:, None] >= pend[None, :], axis=1, dtype=jnp.int32)
    tile_e = jnp.minimum(tile_e, N_EXPERTS - 1)
    n_used = (pend[-1] // tme).astype(jnp.int32)[None]
    e_ids = rt_p[:, 0:2]
    hit = e_ids[:, :, None] == jnp.arange(N_EXPERTS, dtype=F32)[None, None, :]
    offs_tok = jnp.sum(jnp.where(hit, offs[None, None, :], 0), axis=-1, dtype=jnp.int32)
    pos = (offs_tok + rt_p[:, 4:6].astype(jnp.int32)).reshape(n_asg)
    xs = _dispatch(pos, h2_p, n_rows, tm=256)
    ys = _gmm(tile_e, n_used, xs, wg, wu, wd, tm=tme)
    y_prompt = _combine(pos, x1_p, mod_p, rt_p, g_post_ffn, ys, seq=seq, tm=256)

    xs2 = x_sample.reshape(ts, d)
    pos_s = jnp.tile(past_len + jnp.arange(n_new), dec_batch)
    tabs_s = _rope_tables(pos_s)
    q_s, ckv_s, krp_s, fq_s, fk_s, fv_s, lfp_s = _proj_sample(
        xs2, mod_s, g_pre_mix, g_q_lora, g_kv_lora, bfg, tabs_s, w_all_hl, w_a_hl, w_b_hl)
    q4 = q_s.reshape(dec_batch, n_new, MLA_HEADS, QHEAD_W)
    nrow = n_new * MLA_HEADS
    qa = q4[..., :KV_LORA].reshape(dec_batch, nrow, KV_LORA)
    qr = q4[..., KV_LORA:].reshape(dec_batch, nrow, QHEAD_W - KV_LORA)
    head_mask = (jnp.arange(FOX_W)[None, :] // FOX_HEAD_DIM == jnp.arange(FOX_HEADS)[:, None])
    qf = (fq_s.reshape(dec_batch, n_new, 1, FOX_W) * head_mask[None, None].astype(F32))
    qf = qf.reshape(dec_batch, nrow, FOX_W)
    padk = page - n_new
    latn = jnp.pad(ckv_s.reshape(dec_batch, n_new, KV_LORA), ((0, 0), (0, padk), (0, 0)))
    krn = jnp.pad(jnp.transpose(krp_s[:, :QK_ROPE].reshape(dec_batch, n_new, QK_ROPE), (0, 2, 1)),
                  ((0, 0), (0, 0), (0, padk)))
    fkn = jnp.pad(jnp.transpose(fk_s.reshape(dec_batch, n_new, FOX_W), (0, 2, 1)),
                  ((0, 0), (0, 0), (0, padk)))
    fvn = jnp.pad(jnp.transpose(fv_s.reshape(dec_batch, n_new, FOX_W), (0, 2, 1)),
                  ((0, 0), (0, 0), (0, padk)))
    lfn = jnp.pad(jnp.transpose(lfp_s[:, :FOX_HEADS].reshape(dec_batch, n_new, FOX_HEADS),
                                (0, 2, 1)), ((0, 0), (0, 0), (0, padk)))
    c_lat = cache_mla_latent
    c_kr = jnp.transpose(cache_mla_krope, (0, 1, 3, 2))
    c_fk = jnp.transpose(cache_fox_k, (0, 1, 3, 4, 2))
    c_fv = jnp.transpose(cache_fox_v, (0, 1, 3, 4, 2))
    c_lf = jnp.transpose(cache_fox_logf, (0, 1, 3, 2))
    om, oft = _sample_attn(page_table, qa, qr, qf, latn, krn, fkn, fvn, lfn,
                           c_lat, c_kr, c_fk, c_fv, c_lf, pages=16)
    o_lat_s = om.reshape(dec_batch, n_new, MLA_HEADS * KV_LORA).reshape(ts, MLA_HEADS * KV_LORA)
    oft4 = oft[:, :, :nrow].reshape(dec_batch, FOX_HEADS, FOX_HEAD_DIM, n_new, FOX_HEADS)
    o_fox_s = jnp.einsum('bhdqh->bqhd', oft4).reshape(ts, FOX_W)
    x1_s, h2_s, rt_s = _post_sample(xs2, mod_s, o_lat_s, o_fox_s, w_ov_hl, w_of_hl,
                                    g_post_mix, g_pre_ffn, w_r_hl, b_r)
    y_sample = _moe_sample(h2_s, rt_s, x1_s, mod_s, g_post_ffn, wg, wu, wd)

    return (y_prompt.reshape(batch, seq, d), y_sample.reshape(dec_batch, n_new, d),
            ckv_p.reshape(batch, 1, seq, KV_LORA),
            jnp.transpose(kr_p.reshape(batch, 1, QK_ROPE, seq), (0, 1, 3, 2)),
            jnp.transpose(fk_p.reshape(batch, 1, FOX_HEADS, FOX_HEAD_DIM, seq), (0, 1, 4, 2, 3)),
            jnp.transpose(fv_p.reshape(batch, 1, FOX_HEADS, FOX_HEAD_DIM, seq), (0, 1, 4, 2, 3)),
            jnp.transpose(lf_p.reshape(batch, 1, FOX_HEADS, seq), (0, 1, 3, 2)),
            ckv_s.reshape(dec_batch, 1, n_new, KV_LORA),
            krp_s[:, :QK_ROPE].reshape(dec_batch, 1, n_new, QK_ROPE),
            fk_s.reshape(dec_batch, 1, n_new, FOX_HEADS, FOX_HEAD_DIM),
            fv_s.reshape(dec_batch, 1, n_new, FOX_HEADS, FOX_HEAD_DIM),
            lfp_s[:, :FOX_HEADS].reshape(dec_batch, 1, n_new, FOX_HEADS))
```

```python
import functools
import math

import jax
import jax.numpy as jnp
import numpy as np
from jax import lax
from jax.experimental import pallas as pl
from jax.experimental.pallas import tpu as pltpu

F32 = jnp.float32
BF16 = jnp.bfloat16

D_MODEL = 1024
MLA_HEADS = 8
QK_NOPE = 64
QK_ROPE = 32
V_HEAD = 64
Q_LORA = 256
KV_LORA = 128
FOX_HEADS = 8
FOX_HEAD_DIM = 64
FOX_W = FOX_HEADS * FOX_HEAD_DIM
N_GROUPS = 4
EXPERTS_PER_GROUP = 8
N_EXPERTS = N_GROUPS * EXPERTS_PER_GROUP
D_EXPERT = 256
ROPE_THETA = 10000.0
EPS = 1e-6
NEG = -1e30
LOG2E = 1.4426950408889634
MLA_SCALE = (QK_NOPE + QK_ROPE) ** -0.5
FOX_SCALE = FOX_HEAD_DIM ** -0.5

LANES = 128
QHEAD_W = 256
W_ALL_COLS = 2304
VMEM_LIMIT = 56 * 1024 * 1024


def _split(x):
    hi = x.astype(BF16)
    lo = (x - hi.astype(F32)).astype(BF16)
    return hi, lo


def _mm(a, b):
    return jnp.dot(a, b, preferred_element_type=F32)


def _mm_nt(a, b):
    return lax.dot_general(a, b, (((1,), (1,)), ((), ())), preferred_element_type=F32)


def _mm3(a, b_hi, b_lo):
    a_hi, a_lo = _split(a)
    m = a.shape[0]
    top = _mm(jnp.concatenate([a_hi, a_lo], axis=0), b_hi)
    return top[:m] + top[m:] + _mm(a_hi, b_lo)


def _mm3_nt(a, b_hi, b_lo):
    a_hi, a_lo = _split(a)
    m = a.shape[0]
    top = _mm_nt(jnp.concatenate([a_hi, a_lo], axis=0), b_hi)
    return top[:m] + top[m:] + _mm_nt(a_hi, b_lo)


def _rms(x, g):
    return x * lax.rsqrt(jnp.mean(x * x, axis=-1, keepdims=True) + EPS) * g


def _silu(x):
    return x / (1.0 + jnp.exp(-x))


def _log_sigmoid(x):
    return jnp.minimum(x, 0.0) - jnp.log(1.0 + jnp.exp(-jnp.abs(x)))


def _cparams(sem, vmem=None):
    return pltpu.CompilerParams(dimension_semantics=sem, vmem_limit_bytes=vmem)


def _full(shape):
    nd = len(shape)
    return pl.BlockSpec(shape, lambda *_: (0,) * nd)


def _ada_kernel(c_ref, w_ref, b_ref, o_ref):
    a = _silu(c_ref[...])
    w_hi, w_lo = _split(w_ref[...])
    o_ref[...] = _mm3(a, w_hi, w_lo) + b_ref[...]


def _ada(c_all, w_ada, b_ada):
    n, d = c_all.shape
    cols = w_ada.shape[1]
    tn = 1024
    return pl.pallas_call(
        _ada_kernel,
        grid=(cols // tn,),
        in_specs=[_full((n, d)),
                  pl.BlockSpec((d, tn), lambda j: (0, j)),
                  pl.BlockSpec((1, tn), lambda j: (0, j))],
        out_specs=pl.BlockSpec((n, tn), lambda j: (0, j)),
        out_shape=jax.ShapeDtypeStruct((n, cols), F32),
        compiler_params=_cparams(("parallel",)),
        name="ada_mod",
    )(c_all, w_ada, b_ada)


def _fold_kernel(a_ref, b_ref, o_ref):
    b_hi, b_lo = _split(b_ref[0])
    o_ref[0] = _mm3(a_ref[0], b_hi, b_lo)


def _fold(a, b):
    h, m, k = a.shape
    n = b.shape[2]
    return pl.pallas_call(
        _fold_kernel,
        grid=(h,),
        in_specs=[pl.BlockSpec((1, m, k), lambda i: (i, 0, 0)),
                  pl.BlockSpec((1, k, n), lambda i: (i, 0, 0))],
        out_specs=pl.BlockSpec((1, m, n), lambda i: (i, 0, 0)),
        out_shape=jax.ShapeDtypeStruct((h, m, n), F32),
        compiler_params=_cparams(("parallel",)),
        name="fold_weights",
    )(a, b)


def _proj_common(x, mod, g_pre, g_q, g_kv, bfg, tabs, mm_all, mm_qa, mm_qb):
    ctab, stab, kct, kst = tabs
    shift = mod[:, 0:D_MODEL]
    scale = mod[:, D_MODEL:2 * D_MODEL]
    h = _rms(x, g_pre) * (1.0 + scale) + shift
    z = mm_all(h)
    cq = _rms(z[:, 0:Q_LORA], g_q)
    nh = MLA_HEADS
    q = (mm_qa(cq) * jnp.concatenate([ctab] * nh, axis=1)
         + mm_qb(cq) * jnp.concatenate([stab] * nh, axis=1))
    c_kv = _rms(z[:, 256:384], g_kv)
    krp = z[:, 384:512] * kct + z[:, 512:640] * kst
    fq = z[:, 640:1152] * (FOX_SCALE * LOG2E)
    fk = z[:, 1152:1664]
    fv = z[:, 1664:2176]
    logf = _log_sigmoid(z[:, 2176:2304] + bfg)
    return q, c_kv, krp, fq, fk, fv, logf


def _proj_prompt_kernel(x_ref, mod_ref, gpre_ref, gq_ref, gkv_ref, bfg_ref,
                        ct_ref, st_ref, kct_ref, kst_ref,
                        wall_ref, wa_ref, wb_ref,
                        q_ref, kmla_ref, ckv_ref, kr_ref, fq_ref, fk_ref, fk16_ref,
                        fv_ref, fv16_ref, lf_ref, cum_ref, carry_ref, *, tiles_per_batch):
    i = pl.program_id(0)
    tm = x_ref.shape[0]
    q, c_kv, krp, fq, fk, fv, logf = _proj_common(
        x_ref[...], mod_ref[0], gpre_ref[...], gq_ref[...], gkv_ref[...], bfg_ref[...],
        (ct_ref[...], st_ref[...], kct_ref[...], kst_ref[...]),
        lambda h: _mm(h.astype(BF16), wall_ref[...]),
        lambda c: _mm(c.astype(BF16), wa_ref[...]),
        lambda c: _mm(c.astype(BF16), wb_ref[...]))
    q_ref[...] = q.astype(BF16)
    ckv_ref[...] = c_kv
    kr_ref[0] = jnp.transpose(krp)[0:QK_ROPE, :]
    kmla_ref[...] = jnp.concatenate([c_kv, krp], axis=1).astype(BF16)
    fq_ref[...] = fq.astype(BF16)
    fk_ref[0] = jnp.transpose(fk)
    fk16_ref[...] = fk.astype(BF16)
    fv_ref[0] = jnp.transpose(fv)
    fv16_ref[...] = fv.astype(BF16)
    lf_ref[0] = jnp.transpose(logf)[0:FOX_HEADS, :]

    @pl.when(i % tiles_per_batch == 0)
    def _():
        carry_ref[...] = jnp.zeros_like(carry_ref)

    row = lax.broadcasted_iota(jnp.int32, (tm, tm), 0)
    col = lax.broadcasted_iota(jnp.int32, (tm, tm), 1)
    tri = (row >= col).astype(BF16)
    p1 = logf.astype(BF16)
    r1 = logf - p1.astype(F32)
    p2 = r1.astype(BF16)
    p3 = (r1 - p2.astype(F32)).astype(BF16)
    cum = _mm(tri, p1) + _mm(tri, p2) + _mm(tri, p3) + carry_ref[0:1, :]
    carry_ref[0:1, :] = cum[tm - 1:tm, :]
    cum_ref[0] = jnp.transpose(cum * LOG2E)[0:FOX_HEADS, :]


def _proj_prompt(x2d, mod3, gpre, gq, gkv, bfg, tabs, wall, wa, wb, *, seq, tm):
    t = x2d.shape[0]
    tpb = seq // tm
    row = lambda w: pl.BlockSpec((tm, w), lambda i: (i, 0))
    pos = lambda w: pl.BlockSpec((tm, w), lambda i: (i % tpb, 0))
    nb = t // seq
    keym = lambda w: pl.BlockSpec((1, w, tm), lambda i: (i // tpb, 0, i % tpb))
    out_shapes = [
        jax.ShapeDtypeStruct((t, MLA_HEADS * QHEAD_W), BF16),
        jax.ShapeDtypeStruct((t, QHEAD_W), BF16),
        jax.ShapeDtypeStruct((t, KV_LORA), F32),
        jax.ShapeDtypeStruct((nb, QK_ROPE, seq), F32),
        jax.ShapeDtypeStruct((t, FOX_W), BF16),
        jax.ShapeDtypeStruct((nb, FOX_W, seq), F32),
        jax.ShapeDtypeStruct((t, FOX_W), BF16),
        jax.ShapeDtypeStruct((nb, FOX_W, seq), F32),
        jax.ShapeDtypeStruct((t, FOX_W), BF16),
        jax.ShapeDtypeStruct((nb, FOX_HEADS, seq), F32),
        jax.ShapeDtypeStruct((nb, FOX_HEADS, seq), F32),
    ]
    out_specs = [row(MLA_HEADS * QHEAD_W), row(QHEAD_W), row(KV_LORA), keym(QK_ROPE),
                 row(FOX_W), keym(FOX_W), row(FOX_W), keym(FOX_W), row(FOX_W),
                 keym(FOX_HEADS), keym(FOX_HEADS)]
    return pl.pallas_call(
        functools.partial(_proj_prompt_kernel, tiles_per_batch=tpb),
        grid=(t // tm,),
        in_specs=[row(D_MODEL),
                  pl.BlockSpec((1, 1, 6 * D_MODEL), lambda i: (i // tpb, 0, 0)),
                  _full(gpre.shape), _full(gq.shape), _full(gkv.shape), _full(bfg.shape),
                  pos(QHEAD_W), pos(QHEAD_W), pos(LANES), pos(LANES),
                  _full(wall.shape), _full(wa.shape), _full(wb.shape)],
        out_specs=out_specs,
        out_shape=out_shapes,
        scratch_shapes=[pltpu.VMEM((8, LANES), F32)],
        compiler_params=_cparams(("arbitrary",), VMEM_LIMIT),
        name="proj_prompt",
    )(x2d, mod3, gpre, gq, gkv, bfg, *tabs, wall, wa, wb)


def _proj_sample_kernel(x_ref, mod_ref, gpre_ref, gq_ref, gkv_ref, bfg_ref,
                        ct_ref, st_ref, kct_ref, kst_ref,
                        wall_hi, wall_lo, wa_hi, wa_lo, wb_hi, wb_lo,
                        q_ref, ckv_ref, krp_ref, fq_ref, fk_ref, fv_ref, lf_ref):
    q, c_kv, krp, fq, fk, fv, logf = _proj_common(
        x_ref[...], mod_ref[...], gpre_ref[...], gq_ref[...], gkv_ref[...], bfg_ref[...],
        (ct_ref[...], st_ref[...], kct_ref[...], kst_ref[...]),
        lambda h: _mm3(h, wall_hi[...], wall_lo[...]),
        lambda c: _mm3(c, wa_hi[...], wa_lo[...]),
        lambda c: _mm3(c, wb_hi[...], wb_lo[...]))
    q_ref[...] = q
    ckv_ref[...] = c_kv
    krp_ref[...] = krp
    fq_ref[...] = fq
    fk_ref[...] = fk
    fv_ref[...] = fv
    lf_ref[...] = logf


def _proj_sample(x2d, mod_tok, gpre, gq, gkv, bfg, tabs, wall, wa, wb):
    t = x2d.shape[0]
    ins = [x2d, mod_tok, gpre, gq, gkv, bfg, *tabs, *wall, *wa, *wb]
    shapes = [(t, MLA_HEADS * QHEAD_W), (t, KV_LORA), (t, LANES), (t, FOX_W), (t, FOX_W),
              (t, FOX_W), (t, LANES)]
    return pl.pallas_call(
        _proj_sample_kernel,
        grid=(1,),
        in_specs=[_full(a.shape) for a in ins],
        out_specs=[_full(s) for s in shapes],
        out_shape=[jax.ShapeDtypeStruct(s, F32) for s in shapes],
        compiler_params=_cparams(("arbitrary",), VMEM_LIMIT),
        name="proj_sample",
    )(*ins)


def _flash_update(s, v1, m_ref, acc_ref):
    m_prev = m_ref[...]
    m_new = jnp.maximum(m_prev, jnp.max(s, axis=1, keepdims=True))
    alpha = jnp.exp2(m_prev - m_new)
    p = jnp.exp2(s - jnp.concatenate([m_new] * (s.shape[1] // LANES), axis=1))
    acc_ref[...] = (jnp.concatenate([alpha, alpha], axis=1) * acc_ref[...]
                    + _mm(p.astype(BF16), v1))
    m_ref[...] = m_new


def _with_ones(v):
    return jnp.concatenate([v, jnp.ones(v.shape, v.dtype)], axis=1)


def _causal_bounds(i, tq, tk):
    n_full = (i * tq + 1) // tk
    n_all = (i * tq + tq + tk - 1) // tk
    return n_full, n_all


def _mla_prompt_kernel(q_ref, k_ref, o_ref, m_ref, acc_ref, *, tq, tk, chunk):
    i = pl.program_id(1)
    nh = MLA_HEADS
    qs = jnp.concatenate([q_ref[:, h * QHEAD_W:(h + 1) * QHEAD_W] for h in range(nh)], axis=0)
    m_ref[...] = jnp.full(m_ref.shape, NEG, F32)
    acc_ref[...] = jnp.zeros(acc_ref.shape, F32)

    def step(j, masked):
        kb = k_ref[pl.ds(pl.multiple_of(j * tk, tk), tk), :]
        vb = _with_ones(kb[:, 0:KV_LORA])
        s_all = _mm_nt(qs, kb)
        for c in range(nh * tq // chunk):
            rows = slice(c * chunk, (c + 1) * chunk)
            s = s_all[rows]
            if masked:
                row = lax.broadcasted_iota(jnp.int32, s.shape, 0)
                col = lax.broadcasted_iota(jnp.int32, s.shape, 1)
                s = jnp.where(j * tk + col <= i * tq + ((c * chunk + row) & (tq - 1)), s, NEG)
            _flash_update(s, vb, m_ref.at[rows], acc_ref.at[rows])

    n_full, n_all = _causal_bounds(i, tq, tk)
    lax.fori_loop(0, n_full, lambda j, c: (step(j, False), c)[1], 0)
    lax.fori_loop(n_full, n_all, lambda j, c: (step(j, True), c)[1], 0)
    o = acc_ref[:, 0:KV_LORA] / acc_ref[:, KV_LORA:2 * KV_LORA]
    o_ref[...] = jnp.concatenate([o[h * tq:(h + 1) * tq] for h in range(nh)], axis=1).astype(BF16)


def _mla_prompt(q, kmla, *, batch, seq, tq, tk):
    t = q.shape[0]
    nq = seq // tq
    rows = MLA_HEADS * tq
    return pl.pallas_call(
        functools.partial(_mla_prompt_kernel, tq=tq, tk=tk, chunk=128),
        grid=(batch, nq),
        in_specs=[pl.BlockSpec((tq, MLA_HEADS * QHEAD_W), lambda b, i: (b * nq + i, 0)),
                  pl.BlockSpec((seq, QHEAD_W), lambda b, i: (b, 0))],
        out_specs=pl.BlockSpec((tq, MLA_HEADS * KV_LORA), lambda b, i: (b * nq + i, 0)),
        out_shape=jax.ShapeDtypeStruct((t, MLA_HEADS * KV_LORA), BF16),
        scratch_shapes=[pltpu.VMEM((rows, LANES), F32), pltpu.VMEM((rows, 2 * KV_LORA), F32)],
        compiler_params=_cparams(("parallel", "parallel"), VMEM_LIMIT),
        name="mla_prompt",
    )(q, kmla)


def _fox_prompt_kernel(q_ref, k_ref, v_ref, ck_ref, o_ref, m_ref, acc_ref, *, tq, tk, chunk):
    i = pl.program_id(1)
    lane = lax.broadcasted_iota(jnp.int32, (tq, LANES), 1)
    n_full, n_all = _causal_bounds(i, tq, tk)
    lo_half = lane < FOX_HEAD_DIM
    m_ref[...] = jnp.full(m_ref.shape, NEG, F32)
    acc_ref[...] = jnp.zeros(acc_ref.shape, F32)

    def step(j, masked):
        keys = pl.ds(pl.multiple_of(j * tk, tk), tk)

        def scores(h):
            cols = slice((h // 2) * LANES, (h // 2 + 1) * LANES)
            qp = q_ref[:, cols]
            sel = lo_half if h % 2 == 0 else jnp.logical_not(lo_half)
            qm = jnp.where(sel, qp, jnp.zeros_like(qp))
            return _mm_nt(qm, k_ref[keys, cols])

        def values(h, s_all):
            cols = slice((h // 2) * LANES, (h // 2 + 1) * LANES)
            vb = _with_ones(v_ref[keys, cols])
            ck = ck_ref[0, h, pl.ds(j, 1), :]
            for c in range(tq // chunk):
                rows = slice(c * chunk, (c + 1) * chunk)
                s = s_all[rows] - ck
                if masked:
                    row = lax.broadcasted_iota(jnp.int32, s.shape, 0)
                    col = lax.broadcasted_iota(jnp.int32, s.shape, 1)
                    s = jnp.where(j * tk + col <= i * tq + c * chunk + row, s, NEG)
                _flash_update(s, vb, m_ref.at[h, rows], acc_ref.at[h, rows])

        s_next = scores(0)
        for h in range(FOX_HEADS):
            s_cur = s_next
            if h + 1 < FOX_HEADS:
                s_next = scores(h + 1)
            values(h, s_cur)

    lax.fori_loop(0, n_full, lambda j, c: (step(j, False), c)[1], 0)
    lax.fori_loop(n_full, n_all, lambda j, c: (step(j, True), c)[1], 0)
    for pair in range(FOX_HEADS // 2):
        o0 = acc_ref[2 * pair, :, 0:LANES] / acc_ref[2 * pair, :, LANES:2 * LANES]
        o1 = acc_ref[2 * pair + 1, :, 0:LANES] / acc_ref[2 * pair + 1, :, LANES:2 * LANES]
        o_ref[:, pair * LANES:(pair + 1) * LANES] = jnp.where(lo_half, o0, o1).astype(BF16)


def _fox_prompt(fq, fk, fv, ck4, *, batch, seq, tq, tk):
    t = fq.shape[0]
    nq = seq // tq
    nk = seq // tk
    return pl.pallas_call(
        functools.partial(_fox_prompt_kernel, tq=tq, tk=tk, chunk=128),
        grid=(batch, nq),
        in_specs=[pl.BlockSpec((tq, FOX_W), lambda b, i: (b * nq + i, 0)),
                  pl.BlockSpec((seq, FOX_W), lambda b, i: (b, 0)),
                  pl.BlockSpec((seq, FOX_W), lambda b, i: (b, 0)),
                  pl.BlockSpec((1, FOX_HEADS, nk, tk), lambda b, i: (b, 0, 0, 0))],
        out_specs=pl.BlockSpec((tq, FOX_W), lambda b, i: (b * nq + i, 0)),
        out_shape=jax.ShapeDtypeStruct((t, FOX_W), BF16),
        scratch_shapes=[pltpu.VMEM((FOX_HEADS, tq, LANES), F32),
                        pltpu.VMEM((FOX_HEADS, tq, 2 * LANES), F32)],
        compiler_params=_cparams(("parallel", "parallel"), VMEM_LIMIT),
        name="fox_prompt",
    )(fq, fk, fv, ck4)


def _route(r):
    lane = lax.broadcasted_iota(jnp.int32, r.shape, 1)
    lanef = lane.astype(F32)
    big = jnp.float32(1e9)
    gmask = lane < N_GROUPS
    gl = jnp.where(gmask, r, -jnp.inf)
    gm = jnp.max(gl, axis=1, keepdims=True)
    gs = jnp.sum(jnp.where(gmask, jnp.exp(r - gm), 0.0), axis=1, keepdims=True)
    g_w = 1.0 / gs
    g_idx = jnp.min(jnp.where(gl == gm, lanef, big), axis=1, keepdims=True)
    egrp = ((lane - N_GROUPS) >> 3).astype(F32)
    emask = (lane >= N_GROUPS) & (lane < N_GROUPS + N_EXPERTS) & (egrp == g_idx)
    el = jnp.where(emask, r, -jnp.inf)
    em = jnp.max(el, axis=1, keepdims=True)
    ee = jnp.where(emask, jnp.exp(r - em), 0.0)
    p = ee / jnp.sum(ee, axis=1, keepdims=True)
    pm = jnp.where(emask, p, -1.0)
    p1 = jnp.max(pm, axis=1, keepdims=True)
    i1 = jnp.min(jnp.where(pm == p1, lanef, big), axis=1, keepdims=True)
    pm2 = jnp.where(lanef == i1, -1.0, pm)
    p2 = jnp.max(pm2, axis=1, keepdims=True)
    i2 = jnp.min(jnp.where(pm2 == p2, lanef, big), axis=1, keepdims=True)
    den = p1 + p2
    c1 = g_w * (p1 / den)
    c2 = g_w * (p2 / den)
    out = jnp.where(lane == 0, i1 - N_GROUPS,
                    jnp.where(lane == 1, i2 - N_GROUPS,
                              jnp.where(lane == 2, c1, jnp.where(lane == 3, c2, 0.0))))
    return out


def _post_common(x, mod, mixed, gpost, gffn, wr_hi, wr_lo, br):
    gate_m = mod[:, 2 * D_MODEL:3 * D_MODEL]
    shift_f = mod[:, 3 * D_MODEL:4 * D_MODEL]
    scale_f = mod[:, 4 * D_MODEL:5 * D_MODEL]
    x1 = x + gate_m * _rms(mixed, gpost)
    h2 = _rms(x1, gffn) * (1.0 + scale_f) + shift_f
    r = _mm3(h2, wr_hi, wr_lo) + br
    return x1, h2, _route(r)


def _post_prompt_kernel(x_ref, mod_ref, ol_ref, of_ref, wov_ref, wof_ref, gpost_ref, gffn_ref,
                        wrh_ref, wrl_ref, br_ref, x1_ref, h2_ref, rt_ref, cnt_ref, carry_ref):
    i = pl.program_id(0)
    tm = x_ref.shape[0]
    mixed = _mm(ol_ref[...], wov_ref[...]) + _mm(of_ref[...], wof_ref[...])
    x1, h2, rt = _post_common(x_ref[...], mod_ref[0], mixed, gpost_ref[...], gffn_ref[...],
                              wrh_ref[...], wrl_ref[...], br_ref[...])
    x1_ref[...] = x1
    h2_ref[...] = h2

    @pl.when(i == 0)
    def _():
        carry_ref[...] = jnp.zeros_like(carry_ref)

    lane = lax.broadcasted_iota(jnp.int32, rt.shape, 1)
    lanef = lane.astype(F32)
    oh1 = (lanef == rt[:, 0:1]).astype(F32)
    oh2 = (lanef == rt[:, 1:2]).astype(F32)
    row = lax.broadcasted_iota(jnp.int32, (tm, tm), 0)
    col = lax.broadcasted_iota(jnp.int32, (tm, tm), 1)
    before = _mm((row > col).astype(BF16), (oh1 + oh2).astype(BF16)) + carry_ref[0:1, :]
    rank1 = jnp.sum(oh1 * before, axis=1, keepdims=True)
    rank2 = jnp.sum(oh2 * (before + oh1), axis=1, keepdims=True)
    total = before[tm - 1:tm, :] + oh1[tm - 1:tm, :] + oh2[tm - 1:tm, :]
    carry_ref[0:1, :] = total
    cnt_ref[...] = jnp.broadcast_to(total, cnt_ref.shape)
    rt_ref[...] = jnp.where(lane == 4, rank1, jnp.where(lane == 5, rank2, rt))


def _post_prompt(x2d, mod3, o_lat, o_fox, wov, wof, gpost, gffn, wr, br, *, seq, tm):
    t = x2d.shape[0]
    tpb = seq // tm
    row = lambda w: pl.BlockSpec((tm, w), lambda i: (i, 0))
    return pl.pallas_call(
        _post_prompt_kernel,
        grid=(t // tm,),
        in_specs=[row(D_MODEL),
                  pl.BlockSpec((1, 1, 6 * D_MODEL), lambda i: (i // tpb, 0, 0)),
                  row(MLA_HEADS * KV_LORA), row(FOX_W),
                  _full(wov.shape), _full(wof.shape), _full(gpost.shape), _full(gffn.shape),
                  _full(wr[0].shape), _full(wr[1].shape), _full(br.shape)],
        out_specs=[row(D_MODEL), row(D_MODEL), row(LANES), _full((8, LANES))],
        out_shape=[jax.ShapeDtypeStruct((t, D_MODEL), F32),
                   jax.ShapeDtypeStruct((t, D_MODEL), F32),
                   jax.ShapeDtypeStruct((t, LANES), F32),
                   jax.ShapeDtypeStruct((8, LANES), F32)],
        scratch_shapes=[pltpu.VMEM((8, LANES), F32)],
        compiler_params=_cparams(("arbitrary",), VMEM_LIMIT),
        name="post_prompt",
    )(x2d, mod3, o_lat, o_fox, wov, wof, gpost, gffn, wr[0], wr[1], br)


def _post_sample_kernel(x_ref, mod_ref, ol_ref, of_ref, wovh_ref, wovl_ref, wofh_ref, wofl_ref,
                        gpost_ref, gffn_ref, wrh_ref, wrl_ref, br_ref, x1_ref, h2_ref, rt_ref):
    mixed = (_mm3(ol_ref[...], wovh_ref[...], wovl_ref[...])
             + _mm3(of_ref[...], wofh_ref[...], wofl_ref[...]))
    x1, h2, rt = _post_common(x_ref[...], mod_ref[...], mixed, gpost_ref[...], gffn_ref[...],
                              wrh_ref[...], wrl_ref[...], br_ref[...])
    x1_ref[...] = x1
    h2_ref[...] = h2
    rt_ref[...] = rt


def _post_sample(x2d, mod_tok, o_lat, o_fox, wov, wof, gpost, gffn, wr, br):
    t = x2d.shape[0]
    ins = [x2d, mod_tok, o_lat, o_fox, *wov, *wof, gpost, gffn, *wr, br]
    shapes = [(t, D_MODEL), (t, D_MODEL), (t, LANES)]
    return pl.pallas_call(
        _post_sample_kernel,
        grid=(1,),
        in_specs=[_full(a.shape) for a in ins],
        out_specs=[_full(s) for s in shapes],
        out_shape=[jax.ShapeDtypeStruct(s, F32) for s in shapes],
        compiler_params=_cparams(("arbitrary",), VMEM_LIMIT),
        name="post_sample",
    )(*ins)


def _dispatch_kernel(pos_ref, h2_ref, zero_ref, xs_ref, sem):
    del zero_ref
    i = pl.program_id(0)
    tm = h2_ref.shape[0]
    base = i * (2 * tm)

    def body(t, c):
        for k in range(2):
            p = pos_ref[base + 2 * t + k]
            pltpu.make_async_copy(h2_ref.at[pl.ds(t, 1)], xs_ref.at[pl.ds(p, 1)], sem).start()
        return c

    lax.fori_loop(0, tm, body, 0, unroll=8)
    for _ in range(2):
        pltpu.make_async_copy(h2_ref, xs_ref.at[pl.ds(0, tm)], sem).wait()


def _dispatch(pos, h2, n_rows, *, tm):
    t = h2.shape[0]
    grid_spec = pltpu.PrefetchScalarGridSpec(
        num_scalar_prefetch=1,
        grid=(t // tm,),
        in_specs=[pl.BlockSpec((tm, D_MODEL), lambda i, pos: (i, 0)),
                  pl.BlockSpec(memory_space=pl.ANY)],
        out_specs=pl.BlockSpec(memory_space=pl.ANY),
        scratch_shapes=[pltpu.SemaphoreType.DMA],
    )
    return pl.pallas_call(
        _dispatch_kernel,
        grid_spec=grid_spec,
        out_shape=jax.ShapeDtypeStruct((n_rows, D_MODEL), F32),
        input_output_aliases={2: 0},
        compiler_params=_cparams(("arbitrary",), VMEM_LIMIT),
        name="moe_dispatch",
    )(pos, h2, jnp.zeros((n_rows, D_MODEL), F32))


def _gmm_kernel(te_ref, nu_ref, x_ref, wg_ref, wu_ref, wd_ref, o_ref):
    i = pl.program_id(0)

    @pl.when(i < nu_ref[0])
    def _():
        x = x_ref[...].astype(BF16)
        a = _mm(x, wg_ref[0].astype(BF16))
        u = _mm(x, wu_ref[0].astype(BF16))
        act = (_silu(a) * u).astype(BF16)
        o_ref[...] = _mm(act, wd_ref[0].astype(BF16))

    @pl.when(i >= nu_ref[0])
    def _():
        o_ref[...] = jnp.zeros_like(o_ref)


def _gmm(tile_expert, n_used, xs, wg, wu, wd, *, tm):
    nr = xs.shape[0]
    grid_spec = pltpu.PrefetchScalarGridSpec(
        num_scalar_prefetch=2,
        grid=(nr // tm,),
        in_specs=[pl.BlockSpec((tm, D_MODEL), lambda i, te, nu: (i, 0)),
                  pl.BlockSpec((1, D_MODEL, D_EXPERT), lambda i, te, nu: (te[i], 0, 0)),
                  pl.BlockSpec((1, D_MODEL, D_EXPERT), lambda i, te, nu: (te[i], 0, 0)),
                  pl.BlockSpec((1, D_EXPERT, D_MODEL), lambda i, te, nu: (te[i], 0, 0))],
        out_specs=pl.BlockSpec((tm, D_MODEL), lambda i, te, nu: (i, 0)),
    )
    return pl.pallas_call(
        _gmm_kernel,
        grid_spec=grid_spec,
        out_shape=jax.ShapeDtypeStruct((nr, D_MODEL), F32),
        compiler_params=_cparams(("arbitrary",), VMEM_LIMIT),
        name="moe_gmm",
    )(tile_expert, n_used, xs, wg, wu, wd)


def _combine_kernel(pos_ref, x1_ref, mod_ref, rt_ref, g_ref, ys_ref, o_ref, buf_ref, sem):
    i = pl.program_id(0)
    n = pl.num_programs(0)
    tm = x1_ref.shape[0]

    def issue(tile, slot):
        base = tile * (2 * tm)

        def body(t, c):
            for k in range(2):
                p = pos_ref[base + 2 * t + k]
                pltpu.make_async_copy(ys_ref.at[pl.ds(p, 1)], buf_ref.at[slot, k, pl.ds(t, 1)],
                                      sem.at[slot]).start()
            return c

        lax.fori_loop(0, tm, body, 0, unroll=8)

    @pl.when(i == 0)
    def _():
        issue(0, 0)

    @pl.when(i + 1 < n)
    def _():
        issue(i + 1, (i + 1) % 2)

    slot = i % 2
    for k in range(2):
        pltpu.make_async_copy(ys_ref.at[pl.ds(0, tm)], buf_ref.at[slot, k], sem.at[slot]).wait()
    gate_f = mod_ref[0][:, 5 * D_MODEL:6 * D_MODEL]
    rt = rt_ref[...]
    y = rt[:, 2:3] * buf_ref[slot, 0] + rt[:, 3:4] * buf_ref[slot, 1]
    o_ref[...] = x1_ref[...] + gate_f * _rms(y, g_ref[...])


def _combine(pos, x1, mod3, route, gpost, ys, *, seq, tm):
    t = x1.shape[0]
    tpb = seq // tm
    row = lambda w: pl.BlockSpec((tm, w), lambda i, pos: (i, 0))
    grid_spec = pltpu.PrefetchScalarGridSpec(
        num_scalar_prefetch=1,
        grid=(t // tm,),
        in_specs=[row(D_MODEL),
                  pl.BlockSpec((1, 1, 6 * D_MODEL), lambda i, pos: (i // tpb, 0, 0)),
                  row(LANES),
                  pl.BlockSpec(gpost.shape, lambda i, pos: (0, 0)),
                  pl.BlockSpec(memory_space=pl.ANY)],
        out_specs=row(D_MODEL),
        scratch_shapes=[pltpu.VMEM((2, 2, tm, D_MODEL), F32), pltpu.SemaphoreType.DMA((2,))],
    )
    return pl.pallas_call(
        _combine_kernel,
        grid_spec=grid_spec,
        out_shape=jax.ShapeDtypeStruct((t, D_MODEL), F32),
        compiler_params=_cparams(("arbitrary",), VMEM_LIMIT),
        name="moe_combine",
    )(pos, x1, mod3, route, gpost, ys)


def _moe_sample_kernel(h2_ref, rt_ref, x1_ref, mod_ref, g_ref, wg_ref, wu_ref, wd_ref,
                       o_ref, acc_ref):
    e = pl.program_id(0)

    @pl.when(e == 0)
    def _():
        acc_ref[...] = jnp.zeros_like(acc_ref)

    h2 = h2_ref[...]
    rt = rt_ref[...]
    ef = e.astype(F32)
    cw = (jnp.where(rt[:, 0:1] == ef, rt[:, 2:3], 0.0)
          + jnp.where(rt[:, 1:2] == ef, rt[:, 3:4], 0.0))
    a = _mm3(h2, *_split(wg_ref[0]))
    u = _mm3(h2, *_split(wu_ref[0]))
    act = _silu(a) * u * cw
    acc_ref[...] += _mm3(act, *_split(wd_ref[0]))

    @pl.when(e == pl.num_programs(0) - 1)
    def _():
        gate_f = mod_ref[:, 5 * D_MODEL:6 * D_MODEL]
        o_ref[...] = x1_ref[...] + gate_f * _rms(acc_ref[...], g_ref[...])


def _moe_sample(h2, route, x1, mod_tok, gpost, wg, wu, wd):
    t = h2.shape[0]
    return pl.pallas_call(
        _moe_sample_kernel,
        grid=(N_EXPERTS,),
        in_specs=[_full(h2.shape), _full(route.shape), _full(x1.shape), _full(mod_tok.shape),
                  _full(gpost.shape),
                  pl.BlockSpec((1, D_MODEL, D_EXPERT), lambda e: (e, 0, 0)),
                  pl.BlockSpec((1, D_MODEL, D_EXPERT), lambda e: (e, 0, 0)),
                  pl.BlockSpec((1, D_EXPERT, D_MODEL), lambda e: (e, 0, 0))],
        out_specs=_full((t, D_MODEL)),
        out_shape=jax.ShapeDtypeStruct((t, D_MODEL), F32),
        scratch_shapes=[pltpu.VMEM((t, D_MODEL), F32)],
        compiler_params=_cparams(("arbitrary",), VMEM_LIMIT),
        name="moe_sample",
    )(h2, route, x1, mod_tok, gpost, wg, wu, wd)


def _rev_cumsum_lanes(x):
    lane = lax.broadcasted_iota(jnp.int32, x.shape, 1)
    sh = 1
    while sh < LANES:
        x = x + jnp.where(lane + sh < LANES, pltpu.roll(x, LANES - sh, 1), 0.0)
        sh *= 2
    return x


def _fwd_cumsum_lanes(x):
    lane = lax.broadcasted_iota(jnp.int32, x.shape, 1)
    sh = 1
    while sh < LANES:
        x = x + jnp.where(lane >= sh, pltpu.roll(x, sh, 1), 0.0)
        sh *= 2
    return x


def _seg_update(s, m_ref, l_ref):
    m_prev = m_ref[...]
    m_new = jnp.maximum(m_prev, jnp.max(s, axis=1, keepdims=True))
    alpha = jnp.exp2(m_prev - m_new)
    p = jnp.exp2(s - m_new)
    l_ref[...] = alpha * l_ref[...] + jnp.sum(p, axis=1, keepdims=True)
    m_ref[...] = m_new
    return p, alpha


def _col_to_lanes(col, fill=0.0):
    n = col.shape[0]
    row = lax.broadcasted_iota(jnp.int32, (n, LANES), 0)
    lane = lax.broadcasted_iota(jnp.int32, (n, LANES), 1)
    vec = jnp.sum(jnp.where(row == lane, col, 0.0), axis=0, keepdims=True)
    return jnp.where(lane[0:1, :] < n, vec, fill)


def _sample_attn_kernel(pt_ref, qa_ref, qr_ref, qf_ref, latn_ref, krn_ref, fkn_ref, fvn_ref,
                        lfn_ref, clat_ref, ckr_ref, cfk_ref, cfv_ref, clf_ref, om_ref, of_ref,
                        mm_ref, lm_ref, accm_ref, mf_ref, lfs_ref, accf_ref, carry_ref,
                        lat_buf, kr_buf, fk_buf, fv_buf, lf_buf, sem_a, sem_b, *, pages, n_chunks):
    npg = pages
    b = pl.program_id(0)
    c = pl.program_id(1)
    nrow = qa_ref.shape[1]
    n_steps = pl.num_programs(0) * n_chunks
    step = b * n_chunks + c
    slot = step % 2

    def start(chunk_step, dst_slot, group_a):
        cb = chunk_step // n_chunks
        cc = chunk_step % n_chunks
        for p in range(npg):
            page = pt_ref[cb, (n_chunks - 1 - cc) * npg + p]
            keys = pl.ds(p * LANES, LANES)
            if group_a:
                sem = sem_a.at[dst_slot]
                pltpu.make_async_copy(clat_ref.at[page, 0], lat_buf.at[dst_slot, keys, :],
                                      sem).start()
                pltpu.make_async_copy(ckr_ref.at[page, 0], kr_buf.at[dst_slot, :, keys],
                                      sem).start()
                pltpu.make_async_copy(cfk_ref.at[page, 0], fk_buf.at[dst_slot, :, :, keys],
                                      sem).start()
                pltpu.make_async_copy(clf_ref.at[page, 0], lf_buf.at[dst_slot, :, keys],
                                      sem).start()
            else:
                pltpu.make_async_copy(cfv_ref.at[page, 0], fv_buf.at[dst_slot, :, :, keys],
                                      sem_b.at[dst_slot]).start()

    def wait(dst_slot, group_a):
        bufs = (lat_buf, kr_buf, fk_buf, lf_buf) if group_a else (fv_buf,)
        sem = sem_a.at[dst_slot] if group_a else sem_b.at[dst_slot]
        for buf in bufs:
            pltpu.make_async_copy(buf.at[dst_slot], buf.at[dst_slot], sem).wait()

    nxt = jnp.minimum(step + 1, n_steps - 1)

    @pl.when(step == 0)
    def _():
        start(step, slot, True)
        start(step, slot, False)

    start(nxt, 1 - slot, True)
    wait(slot, True)

    @pl.when(c == 0)
    def _():
        mm_ref[...] = jnp.full(mm_ref.shape, NEG, F32)
        mf_ref[...] = jnp.full(mf_ref.shape, NEG, F32)
        lm_ref[...] = jnp.zeros(lm_ref.shape, F32)
        lfs_ref[...] = jnp.zeros(lfs_ref.shape, F32)
        accm_ref[...] = jnp.zeros(accm_ref.shape, F32)
        accf_ref[...] = jnp.zeros(accf_ref.shape, F32)
        carry_ref[...] = jnp.zeros(carry_ref.shape, F32)

    def stack(x):
        hi, lo = _split(x)
        return hi, jnp.concatenate([hi, lo], axis=0)

    qa_hi, qa_cat = stack(qa_ref[0])
    qr_hi, qr_cat = stack(qr_ref[0][:, 0:QK_ROPE])
    qf_hi, qf_cat = stack(qf_ref[0])

    def attend(lat, krt, fkt, get_fvt, bias_f, bias_m, ngrp, between=None):
        width = lat.shape[0] // ngrp
        groups = []
        for g in range(ngrp):
            ks = slice(g * width, (g + 1) * width)
            lh, ll = _split(lat[ks, :])
            kh, kl = _split(krt[:, ks])
            fh, fl = _split(fkt[:, ks])
            top = _mm_nt(qa_cat, lh) + _mm(qr_cat, kh)
            s_m = top[:nrow] + top[nrow:] + _mm_nt(qa_hi, ll) + _mm(qr_hi, kl)
            top = _mm(qf_cat, fh)
            s_f = (top[:nrow] + top[nrow:] + _mm(qf_hi, fl)
                   + bias_f[:, g * width:(g + 1) * width])
            if bias_m is not None:
                s_m = s_m + bias_m
            groups.append((s_m, s_f, lh, ll))
        if between is not None:
            between()
        fvt = get_fvt()
        for g in range(ngrp):
            s_m, s_f, lh, ll = groups[g]
            ks = slice(g * width, (g + 1) * width)
            p_m, a_m = _seg_update(s_m, mm_ref, lm_ref)
            p_f, a_f = _seg_update(s_f, mf_ref, lfs_ref)
            pm_hi, pm_lo = _split(p_m)
            pf_hi, pf_lo = _split(p_f)
            zrow = jnp.zeros(pm_hi.shape, BF16)
            pm_cat = jnp.concatenate([pm_hi, pm_lo], axis=0)
            pm_hi0 = jnp.concatenate([pm_hi, zrow], axis=0)
            pf_cat = jnp.concatenate([pf_hi, pf_lo, zrow, zrow], axis=0)
            pf_hi0 = jnp.concatenate([pf_hi, zrow, zrow, zrow], axis=0)
            vh, vl = _split(fvt[:, ks])
            upd_m = _mm(pm_cat, lh) + _mm(pm_hi0, ll)
            upd_f = _mm_nt(vh, pf_cat) + _mm_nt(vl, pf_hi0)
            accm_ref[...] = jnp.concatenate([a_m, a_m], axis=0) * accm_ref[...] + upd_m
            a_lanes = _col_to_lanes(a_f)
            accf_ref[...] = (a_lanes + pltpu.roll(a_lanes, nrow, 1)) * accf_ref[...] + upd_f

    sufs = [None] * npg
    carry = carry_ref[...]
    for p in range(npg - 1, -1, -1):
        lf = lf_buf[slot, :, p * LANES:(p + 1) * LANES]
        inc = _rev_cumsum_lanes(lf)
        sufs[p] = (inc - lf + carry) * LOG2E
        carry = carry + inc[:, 0:1]
    carry_ref[...] = jnp.broadcast_to(carry[:, 0:1], carry_ref.shape)
    suf = jnp.concatenate(sufs, axis=1)
    bias_f = jnp.concatenate([suf] * (nrow // FOX_HEADS), axis=0)
    n_keys = npg * LANES

    def request_values():
        start(nxt, 1 - slot, False)
        wait(slot, False)

    attend(lat_buf[slot], kr_buf[slot], fk_buf[slot].reshape(FOX_W, n_keys),
           lambda: fv_buf[slot].reshape(FOX_W, n_keys), bias_f, None, 2, request_values)

    @pl.when(step == n_steps - 1)
    def _():
        wait(1 - slot, True)
        wait(1 - slot, False)

    @pl.when(c == n_chunks - 1)
    def _():
        row = lax.broadcasted_iota(jnp.int32, (nrow, LANES), 0)
        lane = lax.broadcasted_iota(jnp.int32, (nrow, LANES), 1)
        ok = lane <= (row >> 3)
        cum_new = _fwd_cumsum_lanes(lfn_ref[0]) * LOG2E
        b_f = jnp.where(ok, -jnp.concatenate([cum_new] * (nrow // FOX_HEADS), axis=0), NEG)
        b_m = jnp.where(ok, 0.0, NEG)
        attend(latn_ref[0], krn_ref[0], fkn_ref[0], lambda: fvn_ref[0], b_f, b_m, 1)
        accm = accm_ref[...]
        om_ref[0] = (accm[:nrow] + accm[nrow:]) / lm_ref[...]
        accf = accf_ref[...]
        of_ref[0] = (accf + pltpu.roll(accf, LANES - nrow, 1)) / _col_to_lanes(lfs_ref[...], 1.0)


def _sample_attn(page_table, qa, qr, qf, latn, krn, fkn, fvn, lfn,
                 c_lat, c_kr, c_fk, c_fv, c_lf, *, pages):
    nb, n_pages = page_table.shape
    n_chunks = n_pages // pages
    nrow = qa.shape[1]

    def per_b(shape):
        nd = len(shape)
        return pl.BlockSpec((1,) + shape[1:], lambda b, c, pt: (b,) + (0,) * (nd - 1))

    small = [qa, qr, qf, latn, krn, fkn, fvn, lfn]
    caches = [c_lat, c_kr, c_fk, c_fv, c_lf]
    in_specs = [per_b(a.shape) for a in small] + [pl.BlockSpec(memory_space=pl.ANY)] * len(caches)
    operands = small + caches
    n_keys = pages * c_lat.shape[2]
    grid_spec = pltpu.PrefetchScalarGridSpec(
        num_scalar_prefetch=1,
        grid=(nb, n_chunks),
        in_specs=in_specs,
        out_specs=[pl.BlockSpec((1, nrow, KV_LORA), lambda b, c, pt: (b, 0, 0)),
                   pl.BlockSpec((1, FOX_W, LANES), lambda b, c, pt: (b, 0, 0))],
        scratch_shapes=[pltpu.VMEM((nrow, 1), F32), pltpu.VMEM((nrow, 1), F32),
                        pltpu.VMEM((2 * nrow, KV_LORA), F32),
                        pltpu.VMEM((nrow, 1), F32), pltpu.VMEM((nrow, 1), F32),
                        pltpu.VMEM((FOX_W, LANES), F32),
                        pltpu.VMEM((FOX_HEADS, LANES), F32),
                        pltpu.VMEM((2, n_keys, KV_LORA), F32),
                        pltpu.VMEM((2, QK_ROPE, n_keys), F32),
                        pltpu.VMEM((2, FOX_HEADS, FOX_HEAD_DIM, n_keys), F32),
                        pltpu.VMEM((2, FOX_HEADS, FOX_HEAD_DIM, n_keys), F32),
                        pltpu.VMEM((2, FOX_HEADS, n_keys), F32),
                        pltpu.SemaphoreType.DMA((2,)), pltpu.SemaphoreType.DMA((2,))],
    )
    return pl.pallas_call(
        functools.partial(_sample_attn_kernel, pages=pages, n_chunks=n_chunks),
        grid_spec=grid_spec,
        out_shape=[jax.ShapeDtypeStruct((nb, nrow, KV_LORA), F32),
                   jax.ShapeDtypeStruct((nb, FOX_W, LANES), F32)],
        compiler_params=_cparams(("arbitrary", "arbitrary"), VMEM_LIMIT),
        name="sample_attn",
    )(page_table, *operands)


def _hl(w):
    bits = lax.bitcast_convert_type(w, jnp.uint32) & jnp.uint32(0xFFFF0000)
    hi = lax.bitcast_convert_type(bits, F32)
    return hi.astype(BF16), (w - hi).astype(BF16)


def _rope_tables(pos):
    half = QK_ROPE // 2
    inv = ROPE_THETA ** (-jnp.arange(half, dtype=F32) / half)
    ang = pos.astype(F32)[:, None] * inv[None, :]
    cc = jnp.concatenate([jnp.cos(ang), jnp.cos(ang)], axis=1)
    ss = jnp.concatenate([jnp.sin(ang), jnp.sin(ang)], axis=1)
    n = pos.shape[0]
    qs = MLA_SCALE * LOG2E
    ctab = jnp.concatenate([jnp.full((n, KV_LORA), qs, F32), cc * qs,
                            jnp.zeros((n, QHEAD_W - KV_LORA - QK_ROPE), F32)], axis=1)
    stab = jnp.concatenate([jnp.zeros((n, KV_LORA), F32), ss * qs,
                            jnp.zeros((n, QHEAD_W - KV_LORA - QK_ROPE), F32)], axis=1)
    zpad = jnp.zeros((n, LANES - QK_ROPE), F32)
    return ctab, stab, jnp.concatenate([cc, zpad], axis=1), jnp.concatenate([ss, zpad], axis=1)


def _swap_halves(w):
    half = QK_ROPE // 2
    return jnp.concatenate([-w[..., half:], w[..., :half]], axis=-1)


def kernel(x_prompt, x_sample, cache_mla_latent, cache_mla_krope, cache_fox_k, cache_fox_v,
           cache_fox_logf, page_table, c_prompt, c_sample, w_ada, b_ada, g_pre_mix, g_post_mix,
           g_pre_ffn, g_post_ffn, w_in, b_forget, g_q_lora, g_kv_lora, w_uq, w_uk, w_uv, w_o,
           w_group_router, b_group_router, w_expert_router, b_expert_router, w_gate, w_up, w_down):
    assert w_ada.shape[0] == 1, "single-layer trunk"
    batch, seq, d = x_prompt.shape
    dec_batch, n_new, _ = x_sample.shape
    n_pages = page_table.shape[1]
    page = cache_mla_latent.shape[2]
    past_len = n_pages * page
    tp = batch * seq
    ts = dec_batch * n_new

    w_in0 = w_in[0]
    s0, s1, s2, s3, s4, s5 = 256, 384, 416, 928, 1440, 1952
    w_kr = w_in0[:, s1:s2]
    z = lambda n: jnp.zeros((d, n), F32)
    w_all = jnp.concatenate([
        w_in0[:, 0:s0],
        w_in0[:, s0:s1], w_kr, z(96),
        _swap_halves(w_kr), z(96),
        w_in0[:, s2:s3], w_in0[:, s3:s4], w_in0[:, s4:s5],
        w_in0[:, s5:], z(120)], axis=1)
    w_uq0 = w_uq[0]
    w_qabs = _fold(jnp.transpose(w_uq0[:, :, :QK_NOPE], (1, 0, 2)),
                   jnp.transpose(w_uk[0], (0, 2, 1)))
    w_qr = jnp.transpose(w_uq0[:, :, QK_NOPE:], (1, 0, 2))
    zq = lambda n: jnp.zeros((MLA_HEADS, Q_LORA, n), F32)
    w_a = jnp.concatenate([w_qabs, w_qr, zq(96)], axis=2)
    w_b = jnp.concatenate([zq(128), _swap_halves(w_qr), zq(96)], axis=2)
    w_a = jnp.transpose(w_a, (1, 0, 2)).reshape(Q_LORA, MLA_HEADS * QHEAD_W)
    w_b = jnp.transpose(w_b, (1, 0, 2)).reshape(Q_LORA, MLA_HEADS * QHEAD_W)
    w_o0 = w_o[0]
    w_ov = _fold(w_uv[0], w_o0[:MLA_HEADS * V_HEAD].reshape(MLA_HEADS, V_HEAD, d))
    w_ov = w_ov.reshape(MLA_HEADS * KV_LORA, d)
    w_of = w_o0[MLA_HEADS * V_HEAD:]
    w_r = jnp.concatenate([w_group_router[0], w_expert_router[0].reshape(d, N_EXPERTS),
                           jnp.zeros((d, LANES - N_GROUPS - N_EXPERTS), F32)], axis=1)
    b_r = jnp.concatenate([b_group_router[0], b_expert_router[0].reshape(N_EXPERTS),
                           jnp.zeros((LANES - N_GROUPS - N_EXPERTS,), F32)])[None, :]
    bfg = jnp.concatenate([b_forget[0], jnp.zeros((LANES - FOX_HEADS,), F32)])[None, :]
    w_all_hl, w_a_hl, w_b_hl = _hl(w_all), _hl(w_a), _hl(w_b)
    w_ov_hl, w_of_hl, w_r_hl = _hl(w_ov), _hl(w_of), _hl(w_r)
    wg = w_gate[0].reshape(N_EXPERTS, d, D_EXPERT)
    wu = w_up[0].reshape(N_EXPERTS, d, D_EXPERT)
    wd = w_down[0].reshape(N_EXPERTS, D_EXPERT, d)

    c_all = jnp.concatenate([c_prompt, c_sample], axis=0)
    c_all = jnp.pad(c_all, ((0, -c_all.shape[0] % 64), (0, 0)))
    mod = _ada(c_all, w_ada[0], b_ada)
    mod_p = mod[:batch].reshape(batch, 1, 6 * d)
    mod_s = jnp.repeat(mod[batch:batch + dec_batch], n_new, axis=0)

    tm = 512
    xp2 = x_prompt.reshape(tp, d)
    tabs_p = _rope_tables(jnp.arange(seq))
    (q_p, kmla_p, ckv_p, kr_p, fq_p, fk_p, fk16_p, fv_p, fv16_p, lf_p, cum_p) = _proj_prompt(
        xp2, mod_p, g_pre_mix, g_q_lora, g_kv_lora, bfg, tabs_p,
        w_all.astype(BF16), w_a.astype(BF16), w_b.astype(BF16), seq=seq, tm=tm)
    tk = 512
    o_lat_p = _mla_prompt(q_p, kmla_p, batch=batch, seq=seq, tq=128, tk=tk)
    ck4 = cum_p.reshape(batch, FOX_HEADS, seq // tk, tk)
    o_fox_p = _fox_prompt(fq_p, fk16_p, fv16_p, ck4, batch=batch, seq=seq, tq=512, tk=tk)
    x1_p, h2_p, rt_p, cnt_p = _post_prompt(xp2, mod_p, o_lat_p, o_fox_p,
                                           w_ov.astype(BF16), w_of.astype(BF16),
                                           g_post_mix, g_pre_ffn, w_r_hl, b_r, seq=seq, tm=tm)

    tme = 256
    n_asg = 2 * tp
    n_rows = n_asg + N_EXPERTS * tme
    cnt = cnt_p[0, :N_EXPERTS].astype(jnp.int32)
    pend = jnp.cumsum(((cnt + tme - 1) // tme) * tme)
    offs = pend - ((cnt + tme - 1) // tme) * tme
    tile_start = jnp.arange(n_rows // tme, dtype=jnp.int32) * tme
    tile_e = jnp.sum(tile_start[:, None] >= pend[None, :], axis=1, dtype=jnp.int32)
    tile_e = jnp.minimum(tile_e, N_EXPERTS - 1)
    n_used = (pend[-1] // tme).astype(jnp.int32)[None]
    e_ids = rt_p[:, 0:2]
    hit = e_ids[:, :, None] == jnp.arange(N_EXPERTS, dtype=F32)[None, None, :]
    offs_tok = jnp.sum(jnp.where(hit, offs[None, None, :], 0), axis=-1, dtype=jnp.int32)
    pos = (offs_tok + rt_p[:, 4:6].astype(jnp.int32)).reshape(n_asg)
    xs = _dispatch(pos, h2_p, n_rows, tm=256)
    ys = _gmm(tile_e, n_used, xs, wg, wu, wd, tm=tme)
    y_prompt = _combine(pos, x1_p, mod_p, rt_p, g_post_ffn, ys, seq=seq, tm=256)

    xs2 = x_sample.reshape(ts, d)
    pos_s = jnp.tile(past_len + jnp.arange(n_new), dec_batch)
    tabs_s = _rope_tables(pos_s)
    q_s, ckv_s, krp_s, fq_s, fk_s, fv_s, lfp_s = _proj_sample(
        xs2, mod_s, g_pre_mix, g_q_lora, g_kv_lora, bfg, tabs_s, w_all_hl, w_a_hl, w_b_hl)
    q4 = q_s.reshape(dec_batch, n_new, MLA_HEADS, QHEAD_W)
    nrow = n_new * MLA_HEADS
    qa = q4[..., :KV_LORA].reshape(dec_batch, nrow, KV_LORA)
    qr = q4[..., KV_LORA:].reshape(dec_batch, nrow, QHEAD_W - KV_LORA)
    head_mask = (jnp.arange(FOX_W)[None, :] // FOX_HEAD_DIM == jnp.arange(FOX_HEADS)[:, None])
    qf = (fq_s.reshape(dec_batch, n_new, 1, FOX_W) * head_mask[None, None].astype(F32))
    qf = qf.reshape(dec_batch, nrow, FOX_W)
    padk = page - n_new
    latn = jnp.pad(ckv_s.reshape(dec_batch, n_new, KV_LORA), ((0, 0), (0, padk), (0, 0)))
    krn = jnp.pad(jnp.transpose(krp_s[:, :QK_ROPE].reshape(dec_batch, n_new, QK_ROPE), (0, 2, 1)),
                  ((0, 0), (0, 0), (0, padk)))
    fkn = jnp.pad(jnp.transpose(fk_s.reshape(dec_batch, n_new, FOX_W), (0, 2, 1)),
                  ((0, 0), (0, 0), (0, padk)))
    fvn = jnp.pad(jnp.transpose(fv_s.reshape(dec_batch, n_new, FOX_W), (0, 2, 1)),
                  ((0, 0), (0, 0), (0, padk)))
    lfn = jnp.pad(jnp.transpose(lfp_s[:, :FOX_HEADS].reshape(dec_batch, n_new, FOX_HEADS),
                                (0, 2, 1)), ((0, 0), (0, 0), (0, padk)))
    c_lat = cache_mla_latent
    c_kr = jnp.transpose(cache_mla_krope, (0, 1, 3, 2))
    c_fk = jnp.transpose(cache_fox_k, (0, 1, 3, 4, 2))
    c_fv = jnp.transpose(cache_fox_v, (0, 1, 3, 4, 2))
    c_lf = jnp.transpose(cache_fox_logf, (0, 1, 3, 2))
    om, oft = _sample_attn(page_table, qa, qr, qf, latn, krn, fkn, fvn, lfn,
                           c_lat, c_kr, c_fk, c_fv, c_lf, pages=16)
    o_lat_s = om.reshape(dec_batch, n_new, MLA_HEADS * KV_LORA).reshape(ts, MLA_HEADS * KV_LORA)
    oft4 = oft[:, :, :nrow].reshape(dec_batch, FOX_HEADS, FOX_HEAD_DIM, n_new, FOX_HEADS)
    o_fox_s = jnp.einsum('bhdqh->bqhd', oft4).reshape(ts, FOX_W)
    x1_s, h2_s, rt_s = _post_sample(xs2, mod_s, o_lat_s, o_fox_s, w_ov_hl, w_of_hl,
                                    g_post_mix, g_pre_ffn, w_r_hl, b_r)
    y_sample = _moe_sample(h2_s, rt_s, x1_s, mod_s, g_post_ffn, wg, wu, wd)

    return (y_prompt.reshape(batch, seq, d), y_sample.reshape(dec_batch, n_new, d),
            ckv_p.reshape(batch, 1, seq, KV_LORA),
            jnp.transpose(kr_p.reshape(batch, 1, QK_ROPE, seq), (0, 1, 3, 2)),
            jnp.transpose(fk_p.reshape(batch, 1, FOX_HEADS, FOX_HEAD_DIM, seq), (0, 1, 4, 2, 3)),
            jnp.transpose(fv_p.reshape(batch, 1, FOX_HEADS, FOX_HEAD_DIM, seq), (0, 1, 4, 2, 3)),
            jnp.transpose(lf_p.reshape(batch, 1, FOX_HEADS, seq), (0, 1, 3, 2)),
            ckv_s.reshape(dec_batch, 1, n_new, KV_LORA),
            krp_s[:, :QK_ROPE].reshape(dec_batch, 1, n_new, QK_ROPE),
            fk_s.reshape(dec_batch, 1, n_new, FOX_HEADS, FOX_HEAD_DIM),
            fv_s.reshape(dec_batch, 1, n_new, FOX_HEADS, FOX_HEAD_DIM),
            lfp_s[:, :FOX_HEADS].reshape(dec_batch, 1, n_new, FOX_HEADS))
```

```python
import functools
import math

import jax
import jax.numpy as jnp
import numpy as np
from jax import lax
from jax.experimental import pallas as pl
from jax.experimental.pallas import tpu as pltpu

F32 = jnp.float32
BF16 = jnp.bfloat16

D_MODEL = 1024
MLA_HEADS = 8
QK_NOPE = 64
QK_ROPE = 32
V_HEAD = 64
Q_LORA = 256
KV_LORA = 128
FOX_HEADS = 8
FOX_HEAD_DIM = 64
FOX_W = FOX_HEADS * FOX_HEAD_DIM
N_GROUPS = 4
EXPERTS_PER_GROUP = 8
N_EXPERTS = N_GROUPS * EXPERTS_PER_GROUP
D_EXPERT = 256
ROPE_THETA = 10000.0
EPS = 1e-6
NEG = -1e30
LOG2E = 1.4426950408889634
MLA_SCALE = (QK_NOPE + QK_ROPE) ** -0.5
FOX_SCALE = FOX_HEAD_DIM ** -0.5

LANES = 128
QHEAD_W = 256
W_ALL_COLS = 2304
VMEM_LIMIT = 56 * 1024 * 1024


def _split(x):
    hi = x.astype(BF16)
    lo = (x - hi.astype(F32)).astype(BF16)
    return hi, lo


def _mm(a, b):
    return jnp.dot(a, b, preferred_element_type=F32)


def _mm_nt(a, b):
    return lax.dot_general(a, b, (((1,), (1,)), ((), ())), preferred_element_type=F32)


def _mm3(a, b_hi, b_lo):
    a_hi, a_lo = _split(a)
    m = a.shape[0]
    top = _mm(jnp.concatenate([a_hi, a_lo], axis=0), b_hi)
    return top[:m] + top[m:] + _mm(a_hi, b_lo)


def _mm3_nt(a, b_hi, b_lo):
    a_hi, a_lo = _split(a)
    m = a.shape[0]
    top = _mm_nt(jnp.concatenate([a_hi, a_lo], axis=0), b_hi)
    return top[:m] + top[m:] + _mm_nt(a_hi, b_lo)


def _rms(x, g):
    return x * lax.rsqrt(jnp.mean(x * x, axis=-1, keepdims=True) + EPS) * g


def _silu(x):
    return x / (1.0 + jnp.exp(-x))


def _log_sigmoid(x):
    return jnp.minimum(x, 0.0) - jnp.log(1.0 + jnp.exp(-jnp.abs(x)))


def _cparams(sem, vmem=None):
    return pltpu.CompilerParams(dimension_semantics=sem, vmem_limit_bytes=vmem)


def _full(shape):
    nd = len(shape)
    return pl.BlockSpec(shape, lambda *_: (0,) * nd)


def _ada_kernel(c_ref, w_ref, b_ref, o_ref):
    a = _silu(c_ref[...])
    w_hi, w_lo = _split(w_ref[...])
    o_ref[...] = _mm3(a, w_hi, w_lo) + b_ref[...]


def _ada(c_all, w_ada, b_ada):
    n, d = c_all.shape
    cols = w_ada.shape[1]
    tn = 1024
    return pl.pallas_call(
        _ada_kernel,
        grid=(cols // tn,),
        in_specs=[_full((n, d)),
                  pl.BlockSpec((d, tn), lambda j: (0, j)),
                  pl.BlockSpec((1, tn), lambda j: (0, j))],
        out_specs=pl.BlockSpec((n, tn), lambda j: (0, j)),
        out_shape=jax.ShapeDtypeStruct((n, cols), F32),
        compiler_params=_cparams(("parallel",)),
        name="ada_mod",
    )(c_all, w_ada, b_ada)


def _fold_kernel(a_ref, b_ref, o_ref):
    b_hi, b_lo = _split(b_ref[0])
    o_ref[0] = _mm3(a_ref[0], b_hi, b_lo)


def _fold(a, b):
    h, m, k = a.shape
    n = b.shape[2]
    return pl.pallas_call(
        _fold_kernel,
        grid=(h,),
        in_specs=[pl.BlockSpec((1, m, k), lambda i: (i, 0, 0)),
                  pl.BlockSpec((1, k, n), lambda i: (i, 0, 0))],
        out_specs=pl.BlockSpec((1, m, n), lambda i: (i, 0, 0)),
        out_shape=jax.ShapeDtypeStruct((h, m, n), F32),
        compiler_params=_cparams(("parallel",)),
        name="fold_weights",
    )(a, b)


def _proj_common(x, mod, g_pre, g_q, g_kv, bfg, tabs, mm_all, mm_qa, mm_qb):
    ctab, stab, kct, kst = tabs
    shift = mod[:, 0:D_MODEL]
    scale = mod[:, D_MODEL:2 * D_MODEL]
    h = _rms(x, g_pre) * (1.0 + scale) + shift
    z = mm_all(h)
    cq = _rms(z[:, 0:Q_LORA], g_q)
    nh = MLA_HEADS
    q = (mm_qa(cq) * jnp.concatenate([ctab] * nh, axis=1)
         + mm_qb(cq) * jnp.concatenate([stab] * nh, axis=1))
    c_kv = _rms(z[:, 256:384], g_kv)
    krp = z[:, 384:512] * kct + z[:, 512:640] * kst
    fq = z[:, 640:1152] * (FOX_SCALE * LOG2E)
    fk = z[:, 1152:1664]
    fv = z[:, 1664:2176]
    logf = _log_sigmoid(z[:, 2176:2304] + bfg)
    return q, c_kv, krp, fq, fk, fv, logf


def _proj_prompt_kernel(x_ref, mod_ref, gpre_ref, gq_ref, gkv_ref, bfg_ref,
                        ct_ref, st_ref, kct_ref, kst_ref,
                        wall_ref, wa_ref, wb_ref,
                        q_ref, kmla_ref, ckv_ref, kr_ref, fq_ref, fk_ref, fk16_ref,
                        fv_ref, fv16_ref, lf_ref, cum_ref, carry_ref, *, tiles_per_batch):
    i = pl.program_id(0)
    tm = x_ref.shape[0]
    q, c_kv, krp, fq, fk, fv, logf = _proj_common(
        x_ref[...], mod_ref[0], gpre_ref[...], gq_ref[...], gkv_ref[...], bfg_ref[...],
        (ct_ref[...], st_ref[...], kct_ref[...], kst_ref[...]),
        lambda h: _mm(h.astype(BF16), wall_ref[...]),
        lambda c: _mm(c.astype(BF16), wa_ref[...]),
        lambda c: _mm(c.astype(BF16), wb_ref[...]))
    q_ref[...] = q.astype(BF16)
    ckv_ref[...] = c_kv
    kr_ref[0] = jnp.transpose(krp)[0:QK_ROPE, :]
    kmla_ref[...] = jnp.concatenate([c_kv, krp], axis=1).astype(BF16)
    fq_ref[...] = fq.astype(BF16)
    fk_ref[0] = jnp.transpose(fk)
    fk16_ref[...] = fk.astype(BF16)
    fv_ref[0] = jnp.transpose(fv)
    fv16_ref[...] = fv.astype(BF16)
    lf_ref[0] = jnp.transpose(logf)[0:FOX_HEADS, :]

    @pl.when(i % tiles_per_batch == 0)
    def _():
        carry_ref[...] = jnp.zeros_like(carry_ref)

    row = lax.broadcasted_iota(jnp.int32, (tm, tm), 0)
    col = lax.broadcasted_iota(jnp.int32, (tm, tm), 1)
    tri = (row >= col).astype(BF16)
    p1 = logf.astype(BF16)
    r1 = logf - p1.astype(F32)
    p2 = r1.astype(BF16)
    p3 = (r1 - p2.astype(F32)).astype(BF16)
    cum = _mm(tri, p1) + _mm(tri, p2) + _mm(tri, p3) + carry_ref[0:1, :]
    carry_ref[0:1, :] = cum[tm - 1:tm, :]
    cum_ref[0] = jnp.transpose(cum * LOG2E)[0:FOX_HEADS, :]


def _proj_prompt(x2d, mod3, gpre, gq, gkv, bfg, tabs, wall, wa, wb, *, seq, tm):
    t = x2d.shape[0]
    tpb = seq // tm
    row = lambda w: pl.BlockSpec((tm, w), lambda i: (i, 0))
    pos = lambda w: pl.BlockSpec((tm, w), lambda i: (i % tpb, 0))
    nb = t // seq
    keym = lambda w: pl.BlockSpec((1, w, tm), lambda i: (i // tpb, 0, i % tpb))
    out_shapes = [
        jax.ShapeDtypeStruct((t, MLA_HEADS * QHEAD_W), BF16),
        jax.ShapeDtypeStruct((t, QHEAD_W), BF16),
        jax.ShapeDtypeStruct((t, KV_LORA), F32),
        jax.ShapeDtypeStruct((nb, QK_ROPE, seq), F32),
        jax.ShapeDtypeStruct((t, FOX_W), BF16),
        jax.ShapeDtypeStruct((nb, FOX_W, seq), F32),
        jax.ShapeDtypeStruct((t, FOX_W), BF16),
        jax.ShapeDtypeStruct((nb, FOX_W, seq), F32),
        jax.ShapeDtypeStruct((t, FOX_W), BF16),
        jax.ShapeDtypeStruct((nb, FOX_HEADS, seq), F32),
        jax.ShapeDtypeStruct((nb, FOX_HEADS, seq), F32),
    ]
    out_specs = [row(MLA_HEADS * QHEAD_W), row(QHEAD_W), row(KV_LORA), keym(QK_ROPE),
                 row(FOX_W), keym(FOX_W), row(FOX_W), keym(FOX_W), row(FOX_W),
                 keym(FOX_HEADS), keym(FOX_HEADS)]
    return pl.pallas_call(
        functools.partial(_proj_prompt_kernel, tiles_per_batch=tpb),
        grid=(t // tm,),
        in_specs=[row(D_MODEL),
                  pl.BlockSpec((1, 1, 6 * D_MODEL), lambda i: (i // tpb, 0, 0)),
                  _full(gpre.shape), _full(gq.shape), _full(gkv.shape), _full(bfg.shape),
                  pos(QHEAD_W), pos(QHEAD_W), pos(LANES), pos(LANES),
                  _full(wall.shape), _full(wa.shape), _full(wb.shape)],
        out_specs=out_specs,
        out_shape=out_shapes,
        scratch_shapes=[pltpu.VMEM((8, LANES), F32)],
        compiler_params=_cparams(("arbitrary",), VMEM_LIMIT),
        name="proj_prompt",
    )(x2d, mod3, gpre, gq, gkv, bfg, *tabs, wall, wa, wb)


def _proj_sample_kernel(x_ref, mod_ref, gpre_ref, gq_ref, gkv_ref, bfg_ref,
                        ct_ref, st_ref, kct_ref, kst_ref,
                        wall_hi, wall_lo, wa_hi, wa_lo, wb_hi, wb_lo,
                        q_ref, ckv_ref, krp_ref, fq_ref, fk_ref, fv_ref, lf_ref):
    q, c_kv, krp, fq, fk, fv, logf = _proj_common(
        x_ref[...], mod_ref[...], gpre_ref[...], gq_ref[...], gkv_ref[...], bfg_ref[...],
        (ct_ref[...], st_ref[...], kct_ref[...], kst_ref[...]),
        lambda h: _mm3(h, wall_hi[...], wall_lo[...]),
        lambda c: _mm3(c, wa_hi[...], wa_lo[...]),
        lambda c: _mm3(c, wb_hi[...], wb_lo[...]))
    q_ref[...] = q
    ckv_ref[...] = c_kv
    krp_ref[...] = krp
    fq_ref[...] = fq
    fk_ref[...] = fk
    fv_ref[...] = fv
    lf_ref[...] = logf


def _proj_sample(x2d, mod_tok, gpre, gq, gkv, bfg, tabs, wall, wa, wb):
    t = x2d.shape[0]
    ins = [x2d, mod_tok, gpre, gq, gkv, bfg, *tabs, *wall, *wa, *wb]
    shapes = [(t, MLA_HEADS * QHEAD_W), (t, KV_LORA), (t, LANES), (t, FOX_W), (t, FOX_W),
              (t, FOX_W), (t, LANES)]
    return pl.pallas_call(
        _proj_sample_kernel,
        grid=(1,),
        in_specs=[_full(a.shape) for a in ins],
        out_specs=[_full(s) for s in shapes],
        out_shape=[jax.ShapeDtypeStruct(s, F32) for s in shapes],
        compiler_params=_cparams(("arbitrary",), VMEM_LIMIT),
        name="proj_sample",
    )(*ins)


def _flash_update(s, v1, m_ref, acc_ref):
    m_prev = m_ref[...]
    m_new = jnp.maximum(m_prev, jnp.max(s, axis=1, keepdims=True))
    alpha = jnp.exp2(m_prev - m_new)
    p = jnp.exp2(s - jnp.concatenate([m_new] * (s.shape[1] // LANES), axis=1))
    acc_ref[...] = (jnp.concatenate([alpha, alpha], axis=1) * acc_ref[...]
                    + _mm(p.astype(BF16), v1))
    m_ref[...] = m_new


def _with_ones(v):
    return jnp.concatenate([v, jnp.ones(v.shape, v.dtype)], axis=1)


def _causal_bounds(i, tq, tk):
    n_full = (i * tq + 1) // tk
    n_all = (i * tq + tq + tk - 1) // tk
    return n_full, n_all


def _mla_prompt_kernel(q_ref, k_ref, o_ref, m_ref, acc_ref, *, tq, tk, chunk):
    i = pl.program_id(1)
    nh = MLA_HEADS
    qs = jnp.concatenate([q_ref[:, h * QHEAD_W:(h + 1) * QHEAD_W] for h in range(nh)], axis=0)
    m_ref[...] = jnp.full(m_ref.shape, NEG, F32)
    acc_ref[...] = jnp.zeros(acc_ref.shape, F32)

    def step(j, masked):
        kb = k_ref[pl.ds(pl.multiple_of(j * tk, tk), tk), :]
        vb = _with_ones(kb[:, 0:KV_LORA])
        s_all = _mm_nt(qs, kb)
        for c in range(nh * tq // chunk):
            rows = slice(c * chunk, (c + 1) * chunk)
            s = s_all[rows]
            if masked:
                row = lax.broadcasted_iota(jnp.int32, s.shape, 0)
                col = lax.broadcasted_iota(jnp.int32, s.shape, 1)
                s = jnp.where(j * tk + col <= i * tq + ((c * chunk + row) & (tq - 1)), s, NEG)
            _flash_update(s, vb, m_ref.at[rows], acc_ref.at[rows])

    n_full, n_all = _causal_bounds(i, tq, tk)
    lax.fori_loop(0, n_full, lambda j, c: (step(j, False), c)[1], 0)
    lax.fori_loop(n_full, n_all, lambda j, c: (step(j, True), c)[1], 0)
    o = acc_ref[:, 0:KV_LORA] / acc_ref[:, KV_LORA:2 * KV_LORA]
    o_ref[...] = jnp.concatenate([o[h * tq:(h + 1) * tq] for h in range(nh)], axis=1).astype(BF16)


def _mla_prompt(q, kmla, *, batch, seq, tq, tk):
    t = q.shape[0]
    nq = seq // tq
    rows = MLA_HEADS * tq
    return pl.pallas_call(
        functools.partial(_mla_prompt_kernel, tq=tq, tk=tk, chunk=128),
        grid=(batch, nq),
        in_specs=[pl.BlockSpec((tq, MLA_HEADS * QHEAD_W), lambda b, i: (b * nq + i, 0)),
                  pl.BlockSpec((seq, QHEAD_W), lambda b, i: (b, 0))],
        out_specs=pl.BlockSpec((tq, MLA_HEADS * KV_LORA), lambda b, i: (b * nq + i, 0)),
        out_shape=jax.ShapeDtypeStruct((t, MLA_HEADS * KV_LORA), BF16),
        scratch_shapes=[pltpu.VMEM((rows, LANES), F32), pltpu.VMEM((rows, 2 * KV_LORA), F32)],
        compiler_params=_cparams(("parallel", "parallel"), VMEM_LIMIT),
        name="mla_prompt",
    )(q, kmla)


def _fox_prompt_kernel(q_ref, k_ref, v_ref, ck_ref, o_ref, m_ref, acc_ref, *, tq, tk, chunk):
    i = pl.program_id(1)
    lane = lax.broadcasted_iota(jnp.int32, (tq, LANES), 1)
    n_full, n_all = _causal_bounds(i, tq, tk)
    lo_half = lane < FOX_HEAD_DIM
    m_ref[...] = jnp.full(m_ref.shape, NEG, F32)
    acc_ref[...] = jnp.zeros(acc_ref.shape, F32)

    def step(j, masked):
        keys = pl.ds(pl.multiple_of(j * tk, tk), tk)

        def scores(h):
            cols = slice((h // 2) * LANES, (h // 2 + 1) * LANES)
            qp = q_ref[:, cols]
            sel = lo_half if h % 2 == 0 else jnp.logical_not(lo_half)
            qm = jnp.where(sel, qp, jnp.zeros_like(qp))
            return _mm_nt(qm, k_ref[keys, cols])

        def values(h, s_all):
            cols = slice((h // 2) * LANES, (h // 2 + 1) * LANES)
            vb = _with_ones(v_ref[keys, cols])
            ck = ck_ref[0, h, pl.ds(j, 1), :]
            for c in range(tq // chunk):
                rows = slice(c * chunk, (c + 1) * chunk)
                s = s_all[rows] - ck
                if masked:
                    row = lax.broadcasted_iota(jnp.int32, s.shape, 0)
                    col = lax.broadcasted_iota(jnp.int32, s.shape, 1)
                    s = jnp.where(j * tk + col <= i * tq + c * chunk + row, s, NEG)
                _flash_update(s, vb, m_ref.at[h, rows], acc_ref.at[h, rows])

        s_next = scores(0)
        for h in range(FOX_HEADS):
            s_cur = s_next
            if h + 1 < FOX_HEADS:
                s_next = scores(h + 1)
            values(h, s_cur)

    lax.fori_loop(0, n_full, lambda j, c: (step(j, False), c)[1], 0)
    lax.fori_loop(n_full, n_all, lambda j, c: (step(j, True), c)[1], 0)
    for pair in range(FOX_HEADS // 2):
        o0 = acc_ref[2 * pair, :, 0:LANES] / acc_ref[2 * pair, :, LANES:2 * LANES]
        o1 = acc_ref[2 * pair + 1, :, 0:LANES] / acc_ref[2 * pair + 1, :, LANES:2 * LANES]
        o_ref[:, pair * LANES:(pair + 1) * LANES] = jnp.where(lo_half, o0, o1).astype(BF16)


def _fox_prompt(fq, fk, fv, ck4, *, batch, seq, tq, tk):
    t = fq.shape[0]
    nq = seq // tq
    nk = seq // tk
    return pl.pallas_call(
        functools.partial(_fox_prompt_kernel, tq=tq, tk=tk, chunk=128),
        grid=(batch, nq),
        in_specs=[pl.BlockSpec((tq, FOX_W), lambda b, i: (b * nq + i, 0)),
                  pl.BlockSpec((seq, FOX_W), lambda b, i: (b, 0)),
                  pl.BlockSpec((seq, FOX_W), lambda b, i: (b, 0)),
                  pl.BlockSpec((1, FOX_HEADS, nk, tk), lambda b, i: (b, 0, 0, 0))],
        out_specs=pl.BlockSpec((tq, FOX_W), lambda b, i: (b * nq + i, 0)),
        out_shape=jax.ShapeDtypeStruct((t, FOX_W), BF16),
        scratch_shapes=[pltpu.VMEM((FOX_HEADS, tq, LANES), F32),
                        pltpu.VMEM((FOX_HEADS, tq, 2 * LANES), F32)],
        compiler_params=_cparams(("parallel", "parallel"), VMEM_LIMIT),
        name="fox_prompt",
    )(fq, fk, fv, ck4)


def _route(r):
    lane = lax.broadcasted_iota(jnp.int32, r.shape, 1)
    lanef = lane.astype(F32)
    big = jnp.float32(1e9)
    gmask = lane < N_GROUPS
    gl = jnp.where(gmask, r, -jnp.inf)
    gm = jnp.max(gl, axis=1, keepdims=True)
    gs = jnp.sum(jnp.where(gmask, jnp.exp(r - gm), 0.0), axis=1, keepdims=True)
    g_w = 1.0 / gs
    g_idx = jnp.min(jnp.where(gl == gm, lanef, big), axis=1, keepdims=True)
    egrp = ((lane - N_GROUPS) >> 3).astype(F32)
    emask = (lane >= N_GROUPS) & (lane < N_GROUPS + N_EXPERTS) & (egrp == g_idx)
    el = jnp.where(emask, r, -jnp.inf)
    em = jnp.max(el, axis=1, keepdims=True)
    ee = jnp.where(emask, jnp.exp(r - em), 0.0)
    p = ee / jnp.sum(ee, axis=1, keepdims=True)
    pm = jnp.where(emask, p, -1.0)
    p1 = jnp.max(pm, axis=1, keepdims=True)
    i1 = jnp.min(jnp.where(pm == p1, lanef, big), axis=1, keepdims=True)
    pm2 = jnp.where(lanef == i1, -1.0, pm)
    p2 = jnp.max(pm2, axis=1, keepdims=True)
    i2 = jnp.min(jnp.where(pm2 == p2, lanef, big), axis=1, keepdims=True)
    den = p1 + p2
    c1 = g_w * (p1 / den)
    c2 = g_w * (p2 / den)
    out = jnp.where(lane == 0, i1 - N_GROUPS,
                    jnp.where(lane == 1, i2 - N_GROUPS,
                              jnp.where(lane == 2, c1, jnp.where(lane == 3, c2, 0.0))))
    return out


def _post_common(x, mod, mixed, gpost, gffn, wr_hi, wr_lo, br):
    gate_m = mod[:, 2 * D_MODEL:3 * D_MODEL]
    shift_f = mod[:, 3 * D_MODEL:4 * D_MODEL]
    scale_f = mod[:, 4 * D_MODEL:5 * D_MODEL]
    x1 = x + gate_m * _rms(mixed, gpost)
    h2 = _rms(x1, gffn) * (1.0 + scale_f) + shift_f
    r = _mm3(h2, wr_hi, wr_lo) + br
    return x1, h2, _route(r)


def _post_prompt_kernel(x_ref, mod_ref, ol_ref, of_ref, wov_ref, wof_ref, gpost_ref, gffn_ref,
                        wrh_ref, wrl_ref, br_ref, x1_ref, h2_ref, rt_ref, cnt_ref, carry_ref):
    i = pl.program_id(0)
    tm = x_ref.shape[0]
    mixed = _mm(ol_ref[...], wov_ref[...]) + _mm(of_ref[...], wof_ref[...])
    x1, h2, rt = _post_common(x_ref[...], mod_ref[0], mixed, gpost_ref[...], gffn_ref[...],
                              wrh_ref[...], wrl_ref[...], br_ref[...])
    x1_ref[...] = x1
    h2_ref[...] = h2

    @pl.when(i == 0)
    def _():
        carry_ref[...] = jnp.zeros_like(carry_ref)

    lane = lax.broadcasted_iota(jnp.int32, rt.shape, 1)
    lanef = lane.astype(F32)
    oh1 = (lanef == rt[:, 0:1]).astype(F32)
    oh2 = (lanef == rt[:, 1:2]).astype(F32)
    row = lax.broadcasted_iota(jnp.int32, (tm, tm), 0)
    col = lax.broadcasted_iota(jnp.int32, (tm, tm), 1)
    before = _mm((row > col).astype(BF16), (oh1 + oh2).astype(BF16)) + carry_ref[0:1, :]
    rank1 = jnp.sum(oh1 * before, axis=1, keepdims=True)
    rank2 = jnp.sum(oh2 * (before + oh1), axis=1, keepdims=True)
    total = before[tm - 1:tm, :] + oh1[tm - 1:tm, :] + oh2[tm - 1:tm, :]
    carry_ref[0:1, :] = total
    cnt_ref[...] = jnp.broadcast_to(total, cnt_ref.shape)
    rt_ref[...] = jnp.where(lane == 4, rank1, jnp.where(lane == 5, rank2, rt))


def _post_prompt(x2d, mod3, o_lat, o_fox, wov, wof, gpost, gffn, wr, br, *, seq, tm):
    t = x2d.shape[0]
    tpb = seq // tm
    row = lambda w: pl.BlockSpec((tm, w), lambda i: (i, 0))
    return pl.pallas_call(
        _post_prompt_kernel,
        grid=(t // tm,),
        in_specs=[row(D_MODEL),
                  pl.BlockSpec((1, 1, 6 * D_MODEL), lambda i: (i // tpb, 0, 0)),
                  row(MLA_HEADS * KV_LORA), row(FOX_W),
                  _full(wov.shape), _full(wof.shape), _full(gpost.shape), _full(gffn.shape),
                  _full(wr[0].shape), _full(wr[1].shape), _full(br.shape)],
        out_specs=[row(D_MODEL), row(D_MODEL), row(LANES), _full((8, LANES))],
        out_shape=[jax.ShapeDtypeStruct((t, D_MODEL), F32),
                   jax.ShapeDtypeStruct((t, D_MODEL), F32),
                   jax.ShapeDtypeStruct((t, LANES), F32),
                   jax.ShapeDtypeStruct((8, LANES), F32)],
        scratch_shapes=[pltpu.VMEM((8, LANES), F32)],
        compiler_params=_cparams(("arbitrary",), VMEM_LIMIT),
        name="post_prompt",
    )(x2d, mod3, o_lat, o_fox, wov, wof, gpost, gffn, wr[0], wr[1], br)


def _post_sample_kernel(x_ref, mod_ref, ol_ref, of_ref, wovh_ref, wovl_ref, wofh_ref, wofl_ref,
                        gpost_ref, gffn_ref, wrh_ref, wrl_ref, br_ref, x1_ref, h2_ref, rt_ref):
    mixed = (_mm3(ol_ref[...], wovh_ref[...], wovl_ref[...])
             + _mm3(of_ref[...], wofh_ref[...], wofl_ref[...]))
    x1, h2, rt = _post_common(x_ref[...], mod_ref[...], mixed, gpost_ref[...], gffn_ref[...],
                              wrh_ref[...], wrl_ref[...], br_ref[...])
    x1_ref[...] = x1
    h2_ref[...] = h2
    rt_ref[...] = rt


def _post_sample(x2d, mod_tok, o_lat, o_fox, wov, wof, gpost, gffn, wr, br):
    t = x2d.shape[0]
    ins = [x2d, mod_tok, o_lat, o_fox, *wov, *wof, gpost, gffn, *wr, br]
    shapes = [(t, D_MODEL), (t, D_MODEL), (t, LANES)]
    return pl.pallas_call(
        _post_sample_kernel,
        grid=(1,),
        in_specs=[_full(a.shape) for a in ins],
        out_specs=[_full(s) for s in shapes],
        out_shape=[jax.ShapeDtypeStruct(s, F32) for s in shapes],
        compiler_params=_cparams(("arbitrary",), VMEM_LIMIT),
        name="post_sample",
    )(*ins)


def _dispatch_kernel(pos_ref, h2_ref, zero_ref, xs_ref, sem):
    del zero_ref
    i = pl.program_id(0)
    tm = h2_ref.shape[0]
    base = i * (2 * tm)

    def body(t, c):
        for k in range(2):
            p = pos_ref[base + 2 * t + k]
            pltpu.make_async_copy(h2_ref.at[pl.ds(t, 1)], xs_ref.at[pl.ds(p, 1)], sem).start()
        return c

    lax.fori_loop(0, tm, body, 0, unroll=8)
    for _ in range(2):
        pltpu.make_async_copy(h2_ref, xs_ref.at[pl.ds(0, tm)], sem).wait()


def _dispatch(pos, h2, n_rows, *, tm):
    t = h2.shape[0]
    grid_spec = pltpu.PrefetchScalarGridSpec(
        num_scalar_prefetch=1,
        grid=(t // tm,),
        in_specs=[pl.BlockSpec((tm, D_MODEL), lambda i, pos: (i, 0)),
                  pl.BlockSpec(memory_space=pl.ANY)],
        out_specs=pl.BlockSpec(memory_space=pl.ANY),
        scratch_shapes=[pltpu.SemaphoreType.DMA],
    )
    return pl.pallas_call(
        _dispatch_kernel,
        grid_spec=grid_spec,
        out_shape=jax.ShapeDtypeStruct((n_rows, D_MODEL), F32),
        input_output_aliases={2: 0},
        compiler_params=_cparams(("arbitrary",), VMEM_LIMIT),
        name="moe_dispatch",
    )(pos, h2, jnp.zeros((n_rows, D_MODEL), F32))


def _gmm_kernel(te_ref, nu_ref, x_ref, wg_ref, wu_ref, wd_ref, o_ref):
    i = pl.program_id(0)

    @pl.when(i < nu_ref[0])
    def _():
        x = x_ref[...].astype(BF16)
        a = _mm(x, wg_ref[0].astype(BF16))
        u = _mm(x, wu_ref[0].astype(BF16))
        act = (_silu(a) * u).astype(BF16)
        o_ref[...] = _mm(act, wd_ref[0].astype(BF16))

    @pl.when(i >= nu_ref[0])
    def _():
        o_ref[...] = jnp.zeros_like(o_ref)


def _gmm(tile_expert, n_used, xs, wg, wu, wd, *, tm):
    nr = xs.shape[0]
    grid_spec = pltpu.PrefetchScalarGridSpec(
        num_scalar_prefetch=2,
        grid=(nr // tm,),
        in_specs=[pl.BlockSpec((tm, D_MODEL), lambda i, te, nu: (i, 0)),
                  pl.BlockSpec((1, D_MODEL, D_EXPERT), lambda i, te, nu: (te[i], 0, 0)),
                  pl.BlockSpec((1, D_MODEL, D_EXPERT), lambda i, te, nu: (te[i], 0, 0)),
                  pl.BlockSpec((1, D_EXPERT, D_MODEL), lambda i, te, nu: (te[i], 0, 0))],
        out_specs=pl.BlockSpec((tm, D_MODEL), lambda i, te, nu: (i, 0)),
    )
    return pl.pallas_call(
        _gmm_kernel,
        grid_spec=grid_spec,
        out_shape=jax.ShapeDtypeStruct((nr, D_MODEL), F32),
        compiler_params=_cparams(("arbitrary",), VMEM_LIMIT),
        name="moe_gmm",
    )(tile_expert, n_used, xs, wg, wu, wd)


def _combine_kernel(pos_ref, x1_ref, mod_ref, rt_ref, g_ref, ys_ref, o_ref, buf_ref, sem):
    i = pl.program_id(0)
    n = pl.num_programs(0)
    tm = x1_ref.shape[0]

    def issue(tile, slot):
        base = tile * (2 * tm)

        def body(t, c):
            for k in range(2):
                p = pos_ref[base + 2 * t + k]
                pltpu.make_async_copy(ys_ref.at[pl.ds(p, 1)], buf_ref.at[slot, k, pl.ds(t, 1)],
                                      sem.at[slot]).start()
            return c

        lax.fori_loop(0, tm, body, 0, unroll=8)

    @pl.when(i == 0)
    def _():
        issue(0, 0)

    @pl.when(i + 1 < n)
    def _():
        issue(i + 1, (i + 1) % 2)

    slot = i % 2
    for k in range(2):
        pltpu.make_async_copy(ys_ref.at[pl.ds(0, tm)], buf_ref.at[slot, k], sem.at[slot]).wait()
    gate_f = mod_ref[0][:, 5 * D_MODEL:6 * D_MODEL]
    rt = rt_ref[...]
    y = rt[:, 2:3] * buf_ref[slot, 0] + rt[:, 3:4] * buf_ref[slot, 1]
    o_ref[...] = x1_ref[...] + gate_f * _rms(y, g_ref[...])


def _combine(pos, x1, mod3, route, gpost, ys, *, seq, tm):
    t = x1.shape[0]
    tpb = seq // tm
    row = lambda w: pl.BlockSpec((tm, w), lambda i, pos: (i, 0))
    grid_spec = pltpu.PrefetchScalarGridSpec(
        num_scalar_prefetch=1,
        grid=(t // tm,),
        in_specs=[row(D_MODEL),
                  pl.BlockSpec((1, 1, 6 * D_MODEL), lambda i, pos: (i // tpb, 0, 0)),
                  row(LANES),
                  pl.BlockSpec(gpost.shape, lambda i, pos: (0, 0)),
                  pl.BlockSpec(memory_space=pl.ANY)],
        out_specs=row(D_MODEL),
        scratch_shapes=[pltpu.VMEM((2, 2, tm, D_MODEL), F32), pltpu.SemaphoreType.DMA((2,))],
    )
    return pl.pallas_call(
        _combine_kernel,
        grid_spec=grid_spec,
        out_shape=jax.ShapeDtypeStruct((t, D_MODEL), F32),
        compiler_params=_cparams(("arbitrary",), VMEM_LIMIT),
        name="moe_combine",
    )(pos, x1, mod3, route, gpost, ys)


def _moe_sample_kernel(h2_ref, rt_ref, x1_ref, mod_ref, g_ref, wg_ref, wu_ref, wd_ref,
                       o_ref, acc_ref):
    e = pl.program_id(0)

    @pl.when(e == 0)
    def _():
        acc_ref[...] = jnp.zeros_like(acc_ref)

    h2 = h2_ref[...]
    rt = rt_ref[...]
    ef = e.astype(F32)
    cw = (jnp.where(rt[:, 0:1] == ef, rt[:, 2:3], 0.0)
          + jnp.where(rt[:, 1:2] == ef, rt[:, 3:4], 0.0))
    a = _mm3(h2, *_split(wg_ref[0]))
    u = _mm3(h2, *_split(wu_ref[0]))
    act = _silu(a) * u * cw
    acc_ref[...] += _mm3(act, *_split(wd_ref[0]))

    @pl.when(e == pl.num_programs(0) - 1)
    def _():
        gate_f = mod_ref[:, 5 * D_MODEL:6 * D_MODEL]
        o_ref[...] = x1_ref[...] + gate_f * _rms(acc_ref[...], g_ref[...])


def _moe_sample(h2, route, x1, mod_tok, gpost, wg, wu, wd):
    t = h2.shape[0]
    return pl.pallas_call(
        _moe_sample_kernel,
        grid=(N_EXPERTS,),
        in_specs=[_full(h2.shape), _full(route.shape), _full(x1.shape), _full(mod_tok.shape),
                  _full(gpost.shape),
                  pl.BlockSpec((1, D_MODEL, D_EXPERT), lambda e: (e, 0, 0)),
                  pl.BlockSpec((1, D_MODEL, D_EXPERT), lambda e: (e, 0, 0)),
                  pl.BlockSpec((1, D_EXPERT, D_MODEL), lambda e: (e, 0, 0))],
        out_specs=_full((t, D_MODEL)),
        out_shape=jax.ShapeDtypeStruct((t, D_MODEL), F32),
        scratch_shapes=[pltpu.VMEM((t, D_MODEL), F32)],
        compiler_params=_cparams(("arbitrary",), VMEM_LIMIT),
        name="moe_sample",
    )(h2, route, x1, mod_tok, gpost, wg, wu, wd)


def _rev_cumsum_lanes(x):
    lane = lax.broadcasted_iota(jnp.int32, x.shape, 1)
    sh = 1
    while sh < LANES:
        x = x + jnp.where(lane + sh < LANES, pltpu.roll(x, LANES - sh, 1), 0.0)
        sh *= 2
    return x


def _fwd_cumsum_lanes(x):
    lane = lax.broadcasted_iota(jnp.int32, x.shape, 1)
    sh = 1
    while sh < LANES:
        x = x + jnp.where(lane >= sh, pltpu.roll(x, sh, 1), 0.0)
        sh *= 2
    return x


def _seg_update(s, m_ref, l_ref):
    m_prev = m_ref[...]
    m_new = jnp.maximum(m_prev, jnp.max(s, axis=1, keepdims=True))
    alpha = jnp.exp2(m_prev - m_new)
    p = jnp.exp2(s - m_new)
    l_ref[...] = alpha * l_ref[...] + jnp.sum(p, axis=1, keepdims=True)
    m_ref[...] = m_new
    return p, alpha


def _col_to_lanes(col, fill=0.0):
    n = col.shape[0]
    row = lax.broadcasted_iota(jnp.int32, (n, LANES), 0)
    lane = lax.broadcasted_iota(jnp.int32, (n, LANES), 1)
    vec = jnp.sum(jnp.where(row == lane, col, 0.0), axis=0, keepdims=True)
    return jnp.where(lane[0:1, :] < n, vec, fill)


def _sample_attn_kernel(pt_ref, qa_ref, qr_ref, qf_ref, latn_ref, krn_ref, fkn_ref, fvn_ref,
                        lfn_ref, clat_ref, ckr_ref, cfk_ref, cfv_ref, clf_ref, om_ref, of_ref,
                        mm_ref, lm_ref, accm_ref, mf_ref, lfs_ref, accf_ref, carry_ref,
                        lat_buf, kr_buf, fk_buf, fv_buf, lf_buf, sem_a, sem_b, *, pages, n_chunks):
    npg = pages
    b = pl.program_id(0)
    c = pl.program_id(1)
    nrow = qa_ref.shape[1]
    n_steps = pl.num_programs(0) * n_chunks
    step = b * n_chunks + c
    slot = step % 2

    def start(chunk_step, dst_slot, group_a):
        cb = chunk_step // n_chunks
        cc = chunk_step % n_chunks
        for p in range(npg):
            page = pt_ref[cb, (n_chunks - 1 - cc) * npg + p]
            keys = pl.ds(p * LANES, LANES)
            if group_a:
                sem = sem_a.at[dst_slot]
                pltpu.make_async_copy(clat_ref.at[page, 0], lat_buf.at[dst_slot, keys, :],
                                      sem).start()
                pltpu.make_async_copy(ckr_ref.at[page, 0], kr_buf.at[dst_slot, :, keys],
                                      sem).start()
                pltpu.make_async_copy(cfk_ref.at[page, 0], fk_buf.at[dst_slot, :, :, keys],
                                      sem).start()
                pltpu.make_async_copy(clf_ref.at[page, 0], lf_buf.at[dst_slot, :, keys],
                                      sem).start()
            else:
                pltpu.make_async_copy(cfv_ref.at[page, 0], fv_buf.at[dst_slot, :, :, keys],
                                      sem_b.at[dst_slot]).start()

    def wait(dst_slot, group_a):
        bufs = (lat_buf, kr_buf, fk_buf, lf_buf) if group_a else (fv_buf,)
        sem = sem_a.at[dst_slot] if group_a else sem_b.at[dst_slot]
        for buf in bufs:
            pltpu.make_async_copy(buf.at[dst_slot], buf.at[dst_slot], sem).wait()

    nxt = jnp.minimum(step + 1, n_steps - 1)

    @pl.when(step == 0)
    def _():
        start(step, slot, True)
        start(step, slot, False)

    wait(slot, True)

    @pl.when(c == 0)
    def _():
        mm_ref[...] = jnp.full(mm_ref.shape, NEG, F32)
        mf_ref[...] = jnp.full(mf_ref.shape, NEG, F32)
        lm_ref[...] = jnp.zeros(lm_ref.shape, F32)
        lfs_ref[...] = jnp.zeros(lfs_ref.shape, F32)
        accm_ref[...] = jnp.zeros(accm_ref.shape, F32)
        accf_ref[...] = jnp.zeros(accf_ref.shape, F32)
        carry_ref[...] = jnp.zeros(carry_ref.shape, F32)

    def stack(x):
        hi, lo = _split(x)
        return hi, jnp.concatenate([hi, lo], axis=0)

    qa_hi, qa_cat = stack(qa_ref[0])
    qr_hi, qr_cat = stack(qr_ref[0][:, 0:QK_ROPE])
    qf_hi, qf_cat = stack(qf_ref[0])

    def attend(lat, krt, fkt, get_fvt, bias_f, bias_m, ngrp, hooks=()):
        width = lat.shape[0] // ngrp
        groups = []
        for g in range(ngrp):
            ks = slice(g * width, (g + 1) * width)
            lh, ll = _split(lat[ks, :])
            kh, kl = _split(krt[:, ks])
            fh, fl = _split(fkt[:, ks])
            top = _mm_nt(qa_cat, lh) + _mm(qr_cat, kh)
            s_m = top[:nrow] + top[nrow:] + _mm_nt(qa_hi, ll) + _mm(qr_hi, kl)
            top = _mm(qf_cat, fh)
            s_f = (top[:nrow] + top[nrow:] + _mm(qf_hi, fl)
                   + bias_f[:, g * width:(g + 1) * width])
            if bias_m is not None:
                s_m = s_m + bias_m
            groups.append((s_m, s_f, lh, ll))
            if g == 0 and hooks:
                hooks[0]()
        if hooks:
            hooks[1]()
        fvt = get_fvt()
        for g in range(ngrp):
            s_m, s_f, lh, ll = groups[g]
            ks = slice(g * width, (g + 1) * width)
            p_m, a_m = _seg_update(s_m, mm_ref, lm_ref)
            p_f, a_f = _seg_update(s_f, mf_ref, lfs_ref)
            pm_hi, pm_lo = _split(p_m)
            pf_hi, pf_lo = _split(p_f)
            zrow = jnp.zeros(pm_hi.shape, BF16)
            pm_cat = jnp.concatenate([pm_hi, pm_lo], axis=0)
            pm_hi0 = jnp.concatenate([pm_hi, zrow], axis=0)
            pf_cat = jnp.concatenate([pf_hi, pf_lo, zrow, zrow], axis=0)
            pf_hi0 = jnp.concatenate([pf_hi, zrow, zrow, zrow], axis=0)
            vh, vl = _split(fvt[:, ks])
            upd_m = _mm(pm_cat, lh) + _mm(pm_hi0, ll)
            upd_f = _mm_nt(vh, pf_cat) + _mm_nt(vl, pf_hi0)
            accm_ref[...] = jnp.concatenate([a_m, a_m], axis=0) * accm_ref[...] + upd_m
            a_lanes = _col_to_lanes(a_f)
            accf_ref[...] = (a_lanes + pltpu.roll(a_lanes, nrow, 1)) * accf_ref[...] + upd_f
            if g == 0 and hooks:
                hooks[2]()

    sufs = [None] * npg
    carry = carry_ref[...]
    for p in range(npg - 1, -1, -1):
        lf = lf_buf[slot, :, p * LANES:(p + 1) * LANES]
        inc = _rev_cumsum_lanes(lf)
        sufs[p] = (inc - lf + carry) * LOG2E
        carry = carry + inc[:, 0:1]
    carry_ref[...] = jnp.broadcast_to(carry[:, 0:1], carry_ref.shape)
    suf = jnp.concatenate(sufs, axis=1)
    bias_f = jnp.concatenate([suf] * (nrow // FOX_HEADS), axis=0)
    n_keys = npg * LANES

    hooks = (lambda: start(nxt, 1 - slot, True),
             lambda: wait(slot, False),
             lambda: start(nxt, 1 - slot, False))
    attend(lat_buf[slot], kr_buf[slot], fk_buf[slot].reshape(FOX_W, n_keys),
           lambda: fv_buf[slot].reshape(FOX_W, n_keys), bias_f, None, 2, hooks)

    @pl.when(step == n_steps - 1)
    def _():
        wait(1 - slot, True)
        wait(1 - slot, False)

    @pl.when(c == n_chunks - 1)
    def _():
        row = lax.broadcasted_iota(jnp.int32, (nrow, LANES), 0)
        lane = lax.broadcasted_iota(jnp.int32, (nrow, LANES), 1)
        ok = lane <= (row >> 3)
        cum_new = _fwd_cumsum_lanes(lfn_ref[0]) * LOG2E
        b_f = jnp.where(ok, -jnp.concatenate([cum_new] * (nrow // FOX_HEADS), axis=0), NEG)
        b_m = jnp.where(ok, 0.0, NEG)
        attend(latn_ref[0], krn_ref[0], fkn_ref[0], lambda: fvn_ref[0], b_f, b_m, 1)
        accm = accm_ref[...]
        om_ref[0] = (accm[:nrow] + accm[nrow:]) / lm_ref[...]
        accf = accf_ref[...]
        of_ref[0] = (accf + pltpu.roll(accf, LANES - nrow, 1)) / _col_to_lanes(lfs_ref[...], 1.0)


def _sample_attn(page_table, qa, qr, qf, latn, krn, fkn, fvn, lfn,
                 c_lat, c_kr, c_fk, c_fv, c_lf, *, pages):
    nb, n_pages = page_table.shape
    n_chunks = n_pages // pages
    nrow = qa.shape[1]

    def per_b(shape):
        nd = len(shape)
        return pl.BlockSpec((1,) + shape[1:], lambda b, c, pt: (b,) + (0,) * (nd - 1))

    small = [qa, qr, qf, latn, krn, fkn, fvn, lfn]
    caches = [c_lat, c_kr, c_fk, c_fv, c_lf]
    in_specs = [per_b(a.shape) for a in small] + [pl.BlockSpec(memory_space=pl.ANY)] * len(caches)
    operands = small + caches
    n_keys = pages * c_lat.shape[2]
    grid_spec = pltpu.PrefetchScalarGridSpec(
        num_scalar_prefetch=1,
        grid=(nb, n_chunks),
        in_specs=in_specs,
        out_specs=[pl.BlockSpec((1, nrow, KV_LORA), lambda b, c, pt: (b, 0, 0)),
                   pl.BlockSpec((1, FOX_W, LANES), lambda b, c, pt: (b, 0, 0))],
        scratch_shapes=[pltpu.VMEM((nrow, 1), F32), pltpu.VMEM((nrow, 1), F32),
                        pltpu.VMEM((2 * nrow, KV_LORA), F32),
                        pltpu.VMEM((nrow, 1), F32), pltpu.VMEM((nrow, 1), F32),
                        pltpu.VMEM((FOX_W, LANES), F32),
                        pltpu.VMEM((FOX_HEADS, LANES), F32),
                        pltpu.VMEM((2, n_keys, KV_LORA), F32),
                        pltpu.VMEM((2, QK_ROPE, n_keys), F32),
                        pltpu.VMEM((2, FOX_HEADS, FOX_HEAD_DIM, n_keys), F32),
                        pltpu.VMEM((2, FOX_HEADS, FOX_HEAD_DIM, n_keys), F32),
                        pltpu.VMEM((2, FOX_HEADS, n_keys), F32),
                        pltpu.SemaphoreType.DMA((2,)), pltpu.SemaphoreType.DMA((2,))],
    )
    return pl.pallas_call(
        functools.partial(_sample_attn_kernel, pages=pages, n_chunks=n_chunks),
        grid_spec=grid_spec,
        out_shape=[jax.ShapeDtypeStruct((nb, nrow, KV_LORA), F32),
                   jax.ShapeDtypeStruct((nb, FOX_W, LANES), F32)],
        compiler_params=_cparams(("arbitrary", "arbitrary"), VMEM_LIMIT),
        name="sample_attn",
    )(page_table, *operands)


def _hl(w):
    bits = lax.bitcast_convert_type(w, jnp.uint32) & jnp.uint32(0xFFFF0000)
    hi = lax.bitcast_convert_type(bits, F32)
    return hi.astype(BF16), (w - hi).astype(BF16)


def _rope_tables(pos):
    half = QK_ROPE // 2
    inv = ROPE_THETA ** (-jnp.arange(half, dtype=F32) / half)
    ang = pos.astype(F32)[:, None] * inv[None, :]
    cc = jnp.concatenate([jnp.cos(ang), jnp.cos(ang)], axis=1)
    ss = jnp.concatenate([jnp.sin(ang), jnp.sin(ang)], axis=1)
    n = pos.shape[0]
    qs = MLA_SCALE * LOG2E
    ctab = jnp.concatenate([jnp.full((n, KV_LORA), qs, F32), cc * qs,
                            jnp.zeros((n, QHEAD_W - KV_LORA - QK_ROPE), F32)], axis=1)
    stab = jnp.concatenate([jnp.zeros((n, KV_LORA), F32), ss * qs,
                            jnp.zeros((n, QHEAD_W - KV_LORA - QK_ROPE), F32)], axis=1)
    zpad = jnp.zeros((n, LANES - QK_ROPE), F32)
    return ctab, stab, jnp.concatenate([cc, zpad], axis=1), jnp.concatenate([ss, zpad], axis=1)


def _swap_halves(w):
    half = QK_ROPE // 2
    return jnp.concatenate([-w[..., half:], w[..., :half]], axis=-1)


def kernel(x_prompt, x_sample, cache_mla_latent, cache_mla_krope, cache_fox_k, cache_fox_v,
           cache_fox_logf, page_table, c_prompt, c_sample, w_ada, b_ada, g_pre_mix, g_post_mix,
           g_pre_ffn, g_post_ffn, w_in, b_forget, g_q_lora, g_kv_lora, w_uq, w_uk, w_uv, w_o,
           w_group_router, b_group_router, w_expert_router, b_expert_router, w_gate, w_up, w_down):
    assert w_ada.shape[0] == 1, "single-layer trunk"
    batch, seq, d = x_prompt.shape
    dec_batch, n_new, _ = x_sample.shape
    n_pages = page_table.shape[1]
    page = cache_mla_latent.shape[2]
    past_len = n_pages * page
    tp = batch * seq
    ts = dec_batch * n_new

    w_in0 = w_in[0]
    s0, s1, s2, s3, s4, s5 = 256, 384, 416, 928, 1440, 1952
    w_kr = w_in0[:, s1:s2]
    z = lambda n: jnp.zeros((d, n), F32)
    w_all = jnp.concatenate([
        w_in0[:, 0:s0],
        w_in0[:, s0:s1], w_kr, z(96),
        _swap_halves(w_kr), z(96),
        w_in0[:, s2:s3], w_in0[:, s3:s4], w_in0[:, s4:s5],
        w_in0[:, s5:], z(120)], axis=1)
    w_uq0 = w_uq[0]
    w_qabs = _fold(jnp.transpose(w_uq0[:, :, :QK_NOPE], (1, 0, 2)),
                   jnp.transpose(w_uk[0], (0, 2, 1)))
    w_qr = jnp.transpose(w_uq0[:, :, QK_NOPE:], (1, 0, 2))
    zq = lambda n: jnp.zeros((MLA_HEADS, Q_LORA, n), F32)
    w_a = jnp.concatenate([w_qabs, w_qr, zq(96)], axis=2)
    w_b = jnp.concatenate([zq(128), _swap_halves(w_qr), zq(96)], axis=2)
    w_a = jnp.transpose(w_a, (1, 0, 2)).reshape(Q_LORA, MLA_HEADS * QHEAD_W)
    w_b = jnp.transpose(w_b, (1, 0, 2)).reshape(Q_LORA, MLA_HEADS * QHEAD_W)
    w_o0 = w_o[0]
    w_ov = _fold(w_uv[0], w_o0[:MLA_HEADS * V_HEAD].reshape(MLA_HEADS, V_HEAD, d))
    w_ov = w_ov.reshape(MLA_HEADS * KV_LORA, d)
    w_of = w_o0[MLA_HEADS * V_HEAD:]
    w_r = jnp.concatenate([w_group_router[0], w_expert_router[0].reshape(d, N_EXPERTS),
                           jnp.zeros((d, LANES - N_GROUPS - N_EXPERTS), F32)], axis=1)
    b_r = jnp.concatenate([b_group_router[0], b_expert_router[0].reshape(N_EXPERTS),
                           jnp.zeros((LANES - N_GROUPS - N_EXPERTS,), F32)])[None, :]
    bfg = jnp.concatenate([b_forget[0], jnp.zeros((LANES - FOX_HEADS,), F32)])[None, :]
    w_all_hl, w_a_hl, w_b_hl = _hl(w_all), _hl(w_a), _hl(w_b)
    w_ov_hl, w_of_hl, w_r_hl = _hl(w_ov), _hl(w_of), _hl(w_r)
    wg = w_gate[0].reshape(N_EXPERTS, d, D_EXPERT)
    wu = w_up[0].reshape(N_EXPERTS, d, D_EXPERT)
    wd = w_down[0].reshape(N_EXPERTS, D_EXPERT, d)

    c_all = jnp.concatenate([c_prompt, c_sample], axis=0)
    c_all = jnp.pad(c_all, ((0, -c_all.shape[0] % 64), (0, 0)))
    mod = _ada(c_all, w_ada[0], b_ada)
    mod_p = mod[:batch].reshape(batch, 1, 6 * d)
    mod_s = jnp.repeat(mod[batch:batch + dec_batch], n_new, axis=0)

    tm = 512
    xp2 = x_prompt.reshape(tp, d)
    tabs_p = _rope_tables(jnp.arange(seq))
    (q_p, kmla_p, ckv_p, kr_p, fq_p, fk_p, fk16_p, fv_p, fv16_p, lf_p, cum_p) = _proj_prompt(
        xp2, mod_p, g_pre_mix, g_q_lora, g_kv_lora, bfg, tabs_p,
        w_all.astype(BF16), w_a.astype(BF16), w_b.astype(BF16), seq=seq, tm=tm)
    tk = 512
    o_lat_p = _mla_prompt(q_p, kmla_p, batch=batch, seq=seq, tq=512, tk=tk)
    ck4 = cum_p.reshape(batch, FOX_HEADS, seq // tk, tk)
    o_fox_p = _fox_prompt(fq_p, fk16_p, fv16_p, ck4, batch=batch, seq=seq, tq=512, tk=tk)
    x1_p, h2_p, rt_p, cnt_p = _post_prompt(xp2, mod_p, o_lat_p, o_fox_p,
                                           w_ov.astype(BF16), w_of.astype(BF16),
                                           g_post_mix, g_pre_ffn, w_r_hl, b_r, seq=seq, tm=tm)

    tme = 256
    n_asg = 2 * tp
    n_rows = n_asg + N_EXPERTS * tme
    cnt = cnt_p[0, :N_EXPERTS].astype(jnp.int32)
    pend = jnp.cumsum(((cnt + tme - 1) // tme) * tme)
    offs = pend - ((cnt + tme - 1) // tme) * tme
    tile_start = jnp.arange(n_rows // tme, dtype=jnp.int32) * tme
    tile_e = jnp.sum(tile_start[:, None] >= pend[None, :], axis=1, dtype=jnp.int32)
    tile_e = jnp.minimum(tile_e, N_EXPERTS - 1)
    n_used = (pend[-1] // tme).astype(jnp.int32)[None]
    e_ids = rt_p[:, 0:2]
    hit = e_ids[:, :, None] == jnp.arange(N_EXPERTS, dtype=F32)[None, None, :]
    offs_tok = jnp.sum(jnp.where(hit, offs[None, None, :], 0), axis=-1, dtype=jnp.int32)
    pos = (offs_tok + rt_p[:, 4:6].astype(jnp.int32)).reshape(n_asg)
    xs = _dispatch(pos, h2_p, n_rows, tm=256)
    ys = _gmm(tile_e, n_used, xs, wg, wu, wd, tm=tme)
    y_prompt = _combine(pos, x1_p, mod_p, rt_p, g_post_ffn, ys, seq=seq, tm=256)

    xs2 = x_sample.reshape(ts, d)
    pos_s = jnp.tile(past_len + jnp.arange(n_new), dec_batch)
    tabs_s = _rope_tables(pos_s)
    q_s, ckv_s, krp_s, fq_s, fk_s, fv_s, lfp_s = _proj_sample(
        xs2, mod_s, g_pre_mix, g_q_lora, g_kv_lora, bfg, tabs_s, w_all_hl, w_a_hl, w_b_hl)
    q4 = q_s.reshape(dec_batch, n_new, MLA_HEADS, QHEAD_W)
    nrow = n_new * MLA_HEADS
    qa = q4[..., :KV_LORA].reshape(dec_batch, nrow, KV_LORA)
    qr = q4[..., KV_LORA:].reshape(dec_batch, nrow, QHEAD_W - KV_LORA)
    head_mask = (jnp.arange(FOX_W)[None, :] // FOX_HEAD_DIM == jnp.arange(FOX_HEADS)[:, None])
    qf = (fq_s.reshape(dec_batch, n_new, 1, FOX_W) * head_mask[None, None].astype(F32))
    qf = qf.reshape(dec_batch, nrow, FOX_W)
    padk = page - n_new
    latn = jnp.pad(ckv_s.reshape(dec_batch, n_new, KV_LORA), ((0, 0), (0, padk), (0, 0)))
    krn = jnp.pad(jnp.transpose(krp_s[:, :QK_ROPE].reshape(dec_batch, n_new, QK_ROPE), (0, 2, 1)),
                  ((0, 0), (0, 0), (0, padk)))
    fkn = jnp.pad(jnp.transpose(fk_s.reshape(dec_batch, n_new, FOX_W), (0, 2, 1)),
                  ((0, 0), (0, 0), (0, padk)))
    fvn = jnp.pad(jnp.transpose(fv_s.reshape(dec_batch, n_new, FOX_W), (0, 2, 1)),
                  ((0, 0), (0, 0), (0, padk)))
    lfn = jnp.pad(jnp.transpose(lfp_s[:, :FOX_HEADS].reshape(dec_batch, n_new, FOX_HEADS),
                                (0, 2, 1)), ((0, 0), (0, 0), (0, padk)))
    c_lat = cache_mla_latent
    c_kr = jnp.transpose(cache_mla_krope, (0, 1, 3, 2))
    c_fk = jnp.transpose(cache_fox_k, (0, 1, 3, 4, 2))
    c_fv = jnp.transpose(cache_fox_v, (0, 1, 3, 4, 2))
    c_lf = jnp.transpose(cache_fox_logf, (0, 1, 3, 2))
    om, oft = _sample_attn(page_table, qa, qr, qf, latn, krn, fkn, fvn, lfn,
                           c_lat, c_kr, c_fk, c_fv, c_lf, pages=16)
    o_lat_s = om.reshape(dec_batch, n_new, MLA_HEADS * KV_LORA).reshape(ts, MLA_HEADS * KV_LORA)
    oft4 = oft[:, :, :nrow].reshape(dec_batch, FOX_HEADS, FOX_HEAD_DIM, n_new, FOX_HEADS)
    o_fox_s = jnp.einsum('bhdqh->bqhd', oft4).reshape(ts, FOX_W)
    x1_s, h2_s, rt_s = _post_sample(xs2, mod_s, o_lat_s, o_fox_s, w_ov_hl, w_of_hl,
                                    g_post_mix, g_pre_ffn, w_r_hl, b_r)
    y_sample = _moe_sample(h2_s, rt_s, x1_s, mod_s, g_post_ffn, wg, wu, wd)

    return (y_prompt.reshape(batch, seq, d), y_sample.reshape(dec_batch, n_new, d),
            ckv_p.reshape(batch, 1, seq, KV_LORA),
            jnp.transpose(kr_p.reshape(batch, 1, QK_ROPE, seq), (0, 1, 3, 2)),
            jnp.transpose(fk_p.reshape(batch, 1, FOX_HEADS, FOX_HEAD_DIM, seq), (0, 1, 4, 2, 3)),
            jnp.transpose(fv_p.reshape(batch, 1, FOX_HEADS, FOX_HEAD_DIM, seq), (0, 1, 4, 2, 3)),
            jnp.transpose(lf_p.reshape(batch, 1, FOX_HEADS, seq), (0, 1, 3, 2)),
            ckv_s.reshape(dec_batch, 1, n_new, KV_LORA),
            krp_s[:, :QK_ROPE].reshape(dec_batch, 1, n_new, QK_ROPE),
            fk_s.reshape(dec_batch, 1, n_new, FOX_HEADS, FOX_HEAD_DIM),
            fv_s.reshape(dec_batch, 1, n_new, FOX_HEADS, FOX_HEAD_DIM),
            lfp_s[:, :FOX_HEADS].reshape(dec_batch, 1, n_new, FOX_HEADS))
```

```python
import functools
import math

import jax
import jax.numpy as jnp
import numpy as np
from jax import lax
from jax.experimental import pallas as pl
from jax.experimental.pallas import tpu as pltpu

F32 = jnp.float32
BF16 = jnp.bfloat16

D_MODEL = 1024
MLA_HEADS = 8
QK_NOPE = 64
QK_ROPE = 32
V_HEAD = 64
Q_LORA = 256
KV_LORA = 128
FOX_HEADS = 8
FOX_HEAD_DIM = 64
FOX_W = FOX_HEADS * FOX_HEAD_DIM
N_GROUPS = 4
EXPERTS_PER_GROUP = 8
N_EXPERTS = N_GROUPS * EXPERTS_PER_GROUP
D_EXPERT = 256
ROPE_THETA = 10000.0
EPS = 1e-6
NEG = -1e30
LOG2E = 1.4426950408889634
MLA_SCALE = (QK_NOPE + QK_ROPE) ** -0.5
FOX_SCALE = FOX_HEAD_DIM ** -0.5

LANES = 128
QHEAD_W = 256
W_ALL_COLS = 2304
VMEM_LIMIT = 56 * 1024 * 1024


def _split(x):
    hi = x.astype(BF16)
    lo = (x - hi.astype(F32)).astype(BF16)
    return hi, lo


def _mm(a, b):
    return jnp.dot(a, b, preferred_element_type=F32)


def _mm_nt(a, b):
    return lax.dot_general(a, b, (((1,), (1,)), ((), ())), preferred_element_type=F32)


def _mm3(a, b_hi, b_lo):
    a_hi, a_lo = _split(a)
    m = a.shape[0]
    top = _mm(jnp.concatenate([a_hi, a_lo], axis=0), b_hi)
    return top[:m] + top[m:] + _mm(a_hi, b_lo)


def _mm3_nt(a, b_hi, b_lo):
    a_hi, a_lo = _split(a)
    m = a.shape[0]
    top = _mm_nt(jnp.concatenate([a_hi, a_lo], axis=0), b_hi)
    return top[:m] + top[m:] + _mm_nt(a_hi, b_lo)


def _rms(x, g):
    return x * lax.rsqrt(jnp.mean(x * x, axis=-1, keepdims=True) + EPS) * g


def _silu(x):
    return x / (1.0 + jnp.exp(-x))


def _log_sigmoid(x):
    return jnp.minimum(x, 0.0) - jnp.log(1.0 + jnp.exp(-jnp.abs(x)))


def _cparams(sem, vmem=None):
    return pltpu.CompilerParams(dimension_semantics=sem, vmem_limit_bytes=vmem)


def _full(shape):
    nd = len(shape)
    return pl.BlockSpec(shape, lambda *_: (0,) * nd)


def _ada_kernel(c_ref, w_ref, b_ref, o_ref):
    a = _silu(c_ref[...])
    w_hi, w_lo = _split(w_ref[...])
    o_ref[...] = _mm3(a, w_hi, w_lo) + b_ref[...]


def _ada(c_all, w_ada, b_ada):
    n, d = c_all.shape
    cols = w_ada.shape[1]
    tn = 1024
    return pl.pallas_call(
        _ada_kernel,
        grid=(cols // tn,),
        in_specs=[_full((n, d)),
                  pl.BlockSpec((d, tn), lambda j: (0, j)),
                  pl.BlockSpec((1, tn), lambda j: (0, j))],
        out_specs=pl.BlockSpec((n, tn), lambda j: (0, j)),
        out_shape=jax.ShapeDtypeStruct((n, cols), F32),
        compiler_params=_cparams(("parallel",)),
        name="ada_mod",
    )(c_all, w_ada, b_ada)


def _fold_kernel(a_ref, b_ref, o_ref):
    b_hi, b_lo = _split(b_ref[0])
    o_ref[0] = _mm3(a_ref[0], b_hi, b_lo)


def _fold(a, b):
    h, m, k = a.shape
    n = b.shape[2]
    return pl.pallas_call(
        _fold_kernel,
        grid=(h,),
        in_specs=[pl.BlockSpec((1, m, k), lambda i: (i, 0, 0)),
                  pl.BlockSpec((1, k, n), lambda i: (i, 0, 0))],
        out_specs=pl.BlockSpec((1, m, n), lambda i: (i, 0, 0)),
        out_shape=jax.ShapeDtypeStruct((h, m, n), F32),
        compiler_params=_cparams(("parallel",)),
        name="fold_weights",
    )(a, b)


def _proj_common(x, mod, g_pre, g_q, g_kv, bfg, tabs, mm_all, mm_qa, mm_qb):
    ctab, stab, kct, kst = tabs
    shift = mod[:, 0:D_MODEL]
    scale = mod[:, D_MODEL:2 * D_MODEL]
    h = _rms(x, g_pre) * (1.0 + scale) + shift
    z = mm_all(h)
    cq = _rms(z[:, 0:Q_LORA], g_q)
    nh = MLA_HEADS
    q = (mm_qa(cq) * jnp.concatenate([ctab] * nh, axis=1)
         + mm_qb(cq) * jnp.concatenate([stab] * nh, axis=1))
    c_kv = _rms(z[:, 256:384], g_kv)
    krp = z[:, 384:512] * kct + z[:, 512:640] * kst
    fq = z[:, 640:1152] * (FOX_SCALE * LOG2E)
    fk = z[:, 1152:1664]
    fv = z[:, 1664:2176]
    logf = _log_sigmoid(z[:, 2176:2304] + bfg)
    return q, c_kv, krp, fq, fk, fv, logf


def _proj_prompt_kernel(x_ref, mod_ref, gpre_ref, gq_ref, gkv_ref, bfg_ref,
                        ct_ref, st_ref, kct_ref, kst_ref,
                        wall_ref, wa_ref, wb_ref,
                        q_ref, kmla_ref, ckv_ref, kr_ref, fq_ref, fk_ref, fk16_ref,
                        fv_ref, fv16_ref, lf_ref, cum_ref, carry_ref, *, tiles_per_batch):
    i = pl.program_id(0)
    tm = x_ref.shape[0]
    q, c_kv, krp, fq, fk, fv, logf = _proj_common(
        x_ref[...], mod_ref[0], gpre_ref[...], gq_ref[...], gkv_ref[...], bfg_ref[...],
        (ct_ref[...], st_ref[...], kct_ref[...], kst_ref[...]),
        lambda h: _mm(h.astype(BF16), wall_ref[...]),
        lambda c: _mm(c.astype(BF16), wa_ref[...]),
        lambda c: _mm(c.astype(BF16), wb_ref[...]))
    q_ref[...] = q.astype(BF16)
    ckv_ref[...] = c_kv
    kr_ref[0] = jnp.transpose(krp)[0:QK_ROPE, :]
    kmla_ref[...] = jnp.concatenate([c_kv, krp], axis=1).astype(BF16)
    fq_ref[...] = fq.astype(BF16)
    fk_ref[0] = jnp.transpose(fk)
    fk16_ref[...] = fk.astype(BF16)
    fv_ref[0] = jnp.transpose(fv)
    fv16_ref[...] = fv.astype(BF16)
    lf_ref[0] = jnp.transpose(logf)[0:FOX_HEADS, :]

    @pl.when(i % tiles_per_batch == 0)
    def _():
        carry_ref[...] = jnp.zeros_like(carry_ref)

    row = lax.broadcasted_iota(jnp.int32, (tm, tm), 0)
    col = lax.broadcasted_iota(jnp.int32, (tm, tm), 1)
    tri = (row >= col).astype(BF16)
    p1 = logf.astype(BF16)
    r1 = logf - p1.astype(F32)
    p2 = r1.astype(BF16)
    p3 = (r1 - p2.astype(F32)).astype(BF16)
    cum = _mm(tri, p1) + _mm(tri, p2) + _mm(tri, p3) + carry_ref[0:1, :]
    carry_ref[0:1, :] = cum[tm - 1:tm, :]
    cum_ref[0] = jnp.transpose(cum * LOG2E)[0:FOX_HEADS, :]


def _proj_prompt(x2d, mod3, gpre, gq, gkv, bfg, tabs, wall, wa, wb, *, seq, tm):
    t = x2d.shape[0]
    tpb = seq // tm
    row = lambda w: pl.BlockSpec((tm, w), lambda i: (i, 0))
    pos = lambda w: pl.BlockSpec((tm, w), lambda i: (i % tpb, 0))
    nb = t // seq
    keym = lambda w: pl.BlockSpec((1, w, tm), lambda i: (i // tpb, 0, i % tpb))
    out_shapes = [
        jax.ShapeDtypeStruct((t, MLA_HEADS * QHEAD_W), BF16),
        jax.ShapeDtypeStruct((t, QHEAD_W), BF16),
        jax.ShapeDtypeStruct((t, KV_LORA), F32),
        jax.ShapeDtypeStruct((nb, QK_ROPE, seq), F32),
        jax.ShapeDtypeStruct((t, FOX_W), BF16),
        jax.ShapeDtypeStruct((nb, FOX_W, seq), F32),
        jax.ShapeDtypeStruct((t, FOX_W), BF16),
        jax.ShapeDtypeStruct((nb, FOX_W, seq), F32),
        jax.ShapeDtypeStruct((t, FOX_W), BF16),
        jax.ShapeDtypeStruct((nb, FOX_HEADS, seq), F32),
        jax.ShapeDtypeStruct((nb, FOX_HEADS, seq), F32),
    ]
    out_specs = [row(MLA_HEADS * QHEAD_W), row(QHEAD_W), row(KV_LORA), keym(QK_ROPE),
                 row(FOX_W), keym(FOX_W), row(FOX_W), keym(FOX_W), row(FOX_W),
                 keym(FOX_HEADS), keym(FOX_HEADS)]
    return pl.pallas_call(
        functools.partial(_proj_prompt_kernel, tiles_per_batch=tpb),
        grid=(t // tm,),
        in_specs=[row(D_MODEL),
                  pl.BlockSpec((1, 1, 6 * D_MODEL), lambda i: (i // tpb, 0, 0)),
                  _full(gpre.shape), _full(gq.shape), _full(gkv.shape), _full(bfg.shape),
                  pos(QHEAD_W), pos(QHEAD_W), pos(LANES), pos(LANES),
                  _full(wall.shape), _full(wa.shape), _full(wb.shape)],
        out_specs=out_specs,
        out_shape=out_shapes,
        scratch_shapes=[pltpu.VMEM((8, LANES), F32)],
        compiler_params=_cparams(("arbitrary",), VMEM_LIMIT),
        name="proj_prompt",
    )(x2d, mod3, gpre, gq, gkv, bfg, *tabs, wall, wa, wb)


def _proj_sample_kernel(x_ref, mod_ref, gpre_ref, gq_ref, gkv_ref, bfg_ref,
                        ct_ref, st_ref, kct_ref, kst_ref,
                        wall_hi, wall_lo, wa_hi, wa_lo, wb_hi, wb_lo,
                        q_ref, ckv_ref, krp_ref, fq_ref, fk_ref, fv_ref, lf_ref):
    q, c_kv, krp, fq, fk, fv, logf = _proj_common(
        x_ref[...], mod_ref[...], gpre_ref[...], gq_ref[...], gkv_ref[...], bfg_ref[...],
        (ct_ref[...], st_ref[...], kct_ref[...], kst_ref[...]),
        lambda h: _mm3(h, wall_hi[...], wall_lo[...]),
        lambda c: _mm3(c, wa_hi[...], wa_lo[...]),
        lambda c: _mm3(c, wb_hi[...], wb_lo[...]))
    q_ref[...] = q
    ckv_ref[...] = c_kv
    krp_ref[...] = krp
    fq_ref[...] = fq
    fk_ref[...] = fk
    fv_ref[...] = fv
    lf_ref[...] = logf


def _proj_sample(x2d, mod_tok, gpre, gq, gkv, bfg, tabs, wall, wa, wb):
    t = x2d.shape[0]
    ins = [x2d, mod_tok, gpre, gq, gkv, bfg, *tabs, *wall, *wa, *wb]
    shapes = [(t, MLA_HEADS * QHEAD_W), (t, KV_LORA), (t, LANES), (t, FOX_W), (t, FOX_W),
              (t, FOX_W), (t, LANES)]
    return pl.pallas_call(
        _proj_sample_kernel,
        grid=(1,),
        in_specs=[_full(a.shape) for a in ins],
        out_specs=[_full(s) for s in shapes],
        out_shape=[jax.ShapeDtypeStruct(s, F32) for s in shapes],
        compiler_params=_cparams(("arbitrary",), VMEM_LIMIT),
        name="proj_sample",
    )(*ins)


def _flash_update(s, v1, m_ref, acc_ref):
    m_prev = m_ref[...]
    m_new = jnp.maximum(m_prev, jnp.max(s, axis=1, keepdims=True))
    alpha = jnp.exp2(m_prev - m_new)
    p = jnp.exp2(s - jnp.concatenate([m_new] * (s.shape[1] // LANES), axis=1))
    acc_ref[...] = (jnp.concatenate([alpha, alpha], axis=1) * acc_ref[...]
                    + _mm(p.astype(BF16), v1))
    m_ref[...] = m_new


def _with_ones(v):
    return jnp.concatenate([v, jnp.ones(v.shape, v.dtype)], axis=1)


def _causal_bounds(i, tq, tk):
    n_full = (i * tq + 1) // tk
    n_all = (i * tq + tq + tk - 1) // tk
    return n_full, n_all


def _mla_prompt_kernel(q_ref, k_ref, o_ref, m_ref, acc_ref, *, tq, tk, chunk):
    i = pl.program_id(1)
    nh = MLA_HEADS
    qs = jnp.concatenate([q_ref[:, h * QHEAD_W:(h + 1) * QHEAD_W] for h in range(nh)], axis=0)
    m_ref[...] = jnp.full(m_ref.shape, NEG, F32)
    acc_ref[...] = jnp.zeros(acc_ref.shape, F32)

    def step(j, masked):
        kb = k_ref[pl.ds(pl.multiple_of(j * tk, tk), tk), :]
        vb = _with_ones(kb[:, 0:KV_LORA])
        s_all = _mm_nt(qs, kb)
        for c in range(nh * tq // chunk):
            rows = slice(c * chunk, (c + 1) * chunk)
            s = s_all[rows]
            if masked:
                row = lax.broadcasted_iota(jnp.int32, s.shape, 0)
                col = lax.broadcasted_iota(jnp.int32, s.shape, 1)
                s = jnp.where(j * tk + col <= i * tq + ((c * chunk + row) & (tq - 1)), s, NEG)
            _flash_update(s, vb, m_ref.at[rows], acc_ref.at[rows])

    n_full, n_all = _causal_bounds(i, tq, tk)
    lax.fori_loop(0, n_full, lambda j, c: (step(j, False), c)[1], 0)
    lax.fori_loop(n_full, n_all, lambda j, c: (step(j, True), c)[1], 0)
    o = acc_ref[:, 0:KV_LORA] / acc_ref[:, KV_LORA:2 * KV_LORA]
    o_ref[...] = jnp.concatenate([o[h * tq:(h + 1) * tq] for h in range(nh)], axis=1).astype(BF16)


def _mla_prompt(q, kmla, *, batch, seq, tq, tk):
    t = q.shape[0]
    nq = seq // tq
    rows = MLA_HEADS * tq
    return pl.pallas_call(
        functools.partial(_mla_prompt_kernel, tq=tq, tk=tk, chunk=128),
        grid=(batch, nq),
        in_specs=[pl.BlockSpec((tq, MLA_HEADS * QHEAD_W), lambda b, i: (b * nq + i, 0)),
                  pl.BlockSpec((seq, QHEAD_W), lambda b, i: (b, 0))],
        out_specs=pl.BlockSpec((tq, MLA_HEADS * KV_LORA), lambda b, i: (b * nq + i, 0)),
        out_shape=jax.ShapeDtypeStruct((t, MLA_HEADS * KV_LORA), BF16),
        scratch_shapes=[pltpu.VMEM((rows, LANES), F32), pltpu.VMEM((rows, 2 * KV_LORA), F32)],
        compiler_params=_cparams(("parallel", "parallel"), VMEM_LIMIT),
        name="mla_prompt",
    )(q, kmla)


def _fox_prompt_kernel(q_ref, k_ref, v_ref, ck_ref, o_ref, m_ref, acc_ref, *, tq, tk, chunk):
    i = pl.program_id(1)
    lane = lax.broadcasted_iota(jnp.int32, (tq, LANES), 1)
    n_full, n_all = _causal_bounds(i, tq, tk)
    lo_half = lane < FOX_HEAD_DIM
    m_ref[...] = jnp.full(m_ref.shape, NEG, F32)
    acc_ref[...] = jnp.zeros(acc_ref.shape, F32)

    def step(j, masked):
        keys = pl.ds(pl.multiple_of(j * tk, tk), tk)

        def scores(h):
            cols = slice((h // 2) * LANES, (h // 2 + 1) * LANES)
            qp = q_ref[:, cols]
            sel = lo_half if h % 2 == 0 else jnp.logical_not(lo_half)
            qm = jnp.where(sel, qp, jnp.zeros_like(qp))
            return _mm_nt(qm, k_ref[keys, cols])

        def values(h, s_all):
            cols = slice((h // 2) * LANES, (h // 2 + 1) * LANES)
            vb = _with_ones(v_ref[keys, cols])
            ck = ck_ref[0, h, pl.ds(j, 1), :]
            for c in range(tq // chunk):
                rows = slice(c * chunk, (c + 1) * chunk)
                s = s_all[rows] - ck
                if masked:
                    row = lax.broadcasted_iota(jnp.int32, s.shape, 0)
                    col = lax.broadcasted_iota(jnp.int32, s.shape, 1)
                    s = jnp.where(j * tk + col <= i * tq + c * chunk + row, s, NEG)
                _flash_update(s, vb, m_ref.at[h, rows], acc_ref.at[h, rows])

        s_next = scores(0)
        for h in range(FOX_HEADS):
            s_cur = s_next
            if h + 1 < FOX_HEADS:
                s_next = scores(h + 1)
            values(h, s_cur)

    lax.fori_loop(0, n_full, lambda j, c: (step(j, False), c)[1], 0)
    lax.fori_loop(n_full, n_all, lambda j, c: (step(j, True), c)[1], 0)
    for pair in range(FOX_HEADS // 2):
        o0 = acc_ref[2 * pair, :, 0:LANES] / acc_ref[2 * pair, :, LANES:2 * LANES]
        o1 = acc_ref[2 * pair + 1, :, 0:LANES] / acc_ref[2 * pair + 1, :, LANES:2 * LANES]
        o_ref[:, pair * LANES:(pair + 1) * LANES] = jnp.where(lo_half, o0, o1).astype(BF16)


def _fox_prompt(fq, fk, fv, ck4, *, batch, seq, tq, tk):
    t = fq.shape[0]
    nq = seq // tq
    nk = seq // tk
    return pl.pallas_call(
        functools.partial(_fox_prompt_kernel, tq=tq, tk=tk, chunk=128),
        grid=(batch, nq),
        in_specs=[pl.BlockSpec((tq, FOX_W), lambda b, i: (b * nq + i, 0)),
                  pl.BlockSpec((seq, FOX_W), lambda b, i: (b, 0)),
                  pl.BlockSpec((seq, FOX_W), lambda b, i: (b, 0)),
                  pl.BlockSpec((1, FOX_HEADS, nk, tk), lambda b, i: (b, 0, 0, 0))],
        out_specs=pl.BlockSpec((tq, FOX_W), lambda b, i: (b * nq + i, 0)),
        out_shape=jax.ShapeDtypeStruct((t, FOX_W), BF16),
        scratch_shapes=[pltpu.VMEM((FOX_HEADS, tq, LANES), F32),
                        pltpu.VMEM((FOX_HEADS, tq, 2 * LANES), F32)],
        compiler_params=_cparams(("parallel", "parallel"), VMEM_LIMIT),
        name="fox_prompt",
    )(fq, fk, fv, ck4)


def _route(r):
    lane = lax.broadcasted_iota(jnp.int32, r.shape, 1)
    lanef = lane.astype(F32)
    big = jnp.float32(1e9)
    gmask = lane < N_GROUPS
    gl = jnp.where(gmask, r, -jnp.inf)
    gm = jnp.max(gl, axis=1, keepdims=True)
    gs = jnp.sum(jnp.where(gmask, jnp.exp(r - gm), 0.0), axis=1, keepdims=True)
    g_w = 1.0 / gs
    g_idx = jnp.min(jnp.where(gl == gm, lanef, big), axis=1, keepdims=True)
    egrp = ((lane - N_GROUPS) >> 3).astype(F32)
    emask = (lane >= N_GROUPS) & (lane < N_GROUPS + N_EXPERTS) & (egrp == g_idx)
    el = jnp.where(emask, r, -jnp.inf)
    em = jnp.max(el, axis=1, keepdims=True)
    ee = jnp.where(emask, jnp.exp(r - em), 0.0)
    p = ee / jnp.sum(ee, axis=1, keepdims=True)
    pm = jnp.where(emask, p, -1.0)
    p1 = jnp.max(pm, axis=1, keepdims=True)
    i1 = jnp.min(jnp.where(pm == p1, lanef, big), axis=1, keepdims=True)
    pm2 = jnp.where(lanef == i1, -1.0, pm)
    p2 = jnp.max(pm2, axis=1, keepdims=True)
    i2 = jnp.min(jnp.where(pm2 == p2, lanef, big), axis=1, keepdims=True)
    den = p1 + p2
    c1 = g_w * (p1 / den)
    c2 = g_w * (p2 / den)
    out = jnp.where(lane == 0, i1 - N_GROUPS,
                    jnp.where(lane == 1, i2 - N_GROUPS,
                              jnp.where(lane == 2, c1, jnp.where(lane == 3, c2, 0.0))))
    return out


def _post_common(x, mod, mixed, gpost, gffn, wr_hi, wr_lo, br):
    gate_m = mod[:, 2 * D_MODEL:3 * D_MODEL]
    shift_f = mod[:, 3 * D_MODEL:4 * D_MODEL]
    scale_f = mod[:, 4 * D_MODEL:5 * D_MODEL]
    x1 = x + gate_m * _rms(mixed, gpost)
    h2 = _rms(x1, gffn) * (1.0 + scale_f) + shift_f
    r = _mm3(h2, wr_hi, wr_lo) + br
    return x1, h2, _route(r)


def _post_prompt_kernel(x_ref, mod_ref, ol_ref, of_ref, wov_ref, wof_ref, gpost_ref, gffn_ref,
                        wrh_ref, wrl_ref, br_ref, x1_ref, h2_ref, rt_ref, cnt_ref, carry_ref):
    i = pl.program_id(0)
    tm = x_ref.shape[0]
    mixed = _mm(ol_ref[...], wov_ref[...]) + _mm(of_ref[...], wof_ref[...])
    x1, h2, rt = _post_common(x_ref[...], mod_ref[0], mixed, gpost_ref[...], gffn_ref[...],
                              wrh_ref[...], wrl_ref[...], br_ref[...])
    x1_ref[...] = x1
    h2_ref[...] = h2

    @pl.when(i == 0)
    def _():
        carry_ref[...] = jnp.zeros_like(carry_ref)

    lane = lax.broadcasted_iota(jnp.int32, rt.shape, 1)
    lanef = lane.astype(F32)
    oh1 = (lanef == rt[:, 0:1]).astype(F32)
    oh2 = (lanef == rt[:, 1:2]).astype(F32)
    row = lax.broadcasted_iota(jnp.int32, (tm, tm), 0)
    col = lax.broadcasted_iota(jnp.int32, (tm, tm), 1)
    before = _mm((row > col).astype(BF16), (oh1 + oh2).astype(BF16)) + carry_ref[0:1, :]
    rank1 = jnp.sum(oh1 * before, axis=1, keepdims=True)
    rank2 = jnp.sum(oh2 * (before + oh1), axis=1, keepdims=True)
    total = before[tm - 1:tm, :] + oh1[tm - 1:tm, :] + oh2[tm - 1:tm, :]
    carry_ref[0:1, :] = total
    cnt_ref[...] = jnp.broadcast_to(total, cnt_ref.shape)
    rt_ref[...] = jnp.where(lane == 4, rank1, jnp.where(lane == 5, rank2, rt))


def _post_prompt(x2d, mod3, o_lat, o_fox, wov, wof, gpost, gffn, wr, br, *, seq, tm):
    t = x2d.shape[0]
    tpb = seq // tm
    row = lambda w: pl.BlockSpec((tm, w), lambda i: (i, 0))
    return pl.pallas_call(
        _post_prompt_kernel,
        grid=(t // tm,),
        in_specs=[row(D_MODEL),
                  pl.BlockSpec((1, 1, 6 * D_MODEL), lambda i: (i // tpb, 0, 0)),
                  row(MLA_HEADS * KV_LORA), row(FOX_W),
                  _full(wov.shape), _full(wof.shape), _full(gpost.shape), _full(gffn.shape),
                  _full(wr[0].shape), _full(wr[1].shape), _full(br.shape)],
        out_specs=[row(D_MODEL), row(D_MODEL), row(LANES), _full((8, LANES))],
        out_shape=[jax.ShapeDtypeStruct((t, D_MODEL), F32),
                   jax.ShapeDtypeStruct((t, D_MODEL), F32),
                   jax.ShapeDtypeStruct((t, LANES), F32),
                   jax.ShapeDtypeStruct((8, LANES), F32)],
        scratch_shapes=[pltpu.VMEM((8, LANES), F32)],
        compiler_params=_cparams(("arbitrary",), VMEM_LIMIT),
        name="post_prompt",
    )(x2d, mod3, o_lat, o_fox, wov, wof, gpost, gffn, wr[0], wr[1], br)


def _post_sample_kernel(x_ref, mod_ref, ol_ref, of_ref, wovh_ref, wovl_ref, wofh_ref, wofl_ref,
                        gpost_ref, gffn_ref, wrh_ref, wrl_ref, br_ref, x1_ref, h2_ref, rt_ref):
    mixed = (_mm3(ol_ref[...], wovh_ref[...], wovl_ref[...])
             + _mm3(of_ref[...], wofh_ref[...], wofl_ref[...]))
    x1, h2, rt = _post_common(x_ref[...], mod_ref[...], mixed, gpost_ref[...], gffn_ref[...],
                              wrh_ref[...], wrl_ref[...], br_ref[...])
    x1_ref[...] = x1
    h2_ref[...] = h2
    rt_ref[...] = rt


def _post_sample(x2d, mod_tok, o_lat, o_fox, wov, wof, gpost, gffn, wr, br):
    t = x2d.shape[0]
    ins = [x2d, mod_tok, o_lat, o_fox, *wov, *wof, gpost, gffn, *wr, br]
    shapes = [(t, D_MODEL), (t, D_MODEL), (t, LANES)]
    return pl.pallas_call(
        _post_sample_kernel,
        grid=(1,),
        in_specs=[_full(a.shape) for a in ins],
        out_specs=[_full(s) for s in shapes],
        out_shape=[jax.ShapeDtypeStruct(s, F32) for s in shapes],
        compiler_params=_cparams(("arbitrary",), VMEM_LIMIT),
        name="post_sample",
    )(*ins)


def _dispatch_kernel(pos_ref, h2_ref, zero_ref, xs_ref, sem):
    del zero_ref
    i = pl.program_id(0)
    tm = h2_ref.shape[0]
    base = i * (2 * tm)

    def body(t, c):
        for k in range(2):
            p = pos_ref[base + 2 * t + k]
            pltpu.make_async_copy(h2_ref.at[pl.ds(t, 1)], xs_ref.at[pl.ds(p, 1)], sem).start()
        return c

    lax.fori_loop(0, tm, body, 0, unroll=8)
    for _ in range(2):
        pltpu.make_async_copy(h2_ref, xs_ref.at[pl.ds(0, tm)], sem).wait()


def _dispatch(pos, h2, n_rows, *, tm):
    t = h2.shape[0]
    grid_spec = pltpu.PrefetchScalarGridSpec(
        num_scalar_prefetch=1,
        grid=(t // tm,),
        in_specs=[pl.BlockSpec((tm, h2.shape[1]), lambda i, pos: (i, 0)),
                  pl.BlockSpec(memory_space=pl.ANY)],
        out_specs=pl.BlockSpec(memory_space=pl.ANY),
        scratch_shapes=[pltpu.SemaphoreType.DMA],
    )
    return pl.pallas_call(
        _dispatch_kernel,
        grid_spec=grid_spec,
        out_shape=jax.ShapeDtypeStruct((n_rows, h2.shape[1]), h2.dtype),
        input_output_aliases={2: 0},
        compiler_params=_cparams(("arbitrary",), VMEM_LIMIT),
        name="moe_dispatch",
    )(pos, h2, jnp.zeros((n_rows, h2.shape[1]), h2.dtype))


def _gmm_kernel(te_ref, nu_ref, x_ref, wg_ref, wu_ref, wd_ref, o_ref):
    i = pl.program_id(0)

    @pl.when(i < nu_ref[0])
    def _():
        x = x_ref[...].astype(BF16)
        a = _mm(x, wg_ref[0].astype(BF16))
        u = _mm(x, wu_ref[0].astype(BF16))
        act = (_silu(a) * u).astype(BF16)
        o_ref[...] = _mm(act, wd_ref[0].astype(BF16))

    @pl.when(i >= nu_ref[0])
    def _():
        o_ref[...] = jnp.zeros_like(o_ref)


def _gmm(tile_expert, n_used, xs, wg, wu, wd, *, tm):
    nr = xs.shape[0]
    grid_spec = pltpu.PrefetchScalarGridSpec(
        num_scalar_prefetch=2,
        grid=(nr // tm,),
        in_specs=[pl.BlockSpec((tm, D_MODEL), lambda i, te, nu: (i, 0)),
                  pl.BlockSpec((1, D_MODEL, D_EXPERT), lambda i, te, nu: (te[i], 0, 0)),
                  pl.BlockSpec((1, D_MODEL, D_EXPERT), lambda i, te, nu: (te[i], 0, 0)),
                  pl.BlockSpec((1, D_EXPERT, D_MODEL), lambda i, te, nu: (te[i], 0, 0))],
        out_specs=pl.BlockSpec((tm, D_MODEL), lambda i, te, nu: (i, 0)),
    )
    return pl.pallas_call(
        _gmm_kernel,
        grid_spec=grid_spec,
        out_shape=jax.ShapeDtypeStruct((nr, D_MODEL), F32),
        compiler_params=_cparams(("arbitrary",), VMEM_LIMIT),
        name="moe_gmm",
    )(tile_expert, n_used, xs, wg, wu, wd)


def _combine_kernel(pos_ref, x1_ref, mod_ref, rt_ref, g_ref, ys_ref, o_ref, buf_ref, sem):
    i = pl.program_id(0)
    n = pl.num_programs(0)
    tm = x1_ref.shape[0]

    def issue(tile, slot):
        base = tile * (2 * tm)

        def body(t, c):
            for k in range(2):
                p = pos_ref[base + 2 * t + k]
                pltpu.make_async_copy(ys_ref.at[pl.ds(p, 1)], buf_ref.at[slot, k, pl.ds(t, 1)],
                                      sem.at[slot]).start()
            return c

        lax.fori_loop(0, tm, body, 0, unroll=8)

    @pl.when(i == 0)
    def _():
        issue(0, 0)

    @pl.when(i + 1 < n)
    def _():
        issue(i + 1, (i + 1) % 2)

    slot = i % 2
    for k in range(2):
        pltpu.make_async_copy(ys_ref.at[pl.ds(0, tm)], buf_ref.at[slot, k], sem.at[slot]).wait()
    gate_f = mod_ref[0][:, 5 * D_MODEL:6 * D_MODEL]
    rt = rt_ref[...]
    y = rt[:, 2:3] * buf_ref[slot, 0] + rt[:, 3:4] * buf_ref[slot, 1]
    o_ref[...] = x1_ref[...] + gate_f * _rms(y, g_ref[...])


def _combine(pos, x1, mod3, route, gpost, ys, *, seq, tm):
    t = x1.shape[0]
    tpb = seq // tm
    row = lambda w: pl.BlockSpec((tm, w), lambda i, pos: (i, 0))
    grid_spec = pltpu.PrefetchScalarGridSpec(
        num_scalar_prefetch=1,
        grid=(t // tm,),
        in_specs=[row(D_MODEL),
                  pl.BlockSpec((1, 1, 6 * D_MODEL), lambda i, pos: (i // tpb, 0, 0)),
                  row(LANES),
                  pl.BlockSpec(gpost.shape, lambda i, pos: (0, 0)),
                  pl.BlockSpec(memory_space=pl.ANY)],
        out_specs=row(D_MODEL),
        scratch_shapes=[pltpu.VMEM((2, 2, tm, D_MODEL), F32), pltpu.SemaphoreType.DMA((2,))],
    )
    return pl.pallas_call(
        _combine_kernel,
        grid_spec=grid_spec,
        out_shape=jax.ShapeDtypeStruct((t, D_MODEL), F32),
        compiler_params=_cparams(("arbitrary",), VMEM_LIMIT),
        name="moe_combine",
    )(pos, x1, mod3, route, gpost, ys)


def _moe_sample_kernel(h2_ref, rt_ref, x1_ref, mod_ref, g_ref, wg_ref, wu_ref, wd_ref,
                       o_ref, acc_ref):
    e = pl.program_id(0)

    @pl.when(e == 0)
    def _():
        acc_ref[...] = jnp.zeros_like(acc_ref)

    h2 = h2_ref[...]
    rt = rt_ref[...]
    ef = e.astype(F32)
    cw = (jnp.where(rt[:, 0:1] == ef, rt[:, 2:3], 0.0)
          + jnp.where(rt[:, 1:2] == ef, rt[:, 3:4], 0.0))
    a = _mm3(h2, *_split(wg_ref[0]))
    u = _mm3(h2, *_split(wu_ref[0]))
    act = _silu(a) * u * cw
    acc_ref[...] += _mm3(act, *_split(wd_ref[0]))

    @pl.when(e == pl.num_programs(0) - 1)
    def _():
        gate_f = mod_ref[:, 5 * D_MODEL:6 * D_MODEL]
        o_ref[...] = x1_ref[...] + gate_f * _rms(acc_ref[...], g_ref[...])


def _moe_sample(h2, route, x1, mod_tok, gpost, wg, wu, wd):
    t = h2.shape[0]
    return pl.pallas_call(
        _moe_sample_kernel,
        grid=(N_EXPERTS,),
        in_specs=[_full(h2.shape), _full(route.shape), _full(x1.shape), _full(mod_tok.shape),
                  _full(gpost.shape),
                  pl.BlockSpec((1, D_MODEL, D_EXPERT), lambda e: (e, 0, 0)),
                  pl.BlockSpec((1, D_MODEL, D_EXPERT), lambda e: (e, 0, 0)),
                  pl.BlockSpec((1, D_EXPERT, D_MODEL), lambda e: (e, 0, 0))],
        out_specs=_full((t, D_MODEL)),
        out_shape=jax.ShapeDtypeStruct((t, D_MODEL), F32),
        scratch_shapes=[pltpu.VMEM((t, D_MODEL), F32)],
        compiler_params=_cparams(("arbitrary",), VMEM_LIMIT),
        name="moe_sample",
    )(h2, route, x1, mod_tok, gpost, wg, wu, wd)


def _rev_cumsum_lanes(x):
    lane = lax.broadcasted_iota(jnp.int32, x.shape, 1)
    sh = 1
    while sh < LANES:
        x = x + jnp.where(lane + sh < LANES, pltpu.roll(x, LANES - sh, 1), 0.0)
        sh *= 2
    return x


def _fwd_cumsum_lanes(x):
    lane = lax.broadcasted_iota(jnp.int32, x.shape, 1)
    sh = 1
    while sh < LANES:
        x = x + jnp.where(lane >= sh, pltpu.roll(x, sh, 1), 0.0)
        sh *= 2
    return x


def _seg_update(s, m_ref, l_ref):
    m_prev = m_ref[...]
    m_new = jnp.maximum(m_prev, jnp.max(s, axis=1, keepdims=True))
    alpha = jnp.exp2(m_prev - m_new)
    p = jnp.exp2(s - m_new)
    l_ref[...] = alpha * l_ref[...] + jnp.sum(p, axis=1, keepdims=True)
    m_ref[...] = m_new
    return p, alpha


def _col_to_lanes(col, fill=0.0):
    n = col.shape[0]
    row = lax.broadcasted_iota(jnp.int32, (n, LANES), 0)
    lane = lax.broadcasted_iota(jnp.int32, (n, LANES), 1)
    vec = jnp.sum(jnp.where(row == lane, col, 0.0), axis=0, keepdims=True)
    return jnp.where(lane[0:1, :] < n, vec, fill)


def _sample_attn_kernel(pt_ref, qa_ref, qr_ref, qf_ref, latn_ref, krn_ref, fkn_ref, fvn_ref,
                        lfn_ref, clat_ref, ckr_ref, cfk_ref, cfv_ref, clf_ref, om_ref, of_ref,
                        mm_ref, lm_ref, accm_ref, mf_ref, lfs_ref, accf_ref, carry_ref,
                        lat_buf, kr_buf, fk_buf, fv_buf, lf_buf, sem_a, sem_b, *, pages, n_chunks):
    npg = pages
    b = pl.program_id(0)
    c = pl.program_id(1)
    nrow = qa_ref.shape[1]
    n_steps = pl.num_programs(0) * n_chunks
    step = b * n_chunks + c
    slot = step % 2

    def start(chunk_step, dst_slot, group_a):
        cb = chunk_step // n_chunks
        cc = chunk_step % n_chunks
        for p in range(npg):
            page = pt_ref[cb, (n_chunks - 1 - cc) * npg + p]
            keys = pl.ds(p * LANES, LANES)
            if group_a:
                sem = sem_a.at[dst_slot]
                pltpu.make_async_copy(clat_ref.at[page, 0], lat_buf.at[dst_slot, keys, :],
                                      sem).start()
                pltpu.make_async_copy(ckr_ref.at[page, 0], kr_buf.at[dst_slot, :, keys],
                                      sem).start()
                pltpu.make_async_copy(cfk_ref.at[page, 0], fk_buf.at[dst_slot, :, :, keys],
                                      sem).start()
                pltpu.make_async_copy(clf_ref.at[page, 0], lf_buf.at[dst_slot, :, keys],
                                      sem).start()
            else:
                pltpu.make_async_copy(cfv_ref.at[page, 0], fv_buf.at[dst_slot, :, :, keys],
                                      sem_b.at[dst_slot]).start()

    def wait(dst_slot, group_a):
        bufs = (lat_buf, kr_buf, fk_buf, lf_buf) if group_a else (fv_buf,)
        sem = sem_a.at[dst_slot] if group_a else sem_b.at[dst_slot]
        for buf in bufs:
            pltpu.make_async_copy(buf.at[dst_slot], buf.at[dst_slot], sem).wait()

    nxt = jnp.minimum(step + 1, n_steps - 1)

    @pl.when(step == 0)
    def _():
        start(step, slot, True)
        start(step, slot, False)

    start(nxt, 1 - slot, True)
    wait(slot, True)

    @pl.when(c == 0)
    def _():
        mm_ref[...] = jnp.full(mm_ref.shape, NEG, F32)
        mf_ref[...] = jnp.full(mf_ref.shape, NEG, F32)
        lm_ref[...] = jnp.zeros(lm_ref.shape, F32)
        lfs_ref[...] = jnp.zeros(lfs_ref.shape, F32)
        accm_ref[...] = jnp.zeros(accm_ref.shape, F32)
        accf_ref[...] = jnp.zeros(accf_ref.shape, F32)
        carry_ref[...] = jnp.zeros(carry_ref.shape, F32)

    def stack(x):
        hi, lo = _split(x)
        return hi, jnp.concatenate([hi, lo], axis=0)

    qa_hi, qa_cat = stack(qa_ref[0])
    qr_hi, qr_cat = stack(qr_ref[0][:, 0:QK_ROPE])
    qf_hi, qf_cat = stack(qf_ref[0])

    def attend(lat, krt, fkt, get_fvt, bias_f, bias_m, ngrp, hooks=()):
        width = lat.shape[0] // ngrp
        groups = []
        for g in range(ngrp):
            ks = slice(g * width, (g + 1) * width)
            lh, ll = _split(lat[ks, :])
            kh, kl = _split(krt[:, ks])
            fh, fl = _split(fkt[:, ks])
            top = _mm_nt(qa_cat, lh) + _mm(qr_cat, kh)
            s_m = top[:nrow] + top[nrow:] + _mm_nt(qa_hi, ll) + _mm(qr_hi, kl)
            top = _mm(qf_cat, fh)
            s_f = (top[:nrow] + top[nrow:] + _mm(qf_hi, fl)
                   + bias_f[:, g * width:(g + 1) * width])
            if bias_m is not None:
                s_m = s_m + bias_m
            groups.append((s_m, s_f, lh, ll))
            if g == 0 and hooks:
                hooks[0]()
        if hooks:
            hooks[1]()
        fvt = get_fvt()
        for g in range(ngrp):
            s_m, s_f, lh, ll = groups[g]
            ks = slice(g * width, (g + 1) * width)
            p_m, a_m = _seg_update(s_m, mm_ref, lm_ref)
            p_f, a_f = _seg_update(s_f, mf_ref, lfs_ref)
            pm_hi, pm_lo = _split(p_m)
            pf_hi, pf_lo = _split(p_f)
            zrow = jnp.zeros(pm_hi.shape, BF16)
            pm_cat = jnp.concatenate([pm_hi, pm_lo], axis=0)
            pm_hi0 = jnp.concatenate([pm_hi, zrow], axis=0)
            pf_cat = jnp.concatenate([pf_hi, pf_lo, zrow, zrow], axis=0)
            pf_hi0 = jnp.concatenate([pf_hi, zrow, zrow, zrow], axis=0)
            vh, vl = _split(fvt[:, ks])
            upd_m = _mm(pm_cat, lh) + _mm(pm_hi0, ll)
            upd_f = _mm_nt(vh, pf_cat) + _mm_nt(vl, pf_hi0)
            accm_ref[...] = jnp.concatenate([a_m, a_m], axis=0) * accm_ref[...] + upd_m
            a_lanes = _col_to_lanes(a_f)
            accf_ref[...] = (a_lanes + pltpu.roll(a_lanes, nrow, 1)) * accf_ref[...] + upd_f
            if g == 0 and hooks:
                hooks[2]()

    sufs = [None] * npg
    carry = carry_ref[...]
    for p in range(npg - 1, -1, -1):
        lf = lf_buf[slot, :, p * LANES:(p + 1) * LANES]
        inc = _rev_cumsum_lanes(lf)
        sufs[p] = (inc - lf + carry) * LOG2E
        carry = carry + inc[:, 0:1]
    carry_ref[...] = jnp.broadcast_to(carry[:, 0:1], carry_ref.shape)
    suf = jnp.concatenate(sufs, axis=1)
    bias_f = jnp.concatenate([suf] * (nrow // FOX_HEADS), axis=0)
    n_keys = npg * LANES

    def request_values():
        start(nxt, 1 - slot, False)
        wait(slot, False)

    hooks = (lambda: None, request_values, lambda: None)
    attend(lat_buf[slot], kr_buf[slot], fk_buf[slot].reshape(FOX_W, n_keys),
           lambda: fv_buf[slot].reshape(FOX_W, n_keys), bias_f, None, 2, hooks)

    @pl.when(step == n_steps - 1)
    def _():
        wait(1 - slot, True)
        wait(1 - slot, False)

    @pl.when(c == n_chunks - 1)
    def _():
        row = lax.broadcasted_iota(jnp.int32, (nrow, LANES), 0)
        lane = lax.broadcasted_iota(jnp.int32, (nrow, LANES), 1)
        ok = lane <= (row >> 3)
        cum_new = _fwd_cumsum_lanes(lfn_ref[0]) * LOG2E
        b_f = jnp.where(ok, -jnp.concatenate([cum_new] * (nrow // FOX_HEADS), axis=0), NEG)
        b_m = jnp.where(ok, 0.0, NEG)
        attend(latn_ref[0], krn_ref[0], fkn_ref[0], lambda: fvn_ref[0], b_f, b_m, 1)
        accm = accm_ref[...]
        om_ref[0] = (accm[:nrow] + accm[nrow:]) / lm_ref[...]
        accf = accf_ref[...]
        of_ref[0] = (accf + pltpu.roll(accf, LANES - nrow, 1)) / _col_to_lanes(lfs_ref[...], 1.0)


def _sample_attn(page_table, qa, qr, qf, latn, krn, fkn, fvn, lfn,
                 c_lat, c_kr, c_fk, c_fv, c_lf, *, pages):
    nb, n_pages = page_table.shape
    n_chunks = n_pages // pages
    nrow = qa.shape[1]

    def per_b(shape):
        nd = len(shape)
        return pl.BlockSpec((1,) + shape[1:], lambda b, c, pt: (b,) + (0,) * (nd - 1))

    small = [qa, qr, qf, latn, krn, fkn, fvn, lfn]
    caches = [c_lat, c_kr, c_fk, c_fv, c_lf]
    in_specs = [per_b(a.shape) for a in small] + [pl.BlockSpec(memory_space=pl.ANY)] * len(caches)
    operands = small + caches
    n_keys = pages * c_lat.shape[2]
    grid_spec = pltpu.PrefetchScalarGridSpec(
        num_scalar_prefetch=1,
        grid=(nb, n_chunks),
        in_specs=in_specs,
        out_specs=[pl.BlockSpec((1, nrow, KV_LORA), lambda b, c, pt: (b, 0, 0)),
                   pl.BlockSpec((1, FOX_W, LANES), lambda b, c, pt: (b, 0, 0))],
        scratch_shapes=[pltpu.VMEM((nrow, 1), F32), pltpu.VMEM((nrow, 1), F32),
                        pltpu.VMEM((2 * nrow, KV_LORA), F32),
                        pltpu.VMEM((nrow, 1), F32), pltpu.VMEM((nrow, 1), F32),
                        pltpu.VMEM((FOX_W, LANES), F32),
                        pltpu.VMEM((FOX_HEADS, LANES), F32),
                        pltpu.VMEM((2, n_keys, KV_LORA), F32),
                        pltpu.VMEM((2, QK_ROPE, n_keys), F32),
                        pltpu.VMEM((2, FOX_HEADS, FOX_HEAD_DIM, n_keys), F32),
                        pltpu.VMEM((2, FOX_HEADS, FOX_HEAD_DIM, n_keys), F32),
                        pltpu.VMEM((2, FOX_HEADS, n_keys), F32),
                        pltpu.SemaphoreType.DMA((2,)), pltpu.SemaphoreType.DMA((2,))],
    )
    return pl.pallas_call(
        functools.partial(_sample_attn_kernel, pages=pages, n_chunks=n_chunks),
        grid_spec=grid_spec,
        out_shape=[jax.ShapeDtypeStruct((nb, nrow, KV_LORA), F32),
                   jax.ShapeDtypeStruct((nb, FOX_W, LANES), F32)],
        compiler_params=_cparams(("arbitrary", "arbitrary"), VMEM_LIMIT),
        name="sample_attn",
    )(page_table, *operands)


def _hl(w):
    bits = lax.bitcast_convert_type(w, jnp.uint32) & jnp.uint32(0xFFFF0000)
    hi = lax.bitcast_convert_type(bits, F32)
    return hi.astype(BF16), (w - hi).astype(BF16)


def _rope_tables(pos):
    half = QK_ROPE // 2
    inv = ROPE_THETA ** (-jnp.arange(half, dtype=F32) / half)
    ang = pos.astype(F32)[:, None] * inv[None, :]
    cc = jnp.concatenate([jnp.cos(ang), jnp.cos(ang)], axis=1)
    ss = jnp.concatenate([jnp.sin(ang), jnp.sin(ang)], axis=1)
    n = pos.shape[0]
    qs = MLA_SCALE * LOG2E
    ctab = jnp.concatenate([jnp.full((n, KV_LORA), qs, F32), cc * qs,
                            jnp.zeros((n, QHEAD_W - KV_LORA - QK_ROPE), F32)], axis=1)
    stab = jnp.concatenate([jnp.zeros((n, KV_LORA), F32), ss * qs,
                            jnp.zeros((n, QHEAD_W - KV_LORA - QK_ROPE), F32)], axis=1)
    zpad = jnp.zeros((n, LANES - QK_ROPE), F32)
    return ctab, stab, jnp.concatenate([cc, zpad], axis=1), jnp.concatenate([ss, zpad], axis=1)


def _swap_halves(w):
    half = QK_ROPE // 2
    return jnp.concatenate([-w[..., half:], w[..., :half]], axis=-1)


def kernel(x_prompt, x_sample, cache_mla_latent, cache_mla_krope, cache_fox_k, cache_fox_v,
           cache_fox_logf, page_table, c_prompt, c_sample, w_ada, b_ada, g_pre_mix, g_post_mix,
           g_pre_ffn, g_post_ffn, w_in, b_forget, g_q_lora, g_kv_lora, w_uq, w_uk, w_uv, w_o,
           w_group_router, b_group_router, w_expert_router, b_expert_router, w_gate, w_up, w_down):
    assert w_ada.shape[0] == 1, "single-layer trunk"
    batch, seq, d = x_prompt.shape
    dec_batch, n_new, _ = x_sample.shape
    n_pages = page_table.shape[1]
    page = cache_mla_latent.shape[2]
    past_len = n_pages * page
    tp = batch * seq
    ts = dec_batch * n_new

    w_in0 = w_in[0]
    s0, s1, s2, s3, s4, s5 = 256, 384, 416, 928, 1440, 1952
    w_kr = w_in0[:, s1:s2]
    z = lambda n: jnp.zeros((d, n), F32)
    w_all = jnp.concatenate([
        w_in0[:, 0:s0],
        w_in0[:, s0:s1], w_kr, z(96),
        _swap_halves(w_kr), z(96),
        w_in0[:, s2:s3], w_in0[:, s3:s4], w_in0[:, s4:s5],
        w_in0[:, s5:], z(120)], axis=1)
    w_uq0 = w_uq[0]
    w_qabs = _fold(jnp.transpose(w_uq0[:, :, :QK_NOPE], (1, 0, 2)),
                   jnp.transpose(w_uk[0], (0, 2, 1)))
    w_qr = jnp.transpose(w_uq0[:, :, QK_NOPE:], (1, 0, 2))
    zq = lambda n: jnp.zeros((MLA_HEADS, Q_LORA, n), F32)
    w_a = jnp.concatenate([w_qabs, w_qr, zq(96)], axis=2)
    w_b = jnp.concatenate([zq(128), _swap_halves(w_qr), zq(96)], axis=2)
    w_a = jnp.transpose(w_a, (1, 0, 2)).reshape(Q_LORA, MLA_HEADS * QHEAD_W)
    w_b = jnp.transpose(w_b, (1, 0, 2)).reshape(Q_LORA, MLA_HEADS * QHEAD_W)
    w_o0 = w_o[0]
    w_ov = _fold(w_uv[0], w_o0[:MLA_HEADS * V_HEAD].reshape(MLA_HEADS, V_HEAD, d))
    w_ov = w_ov.reshape(MLA_HEADS * KV_LORA, d)
    w_of = w_o0[MLA_HEADS * V_HEAD:]
    w_r = jnp.concatenate([w_group_router[0], w_expert_router[0].reshape(d, N_EXPERTS),
                           jnp.zeros((d, LANES - N_GROUPS - N_EXPERTS), F32)], axis=1)
    b_r = jnp.concatenate([b_group_router[0], b_expert_router[0].reshape(N_EXPERTS),
                           jnp.zeros((LANES - N_GROUPS - N_EXPERTS,), F32)])[None, :]
    bfg = jnp.concatenate([b_forget[0], jnp.zeros((LANES - FOX_HEADS,), F32)])[None, :]
    w_all_hl, w_a_hl, w_b_hl = _hl(w_all), _hl(w_a), _hl(w_b)
    w_ov_hl, w_of_hl, w_r_hl = _hl(w_ov), _hl(w_of), _hl(w_r)
    wg = w_gate[0].reshape(N_EXPERTS, d, D_EXPERT)
    wu = w_up[0].reshape(N_EXPERTS, d, D_EXPERT)
    wd = w_down[0].reshape(N_EXPERTS, D_EXPERT, d)

    c_all = jnp.concatenate([c_prompt, c_sample], axis=0)
    c_all = jnp.pad(c_all, ((0, -c_all.shape[0] % 64), (0, 0)))
    mod = _ada(c_all, w_ada[0], b_ada)
    mod_p = mod[:batch].reshape(batch, 1, 6 * d)
    mod_s = jnp.repeat(mod[batch:batch + dec_batch], n_new, axis=0)

    tm = 512
    xp2 = x_prompt.reshape(tp, d)
    tabs_p = _rope_tables(jnp.arange(seq))
    (q_p, kmla_p, ckv_p, kr_p, fq_p, fk_p, fk16_p, fv_p, fv16_p, lf_p, cum_p) = _proj_prompt(
        xp2, mod_p, g_pre_mix, g_q_lora, g_kv_lora, bfg, tabs_p,
        w_all.astype(BF16), w_a.astype(BF16), w_b.astype(BF16), seq=seq, tm=tm)
    tk = 512
    o_lat_p = _mla_prompt(q_p, kmla_p, batch=batch, seq=seq, tq=512, tk=tk)
    ck4 = cum_p.reshape(batch, FOX_HEADS, seq // tk, tk)
    o_fox_p = _fox_prompt(fq_p, fk16_p, fv16_p, ck4, batch=batch, seq=seq, tq=512, tk=tk)
    x1_p, h2_p, rt_p, cnt_p = _post_prompt(xp2, mod_p, o_lat_p, o_fox_p,
                                           w_ov.astype(BF16), w_of.astype(BF16),
                                           g_post_mix, g_pre_ffn, w_r_hl, b_r, seq=seq, tm=tm)

    tme = 256
    n_asg = 2 * tp
    n_rows = n_asg + N_EXPERTS * tme
    cnt = cnt_p[0, :N_EXPERTS].astype(jnp.int32)
    pend = jnp.cumsum(((cnt + tme - 1) // tme) * tme)
    offs = pend - ((cnt + tme - 1) // tme) * tme
    tile_start = jnp.arange(n_rows // tme, dtype=jnp.int32) * tme
    tile_e = jnp.sum(tile_start[:, None] >= pend[None, :], axis=1, dtype=jnp.int32)
    tile_e = jnp.minimum(tile_e, N_EXPERTS - 1)
    n_used = (pend[-1] // tme).astype(jnp.int32)[None]
    e_ids = rt_p[:, 0:2]
    hit = e_ids[:, :, None] == jnp.arange(N_EXPERTS, dtype=F32)[None, None, :]
    offs_tok = jnp.sum(jnp.where(hit, offs[None, None, :], 0), axis=-1, dtype=jnp.int32)
    pos = (offs_tok + rt_p[:, 4:6].astype(jnp.int32)).reshape(n_asg)
    xs = _dispatch(pos, h2_p, n_rows, tm=256)
    ys = _gmm(tile_e, n_used, xs, wg, wu, wd, tm=tme)
    y_prompt = _combine(pos, x1_p, mod_p, rt_p, g_post_ffn, ys, seq=seq, tm=256)

    xs2 = x_sample.reshape(ts, d)
    pos_s = jnp.tile(past_len + jnp.arange(n_new), dec_batch)
    tabs_s = _rope_tables(pos_s)
    q_s, ckv_s, krp_s, fq_s, fk_s, fv_s, lfp_s = _proj_sample(
        xs2, mod_s, g_pre_mix, g_q_lora, g_kv_lora, bfg, tabs_s, w_all_hl, w_a_hl, w_b_hl)
    q4 = q_s.reshape(dec_batch, n_new, MLA_HEADS, QHEAD_W)
    nrow = n_new * MLA_HEADS
    qa = q4[..., :KV_LORA].reshape(dec_batch, nrow, KV_LORA)
    qr = q4[..., KV_LORA:].reshape(dec_batch, nrow, QHEAD_W - KV_LORA)
    head_mask = (jnp.arange(FOX_W)[None, :] // FOX_HEAD_DIM == jnp.arange(FOX_HEADS)[:, None])
    qf = (fq_s.reshape(dec_batch, n_new, 1, FOX_W) * head_mask[None, None].astype(F32))
    qf = qf.reshape(dec_batch, nrow, FOX_W)
    padk = page - n_new
    latn = jnp.pad(ckv_s.reshape(dec_batch, n_new, KV_LORA), ((0, 0), (0, padk), (0, 0)))
    krn = jnp.pad(jnp.transpose(krp_s[:, :QK_ROPE].reshape(dec_batch, n_new, QK_ROPE), (0, 2, 1)),
                  ((0, 0), (0, 0), (0, padk)))
    fkn = jnp.pad(jnp.transpose(fk_s.reshape(dec_batch, n_new, FOX_W), (0, 2, 1)),
                  ((0, 0), (0, 0), (0, padk)))
    fvn = jnp.pad(jnp.transpose(fv_s.reshape(dec_batch, n_new, FOX_W), (0, 2, 1)),
                  ((0, 0), (0, 0), (0, padk)))
    lfn = jnp.pad(jnp.transpose(lfp_s[:, :FOX_HEADS].reshape(dec_batch, n_new, FOX_HEADS),
                                (0, 2, 1)), ((0, 0), (0, 0), (0, padk)))
    c_lat = cache_mla_latent
    c_kr = jnp.transpose(cache_mla_krope, (0, 1, 3, 2))
    c_fk = jnp.transpose(cache_fox_k, (0, 1, 3, 4, 2))
    c_fv = jnp.transpose(cache_fox_v, (0, 1, 3, 4, 2))
    c_lf = jnp.transpose(cache_fox_logf, (0, 1, 3, 2))
    om, oft = _sample_attn(page_table, qa, qr, qf, latn, krn, fkn, fvn, lfn,
                           c_lat, c_kr, c_fk, c_fv, c_lf, pages=16)
    o_lat_s = om.reshape(dec_batch, n_new, MLA_HEADS * KV_LORA).reshape(ts, MLA_HEADS * KV_LORA)
    oft4 = oft[:, :, :nrow].reshape(dec_batch, FOX_HEADS, FOX_HEAD_DIM, n_new, FOX_HEADS)
    o_fox_s = jnp.einsum('bhdqh->bqhd', oft4).reshape(ts, FOX_W)
    x1_s, h2_s, rt_s = _post_sample(xs2, mod_s, o_lat_s, o_fox_s, w_ov_hl, w_of_hl,
                                    g_post_mix, g_pre_ffn, w_r_hl, b_r)
    y_sample = _moe_sample(h2_s, rt_s, x1_s, mod_s, g_post_ffn, wg, wu, wd)

    return (y_prompt.reshape(batch, seq, d), y_sample.reshape(dec_batch, n_new, d),
            ckv_p.reshape(batch, 1, seq, KV_LORA),
            jnp.transpose(kr_p.reshape(batch, 1, QK_ROPE, seq), (0, 1, 3, 2)),
            jnp.transpose(fk_p.reshape(batch, 1, FOX_HEADS, FOX_HEAD_DIM, seq), (0, 1, 4, 2, 3)),
            jnp.transpose(fv_p.reshape(batch, 1, FOX_HEADS, FOX_HEAD_DIM, seq), (0, 1, 4, 2, 3)),
            jnp.transpose(lf_p.reshape(batch, 1, FOX_HEADS, seq), (0, 1, 3, 2)),
            ckv_s.reshape(dec_batch, 1, n_new, KV_LORA),
            krp_s[:, :QK_ROPE].reshape(dec_batch, 1, n_new, QK_ROPE),
            fk_s.reshape(dec_batch, 1, n_new, FOX_HEADS, FOX_HEAD_DIM),
            fv_s.reshape(dec_batch, 1, n_new, FOX_HEADS, FOX_HEAD_DIM),
            lfp_s[:, :FOX_HEADS].reshape(dec_batch, 1, n_new, FOX_HEADS))
```

```python
import functools

import jax
import jax.numpy as jnp
from jax import lax
from jax.experimental import pallas as pl
from jax.experimental.pallas import tpu as pltpu

F32 = jnp.float32
BF16 = jnp.bfloat16

D_MODEL = 1024
MLA_HEADS = 8
QK_NOPE = 64
QK_ROPE = 32
V_HEAD = 64
Q_LORA = 256
KV_LORA = 128
FOX_HEADS = 8
FOX_HEAD_DIM = 64
FOX_W = FOX_HEADS * FOX_HEAD_DIM
N_GROUPS = 4
EXPERTS_PER_GROUP = 8
N_EXPERTS = N_GROUPS * EXPERTS_PER_GROUP
D_EXPERT = 256
ROPE_THETA = 10000.0
EPS = 1e-6
NEG = -1e30
LOG2E = 1.4426950408889634
MLA_SCALE = (QK_NOPE + QK_ROPE) ** -0.5
FOX_SCALE = FOX_HEAD_DIM ** -0.5

LANES = 128
QHEAD_W = 256
VMEM_LIMIT = 56 * 1024 * 1024


def _split(x):
    hi = x.astype(BF16)
    lo = (x - hi.astype(F32)).astype(BF16)
    return hi, lo


def _mm(a, b):
    return jnp.dot(a, b, preferred_element_type=F32)


def _mm_nt(a, b):
    return lax.dot_general(a, b, (((1,), (1,)), ((), ())), preferred_element_type=F32)


def _mm3(a, b_hi, b_lo):
    a_hi, a_lo = _split(a)
    m = a.shape[0]
    top = _mm(jnp.concatenate([a_hi, a_lo], axis=0), b_hi)
    return top[:m] + top[m:] + _mm(a_hi, b_lo)


def _rms(x, g):
    return x * lax.rsqrt(jnp.mean(x * x, axis=-1, keepdims=True) + EPS) * g


def _silu(x):
    return x / (1.0 + jnp.exp(-x))


def _log_sigmoid(x):
    return jnp.minimum(x, 0.0) - jnp.log(1.0 + jnp.exp(-jnp.abs(x)))


def _cparams(sem, vmem=None):
    return pltpu.CompilerParams(dimension_semantics=sem, vmem_limit_bytes=vmem)


def _full(shape):
    nd = len(shape)
    return pl.BlockSpec(shape, lambda *_: (0,) * nd)


def _ada_kernel(c_ref, w_ref, b_ref, o_ref):
    a = _silu(c_ref[...])
    w_hi, w_lo = _split(w_ref[...])
    o_ref[...] = _mm3(a, w_hi, w_lo) + b_ref[...]


def _ada(c_all, w_ada, b_ada):
    n, d = c_all.shape
    cols = w_ada.shape[1]
    tn = 1024
    return pl.pallas_call(
        _ada_kernel,
        grid=(cols // tn,),
        in_specs=[_full((n, d)),
                  pl.BlockSpec((d, tn), lambda j: (0, j)),
                  pl.BlockSpec((1, tn), lambda j: (0, j))],
        out_specs=pl.BlockSpec((n, tn), lambda j: (0, j)),
        out_shape=jax.ShapeDtypeStruct((n, cols), F32),
        compiler_params=_cparams(("parallel",)),
        name="ada_mod",
    )(c_all, w_ada, b_ada)


def _fold_kernel(a_ref, b_ref, o_ref):
    b_hi, b_lo = _split(b_ref[0])
    o_ref[0] = _mm3(a_ref[0], b_hi, b_lo)


def _fold(a, b):
    h, m, k = a.shape
    n = b.shape[2]
    return pl.pallas_call(
        _fold_kernel,
        grid=(h,),
        in_specs=[pl.BlockSpec((1, m, k), lambda i: (i, 0, 0)),
                  pl.BlockSpec((1, k, n), lambda i: (i, 0, 0))],
        out_specs=pl.BlockSpec((1, m, n), lambda i: (i, 0, 0)),
        out_shape=jax.ShapeDtypeStruct((h, m, n), F32),
        compiler_params=_cparams(("parallel",)),
        name="fold_weights",
    )(a, b)


def _proj_common(x, mod, g_pre, g_q, g_kv, bfg, tabs, mm_all, mm_qa, mm_qb):
    ctab, stab, kct, kst = tabs
    shift = mod[:, 0:D_MODEL]
    scale = mod[:, D_MODEL:2 * D_MODEL]
    h = _rms(x, g_pre) * (1.0 + scale) + shift
    z = mm_all(h)
    cq = _rms(z[:, 0:Q_LORA], g_q)
    nh = MLA_HEADS
    q = (mm_qa(cq) * jnp.concatenate([ctab] * nh, axis=1)
         + mm_qb(cq) * jnp.concatenate([stab] * nh, axis=1))
    c_kv = _rms(z[:, 256:384], g_kv)
    krp = z[:, 384:512] * kct + z[:, 512:640] * kst
    fq = z[:, 640:1152] * (FOX_SCALE * LOG2E)
    fk = z[:, 1152:1664]
    fv = z[:, 1664:2176]
    logf = _log_sigmoid(z[:, 2176:2304] + bfg)
    return q, c_kv, krp, fq, fk, fv, logf


def _proj_prompt_kernel(x_ref, mod_ref, gpre_ref, gq_ref, gkv_ref, bfg_ref,
                        ct_ref, st_ref, kct_ref, kst_ref,
                        wall_ref, wa_ref, wb_ref,
                        q_ref, kmla_ref, ckv_ref, kr_ref, fq_ref, fk_ref, fk16_ref,
                        fv_ref, fv16_ref, lf_ref, cum_ref, carry_ref, *, tiles_per_batch):
    i = pl.program_id(0)
    tm = x_ref.shape[0]
    q, c_kv, krp, fq, fk, fv, logf = _proj_common(
        x_ref[...], mod_ref[0], gpre_ref[...], gq_ref[...], gkv_ref[...], bfg_ref[...],
        (ct_ref[...], st_ref[...], kct_ref[...], kst_ref[...]),
        lambda h: _mm(h.astype(BF16), wall_ref[...]),
        lambda c: _mm(c.astype(BF16), wa_ref[...]),
        lambda c: _mm(c.astype(BF16), wb_ref[...]))
    q_ref[...] = q.astype(BF16)
    ckv_ref[...] = c_kv
    kr_ref[0] = jnp.transpose(krp)[0:QK_ROPE, :]
    kmla_ref[...] = jnp.concatenate([c_kv, krp], axis=1).astype(BF16)
    fq_ref[...] = fq.astype(BF16)
    fk_ref[0] = jnp.transpose(fk)
    fk16_ref[...] = fk.astype(BF16)
    fv_ref[0] = jnp.transpose(fv)
    fv16_ref[...] = fv.astype(BF16)
    lf_ref[0] = jnp.transpose(logf)[0:FOX_HEADS, :]

    @pl.when(i % tiles_per_batch == 0)
    def _():
        carry_ref[...] = jnp.zeros_like(carry_ref)

    row = lax.broadcasted_iota(jnp.int32, (tm, tm), 0)
    col = lax.broadcasted_iota(jnp.int32, (tm, tm), 1)
    tri = (row >= col).astype(BF16)
    p1 = logf.astype(BF16)
    r1 = logf - p1.astype(F32)
    p2 = r1.astype(BF16)
    p3 = (r1 - p2.astype(F32)).astype(BF16)
    cum = _mm(tri, p1) + _mm(tri, p2) + _mm(tri, p3) + carry_ref[0:1, :]
    carry_ref[0:1, :] = cum[tm - 1:tm, :]
    cum_ref[0] = jnp.transpose(cum * LOG2E)[0:FOX_HEADS, :]


def _proj_prompt(x2d, mod3, gpre, gq, gkv, bfg, tabs, wall, wa, wb, *, seq, tm):
    t = x2d.shape[0]
    tpb = seq // tm
    row = lambda w: pl.BlockSpec((tm, w), lambda i: (i, 0))
    pos = lambda w: pl.BlockSpec((tm, w), lambda i: (i % tpb, 0))
    nb = t // seq
    keym = lambda w: pl.BlockSpec((1, w, tm), lambda i: (i // tpb, 0, i % tpb))
    out_shapes = [
        jax.ShapeDtypeStruct((t, MLA_HEADS * QHEAD_W), BF16),
        jax.ShapeDtypeStruct((t, QHEAD_W), BF16),
        jax.ShapeDtypeStruct((t, KV_LORA), F32),
        jax.ShapeDtypeStruct((nb, QK_ROPE, seq), F32),
        jax.ShapeDtypeStruct((t, FOX_W), BF16),
        jax.ShapeDtypeStruct((nb, FOX_W, seq), F32),
        jax.ShapeDtypeStruct((t, FOX_W), BF16),
        jax.ShapeDtypeStruct((nb, FOX_W, seq), F32),
        jax.ShapeDtypeStruct((t, FOX_W), BF16),
        jax.ShapeDtypeStruct((nb, FOX_HEADS, seq), F32),
        jax.ShapeDtypeStruct((nb, FOX_HEADS, seq), F32),
    ]
    out_specs = [row(MLA_HEADS * QHEAD_W), row(QHEAD_W), row(KV_LORA), keym(QK_ROPE),
                 row(FOX_W), keym(FOX_W), row(FOX_W), keym(FOX_W), row(FOX_W),
                 keym(FOX_HEADS), keym(FOX_HEADS)]
    return pl.pallas_call(
        functools.partial(_proj_prompt_kernel, tiles_per_batch=tpb),
        grid=(t // tm,),
        in_specs=[row(D_MODEL),
                  pl.BlockSpec((1, 1, 6 * D_MODEL), lambda i: (i // tpb, 0, 0)),
                  _full(gpre.shape), _full(gq.shape), _full(gkv.shape), _full(bfg.shape),
                  pos(QHEAD_W), pos(QHEAD_W), pos(LANES), pos(LANES),
                  _full(wall.shape), _full(wa.shape), _full(wb.shape)],
        out_specs=out_specs,
        out_shape=out_shapes,
        scratch_shapes=[pltpu.VMEM((8, LANES), F32)],
        compiler_params=_cparams(("arbitrary",), VMEM_LIMIT),
        name="proj_prompt",
    )(x2d, mod3, gpre, gq, gkv, bfg, *tabs, wall, wa, wb)


def _proj_sample_kernel(x_ref, mod_ref, gpre_ref, gq_ref, gkv_ref, bfg_ref,
                        ct_ref, st_ref, kct_ref, kst_ref,
                        wall_hi, wall_lo, wa_hi, wa_lo, wb_hi, wb_lo,
                        q_ref, ckv_ref, krp_ref, fq_ref, fk_ref, fv_ref, lf_ref):
    q, c_kv, krp, fq, fk, fv, logf = _proj_common(
        x_ref[...], mod_ref[...], gpre_ref[...], gq_ref[...], gkv_ref[...], bfg_ref[...],
        (ct_ref[...], st_ref[...], kct_ref[...], kst_ref[...]),
        lambda h: _mm3(h, wall_hi[...], wall_lo[...]),
        lambda c: _mm3(c, wa_hi[...], wa_lo[...]),
        lambda c: _mm3(c, wb_hi[...], wb_lo[...]))
    q_ref[...] = q
    ckv_ref[...] = c_kv
    krp_ref[...] = krp
    fq_ref[...] = fq
    fk_ref[...] = fk
    fv_ref[...] = fv
    lf_ref[...] = logf


def _proj_sample(x2d, mod_tok, gpre, gq, gkv, bfg, tabs, wall, wa, wb):
    t = x2d.shape[0]
    ins = [x2d, mod_tok, gpre, gq, gkv, bfg, *tabs, *wall, *wa, *wb]
    shapes = [(t, MLA_HEADS * QHEAD_W), (t, KV_LORA), (t, LANES), (t, FOX_W), (t, FOX_W),
              (t, FOX_W), (t, LANES)]
    return pl.pallas_call(
        _proj_sample_kernel,
        grid=(1,),
        in_specs=[_full(a.shape) for a in ins],
        out_specs=[_full(s) for s in shapes],
        out_shape=[jax.ShapeDtypeStruct(s, F32) for s in shapes],
        compiler_params=_cparams(("arbitrary",), VMEM_LIMIT),
        name="proj_sample",
    )(*ins)


def _flash_update(s, v1, m_ref, acc_ref):
    m_prev = m_ref[...]
    m_new = jnp.maximum(m_prev, jnp.max(s, axis=1, keepdims=True))
    alpha = jnp.exp2(m_prev - m_new)
    p = jnp.exp2(s - jnp.concatenate([m_new] * (s.shape[1] // LANES), axis=1))
    acc_ref[...] = (jnp.concatenate([alpha, alpha], axis=1) * acc_ref[...]
                    + _mm(p.astype(BF16), v1))
    m_ref[...] = m_new


def _with_ones(v):
    return jnp.concatenate([v, jnp.ones(v.shape, v.dtype)], axis=1)


def _causal_bounds(i, tq, tk):
    n_full = (i * tq + 1) // tk
    n_all = (i * tq + tq + tk - 1) // tk
    return n_full, n_all


def _mla_prompt_kernel(q_ref, k_ref, o_ref, m_ref, acc_ref, *, tq, tk, chunk):
    i = pl.program_id(1)
    nh = MLA_HEADS
    qs = jnp.concatenate([q_ref[:, h * QHEAD_W:(h + 1) * QHEAD_W] for h in range(nh)], axis=0)
    m_ref[...] = jnp.full(m_ref.shape, NEG, F32)
    acc_ref[...] = jnp.zeros(acc_ref.shape, F32)

    def step(j, masked):
        kb = k_ref[pl.ds(pl.multiple_of(j * tk, tk), tk), :]
        vb = _with_ones(kb[:, 0:KV_LORA])
        s_all = _mm_nt(qs, kb)
        for c in range(nh * tq // chunk):
            rows = slice(c * chunk, (c + 1) * chunk)
            s = s_all[rows]
            if masked:
                row = lax.broadcasted_iota(jnp.int32, s.shape, 0)
                col = lax.broadcasted_iota(jnp.int32, s.shape, 1)
                s = jnp.where(j * tk + col <= i * tq + ((c * chunk + row) & (tq - 1)), s, NEG)
            _flash_update(s, vb, m_ref.at[rows], acc_ref.at[rows])

    n_full, n_all = _causal_bounds(i, tq, tk)
    lax.fori_loop(0, n_full, lambda j, c: (step(j, False), c)[1], 0)
    lax.fori_loop(n_full, n_all, lambda j, c: (step(j, True), c)[1], 0)
    o = acc_ref[:, 0:KV_LORA] / acc_ref[:, KV_LORA:2 * KV_LORA]
    o_ref[...] = jnp.concatenate([o[h * tq:(h + 1) * tq] for h in range(nh)], axis=1).astype(BF16)


def _mla_prompt(q, kmla, *, batch, seq, tq, tk):
    t = q.shape[0]
    nq = seq // tq
    rows = MLA_HEADS * tq
    return pl.pallas_call(
        functools.partial(_mla_prompt_kernel, tq=tq, tk=tk, chunk=128),
        grid=(batch, nq),
        in_specs=[pl.BlockSpec((tq, MLA_HEADS * QHEAD_W), lambda b, i: (b * nq + i, 0)),
                  pl.BlockSpec((seq, QHEAD_W), lambda b, i: (b, 0))],
        out_specs=pl.BlockSpec((tq, MLA_HEADS * KV_LORA), lambda b, i: (b * nq + i, 0)),
        out_shape=jax.ShapeDtypeStruct((t, MLA_HEADS * KV_LORA), BF16),
        scratch_shapes=[pltpu.VMEM((rows, LANES), F32), pltpu.VMEM((rows, 2 * KV_LORA), F32)],
        compiler_params=_cparams(("parallel", "parallel"), VMEM_LIMIT),
        name="mla_prompt",
    )(q, kmla)


def _fox_prompt_kernel(q_ref, k_ref, v_ref, ck_ref, o_ref, m_ref, acc_ref, *, tq, tk, chunk):
    i = pl.program_id(1)
    lane = lax.broadcasted_iota(jnp.int32, (tq, LANES), 1)
    n_full, n_all = _causal_bounds(i, tq, tk)
    lo_half = lane < FOX_HEAD_DIM
    m_ref[...] = jnp.full(m_ref.shape, NEG, F32)
    acc_ref[...] = jnp.zeros(acc_ref.shape, F32)

    def step(j, masked):
        keys = pl.ds(pl.multiple_of(j * tk, tk), tk)

        def scores(h):
            cols = slice((h // 2) * LANES, (h // 2 + 1) * LANES)
            qp = q_ref[:, cols]
            sel = lo_half if h % 2 == 0 else jnp.logical_not(lo_half)
            qm = jnp.where(sel, qp, jnp.zeros_like(qp))
            return _mm_nt(qm, k_ref[keys, cols])

        def values(h, s_all):
            cols = slice((h // 2) * LANES, (h // 2 + 1) * LANES)
            vb = _with_ones(v_ref[keys, cols])
            ck = ck_ref[0, h, pl.ds(j, 1), :]
            for c in range(tq // chunk):
                rows = slice(c * chunk, (c + 1) * chunk)
                s = s_all[rows] - ck
                if masked:
                    row = lax.broadcasted_iota(jnp.int32, s.shape, 0)
                    col = lax.broadcasted_iota(jnp.int32, s.shape, 1)
                    s = jnp.where(j * tk + col <= i * tq + c * chunk + row, s, NEG)
                _flash_update(s, vb, m_ref.at[h, rows], acc_ref.at[h, rows])

        s_next = scores(0)
        for h in range(FOX_HEADS):
            s_cur = s_next
            if h + 1 < FOX_HEADS:
                s_next = scores(h + 1)
            values(h, s_cur)

    lax.fori_loop(0, n_full, lambda j, c: (step(j, False), c)[1], 0)
    lax.fori_loop(n_full, n_all, lambda j, c: (step(j, True), c)[1], 0)
    for pair in range(FOX_HEADS // 2):
        o0 = acc_ref[2 * pair, :, 0:LANES] / acc_ref[2 * pair, :, LANES:2 * LANES]
        o1 = acc_ref[2 * pair + 1, :, 0:LANES] / acc_ref[2 * pair + 1, :, LANES:2 * LANES]
        o_ref[:, pair * LANES:(pair + 1) * LANES] = jnp.where(lo_half, o0, o1).astype(BF16)


def _fox_prompt(fq, fk, fv, ck4, *, batch, seq, tq, tk):
    t = fq.shape[0]
    nq = seq // tq
    nk = seq // tk
    return pl.pallas_call(
        functools.partial(_fox_prompt_kernel, tq=tq, tk=tk, chunk=128),
        grid=(batch, nq),
        in_specs=[pl.BlockSpec((tq, FOX_W), lambda b, i: (b * nq + i, 0)),
                  pl.BlockSpec((seq, FOX_W), lambda b, i: (b, 0)),
                  pl.BlockSpec((seq, FOX_W), lambda b, i: (b, 0)),
                  pl.BlockSpec((1, FOX_HEADS, nk, tk), lambda b, i: (b, 0, 0, 0))],
        out_specs=pl.BlockSpec((tq, FOX_W), lambda b, i: (b * nq + i, 0)),
        out_shape=jax.ShapeDtypeStruct((t, FOX_W), BF16),
        scratch_shapes=[pltpu.VMEM((FOX_HEADS, tq, LANES), F32),
                        pltpu.VMEM((FOX_HEADS, tq, 2 * LANES), F32)],
        compiler_params=_cparams(("parallel", "parallel"), VMEM_LIMIT),
        name="fox_prompt",
    )(fq, fk, fv, ck4)


def _route(r):
    lane = lax.broadcasted_iota(jnp.int32, r.shape, 1)
    lanef = lane.astype(F32)
    big = jnp.float32(1e9)
    gmask = lane < N_GROUPS
    gl = jnp.where(gmask, r, -jnp.inf)
    gm = jnp.max(gl, axis=1, keepdims=True)
    gs = jnp.sum(jnp.where(gmask, jnp.exp(r - gm), 0.0), axis=1, keepdims=True)
    g_w = 1.0 / gs
    g_idx = jnp.min(jnp.where(gl == gm, lanef, big), axis=1, keepdims=True)
    egrp = ((lane - N_GROUPS) >> 3).astype(F32)
    emask = (lane >= N_GROUPS) & (lane < N_GROUPS + N_EXPERTS) & (egrp == g_idx)
    el = jnp.where(emask, r, -jnp.inf)
    em = jnp.max(el, axis=1, keepdims=True)
    ee = jnp.where(emask, jnp.exp(r - em), 0.0)
    p = ee / jnp.sum(ee, axis=1, keepdims=True)
    pm = jnp.where(emask, p, -1.0)
    p1 = jnp.max(pm, axis=1, keepdims=True)
    i1 = jnp.min(jnp.where(pm == p1, lanef, big), axis=1, keepdims=True)
    pm2 = jnp.where(lanef == i1, -1.0, pm)
    p2 = jnp.max(pm2, axis=1, keepdims=True)
    i2 = jnp.min(jnp.where(pm2 == p2, lanef, big), axis=1, keepdims=True)
    den = p1 + p2
    c1 = g_w * (p1 / den)
    c2 = g_w * (p2 / den)
    out = jnp.where(lane == 0, i1 - N_GROUPS,
                    jnp.where(lane == 1, i2 - N_GROUPS,
                              jnp.where(lane == 2, c1, jnp.where(lane == 3, c2, 0.0))))
    return out


def _post_common(x, mod, mixed, gpost, gffn, wr_hi, wr_lo, br):
    gate_m = mod[:, 2 * D_MODEL:3 * D_MODEL]
    shift_f = mod[:, 3 * D_MODEL:4 * D_MODEL]
    scale_f = mod[:, 4 * D_MODEL:5 * D_MODEL]
    x1 = x + gate_m * _rms(mixed, gpost)
    h2 = _rms(x1, gffn) * (1.0 + scale_f) + shift_f
    r = _mm3(h2, wr_hi, wr_lo) + br
    return x1, h2, _route(r)


def _post_prompt_kernel(x_ref, mod_ref, ol_ref, of_ref, wov_ref, wof_ref, gpost_ref, gffn_ref,
                        wrh_ref, wrl_ref, br_ref, x1_ref, h2_ref, rt_ref, cnt_ref, carry_ref):
    i = pl.program_id(0)
    tm = x_ref.shape[0]
    mixed = _mm(ol_ref[...], wov_ref[...]) + _mm(of_ref[...], wof_ref[...])
    x1, h2, rt = _post_common(x_ref[...], mod_ref[0], mixed, gpost_ref[...], gffn_ref[...],
                              wrh_ref[...], wrl_ref[...], br_ref[...])
    x1_ref[...] = x1
    h2_ref[...] = h2

    @pl.when(i == 0)
    def _():
        carry_ref[...] = jnp.zeros_like(carry_ref)

    lane = lax.broadcasted_iota(jnp.int32, rt.shape, 1)
    lanef = lane.astype(F32)
    oh1 = (lanef == rt[:, 0:1]).astype(F32)
    oh2 = (lanef == rt[:, 1:2]).astype(F32)
    row = lax.broadcasted_iota(jnp.int32, (tm, tm), 0)
    col = lax.broadcasted_iota(jnp.int32, (tm, tm), 1)
    before = _mm((row > col).astype(BF16), (oh1 + oh2).astype(BF16)) + carry_ref[0:1, :]
    rank1 = jnp.sum(oh1 * before, axis=1, keepdims=True)
    rank2 = jnp.sum(oh2 * (before + oh1), axis=1, keepdims=True)
    total = before[tm - 1:tm, :] + oh1[tm - 1:tm, :] + oh2[tm - 1:tm, :]
    carry_ref[0:1, :] = total
    cnt_ref[...] = jnp.broadcast_to(total, cnt_ref.shape)
    rt_ref[...] = jnp.where(lane == 4, rank1, jnp.where(lane == 5, rank2, rt))


def _post_prompt(x2d, mod3, o_lat, o_fox, wov, wof, gpost, gffn, wr, br, *, seq, tm):
    t = x2d.shape[0]
    tpb = seq // tm
    row = lambda w: pl.BlockSpec((tm, w), lambda i: (i, 0))
    return pl.pallas_call(
        _post_prompt_kernel,
        grid=(t // tm,),
        in_specs=[row(D_MODEL),
                  pl.BlockSpec((1, 1, 6 * D_MODEL), lambda i: (i // tpb, 0, 0)),
                  row(MLA_HEADS * KV_LORA), row(FOX_W),
                  _full(wov.shape), _full(wof.shape), _full(gpost.shape), _full(gffn.shape),
                  _full(wr[0].shape), _full(wr[1].shape), _full(br.shape)],
        out_specs=[row(D_MODEL), row(D_MODEL), row(LANES), _full((8, LANES))],
        out_shape=[jax.ShapeDtypeStruct((t, D_MODEL), F32),
                   jax.ShapeDtypeStruct((t, D_MODEL), F32),
                   jax.ShapeDtypeStruct((t, LANES), F32),
                   jax.ShapeDtypeStruct((8, LANES), F32)],
        scratch_shapes=[pltpu.VMEM((8, LANES), F32)],
        compiler_params=_cparams(("arbitrary",), VMEM_LIMIT),
        name="post_prompt",
    )(x2d, mod3, o_lat, o_fox, wov, wof, gpost, gffn, wr[0], wr[1], br)


def _post_sample_kernel(x_ref, mod_ref, ol_ref, of_ref, wovh_ref, wovl_ref, wofh_ref, wofl_ref,
                        gpost_ref, gffn_ref, wrh_ref, wrl_ref, br_ref, x1_ref, h2_ref, rt_ref):
    mixed = (_mm3(ol_ref[...], wovh_ref[...], wovl_ref[...])
             + _mm3(of_ref[...], wofh_ref[...], wofl_ref[...]))
    x1, h2, rt = _post_common(x_ref[...], mod_ref[...], mixed, gpost_ref[...], gffn_ref[...],
                              wrh_ref[...], wrl_ref[...], br_ref[...])
    x1_ref[...] = x1
    h2_ref[...] = h2
    rt_ref[...] = rt


def _post_sample(x2d, mod_tok, o_lat, o_fox, wov, wof, gpost, gffn, wr, br):
    t = x2d.shape[0]
    ins = [x2d, mod_tok, o_lat, o_fox, *wov, *wof, gpost, gffn, *wr, br]
    shapes = [(t, D_MODEL), (t, D_MODEL), (t, LANES)]
    return pl.pallas_call(
        _post_sample_kernel,
        grid=(1,),
        in_specs=[_full(a.shape) for a in ins],
        out_specs=[_full(s) for s in shapes],
        out_shape=[jax.ShapeDtypeStruct(s, F32) for s in shapes],
        compiler_params=_cparams(("arbitrary",), VMEM_LIMIT),
        name="post_sample",
    )(*ins)


def _dispatch_kernel(pos_ref, h2_ref, zero_ref, xs_ref, sem):
    del zero_ref
    i = pl.program_id(0)
    tm = h2_ref.shape[0]
    base = i * (2 * tm)

    def body(t, c):
        for k in range(2):
            p = pos_ref[base + 2 * t + k]
            pltpu.make_async_copy(h2_ref.at[pl.ds(t, 1)], xs_ref.at[pl.ds(p, 1)], sem).start()
        return c

    lax.fori_loop(0, tm, body, 0, unroll=8)
    for _ in range(2):
        pltpu.make_async_copy(h2_ref, xs_ref.at[pl.ds(0, tm)], sem).wait()


def _dispatch(pos, h2, n_rows, *, tm):
    t = h2.shape[0]
    grid_spec = pltpu.PrefetchScalarGridSpec(
        num_scalar_prefetch=1,
        grid=(t // tm,),
        in_specs=[pl.BlockSpec((tm, h2.shape[1]), lambda i, pos: (i, 0)),
                  pl.BlockSpec(memory_space=pl.ANY)],
        out_specs=pl.BlockSpec(memory_space=pl.ANY),
        scratch_shapes=[pltpu.SemaphoreType.DMA],
    )
    return pl.pallas_call(
        _dispatch_kernel,
        grid_spec=grid_spec,
        out_shape=jax.ShapeDtypeStruct((n_rows, h2.shape[1]), h2.dtype),
        input_output_aliases={2: 0},
        compiler_params=_cparams(("arbitrary",), VMEM_LIMIT),
        name="moe_dispatch",
    )(pos, h2, jnp.zeros((n_rows, h2.shape[1]), h2.dtype))


def _gmm_kernel(te_ref, nu_ref, x_ref, wg_ref, wu_ref, wd_ref, o_ref):
    i = pl.program_id(0)

    @pl.when(i < nu_ref[0])
    def _():
        x = x_ref[...].astype(BF16)
        a = _mm(x, wg_ref[0].astype(BF16))
        u = _mm(x, wu_ref[0].astype(BF16))
        act = (_silu(a) * u).astype(BF16)
        o_ref[...] = _mm(act, wd_ref[0].astype(BF16))

    @pl.when(i >= nu_ref[0])
    def _():
        o_ref[...] = jnp.zeros_like(o_ref)


def _gmm(tile_expert, n_used, xs, wg, wu, wd, *, tm):
    nr = xs.shape[0]
    grid_spec = pltpu.PrefetchScalarGridSpec(
        num_scalar_prefetch=2,
        grid=(nr // tm,),
        in_specs=[pl.BlockSpec((tm, D_MODEL), lambda i, te, nu: (i, 0)),
                  pl.BlockSpec((1, D_MODEL, D_EXPERT), lambda i, te, nu: (te[i], 0, 0)),
                  pl.BlockSpec((1, D_MODEL, D_EXPERT), lambda i, te, nu: (te[i], 0, 0)),
                  pl.BlockSpec((1, D_EXPERT, D_MODEL), lambda i, te, nu: (te[i], 0, 0))],
        out_specs=pl.BlockSpec((tm, D_MODEL), lambda i, te, nu: (i, 0)),
    )
    return pl.pallas_call(
        _gmm_kernel,
        grid_spec=grid_spec,
        out_shape=jax.ShapeDtypeStruct((nr, D_MODEL), F32),
        compiler_params=_cparams(("arbitrary",), VMEM_LIMIT),
        name="moe_gmm",
    )(tile_expert, n_used, xs, wg, wu, wd)


def _combine_kernel(pos_ref, x1_ref, mod_ref, rt_ref, g_ref, ys_ref, o_ref, buf_ref, sem):
    i = pl.program_id(0)
    n = pl.num_programs(0)
    tm = x1_ref.shape[0]

    def issue(tile, slot):
        base = tile * (2 * tm)

        def body(t, c):
            for k in range(2):
                p = pos_ref[base + 2 * t + k]
                pltpu.make_async_copy(ys_ref.at[pl.ds(p, 1)], buf_ref.at[slot, k, pl.ds(t, 1)],
                                      sem.at[slot]).start()
            return c

        lax.fori_loop(0, tm, body, 0, unroll=8)

    @pl.when(i == 0)
    def _():
        issue(0, 0)

    @pl.when(i + 1 < n)
    def _():
        issue(i + 1, (i + 1) % 2)

    slot = i % 2
    for k in range(2):
        pltpu.make_async_copy(ys_ref.at[pl.ds(0, tm)], buf_ref.at[slot, k], sem.at[slot]).wait()
    gate_f = mod_ref[0][:, 5 * D_MODEL:6 * D_MODEL]
    rt = rt_ref[...]
    y = rt[:, 2:3] * buf_ref[slot, 0] + rt[:, 3:4] * buf_ref[slot, 1]
    o_ref[...] = x1_ref[...] + gate_f * _rms(y, g_ref[...])


def _combine(pos, x1, mod3, route, gpost, ys, *, seq, tm):
    t = x1.shape[0]
    tpb = seq // tm
    row = lambda w: pl.BlockSpec((tm, w), lambda i, pos: (i, 0))
    grid_spec = pltpu.PrefetchScalarGridSpec(
        num_scalar_prefetch=1,
        grid=(t // tm,),
        in_specs=[row(D_MODEL),
                  pl.BlockSpec((1, 1, 6 * D_MODEL), lambda i, pos: (i // tpb, 0, 0)),
                  row(LANES),
                  pl.BlockSpec(gpost.shape, lambda i, pos: (0, 0)),
                  pl.BlockSpec(memory_space=pl.ANY)],
        out_specs=row(D_MODEL),
        scratch_shapes=[pltpu.VMEM((2, 2, tm, D_MODEL), F32), pltpu.SemaphoreType.DMA((2,))],
    )
    return pl.pallas_call(
        _combine_kernel,
        grid_spec=grid_spec,
        out_shape=jax.ShapeDtypeStruct((t, D_MODEL), F32),
        compiler_params=_cparams(("arbitrary",), VMEM_LIMIT),
        name="moe_combine",
    )(pos, x1, mod3, route, gpost, ys)


def _moe_sample_kernel(h2_ref, rt_ref, x1_ref, mod_ref, g_ref, wg_ref, wu_ref, wd_ref,
                       o_ref, acc_ref):
    e = pl.program_id(0)

    @pl.when(e == 0)
    def _():
        acc_ref[...] = jnp.zeros_like(acc_ref)

    h2 = h2_ref[...]
    rt = rt_ref[...]
    ef = e.astype(F32)
    cw = (jnp.where(rt[:, 0:1] == ef, rt[:, 2:3], 0.0)
          + jnp.where(rt[:, 1:2] == ef, rt[:, 3:4], 0.0))
    a = _mm3(h2, *_split(wg_ref[0]))
    u = _mm3(h2, *_split(wu_ref[0]))
    act = _silu(a) * u * cw
    acc_ref[...] += _mm3(act, *_split(wd_ref[0]))

    @pl.when(e == pl.num_programs(0) - 1)
    def _():
        gate_f = mod_ref[:, 5 * D_MODEL:6 * D_MODEL]
        o_ref[...] = x1_ref[...] + gate_f * _rms(acc_ref[...], g_ref[...])


def _moe_sample(h2, route, x1, mod_tok, gpost, wg, wu, wd):
    t = h2.shape[0]
    return pl.pallas_call(
        _moe_sample_kernel,
        grid=(N_EXPERTS,),
        in_specs=[_full(h2.shape), _full(route.shape), _full(x1.shape), _full(mod_tok.shape),
                  _full(gpost.shape),
                  pl.BlockSpec((1, D_MODEL, D_EXPERT), lambda e: (e, 0, 0)),
                  pl.BlockSpec((1, D_MODEL, D_EXPERT), lambda e: (e, 0, 0)),
                  pl.BlockSpec((1, D_EXPERT, D_MODEL), lambda e: (e, 0, 0))],
        out_specs=_full((t, D_MODEL)),
        out_shape=jax.ShapeDtypeStruct((t, D_MODEL), F32),
        scratch_shapes=[pltpu.VMEM((t, D_MODEL), F32)],
        compiler_params=_cparams(("arbitrary",), VMEM_LIMIT),
        name="moe_sample",
    )(h2, route, x1, mod_tok, gpost, wg, wu, wd)


def _rev_cumsum_lanes(x):
    lane = lax.broadcasted_iota(jnp.int32, x.shape, 1)
    sh = 1
    while sh < LANES:
        x = x + jnp.where(lane + sh < LANES, pltpu.roll(x, LANES - sh, 1), 0.0)
        sh *= 2
    return x


def _fwd_cumsum_lanes(x):
    lane = lax.broadcasted_iota(jnp.int32, x.shape, 1)
    sh = 1
    while sh < LANES:
        x = x + jnp.where(lane >= sh, pltpu.roll(x, sh, 1), 0.0)
        sh *= 2
    return x


def _seg_update(s, m_ref, l_ref):
    m_prev = m_ref[...]
    m_new = jnp.maximum(m_prev, jnp.max(s, axis=1, keepdims=True))
    alpha = jnp.exp2(m_prev - m_new)
    p = jnp.exp2(s - m_new)
    l_ref[...] = alpha * l_ref[...] + jnp.sum(p, axis=1, keepdims=True)
    m_ref[...] = m_new
    return p, alpha


def _col_to_lanes(col, fill=0.0):
    n = col.shape[0]
    row = lax.broadcasted_iota(jnp.int32, (n, LANES), 0)
    lane = lax.broadcasted_iota(jnp.int32, (n, LANES), 1)
    vec = jnp.sum(jnp.where(row == lane, col, 0.0), axis=0, keepdims=True)
    return jnp.where(lane[0:1, :] < n, vec, fill)


def _sample_attn_kernel(pt_ref, qa_ref, qr_ref, qf_ref, latn_ref, krn_ref, fkn_ref, fvn_ref,
                        lfn_ref, clat_ref, ckr_ref, cfk_ref, cfv_ref, clf_ref, om_ref, of_ref,
                        mm_ref, lm_ref, accm_ref, mf_ref, lfs_ref, accf_ref, carry_ref,
                        lat_buf, kr_buf, fk_buf, fv_buf, lf_buf, sem_a, sem_b, *, pages, n_chunks):
    npg = pages
    b = pl.program_id(0)
    c = pl.program_id(1)
    nrow = qa_ref.shape[1]
    n_steps = pl.num_programs(0) * n_chunks
    step = b * n_chunks + c
    slot = step % 2

    def start(chunk_step, dst_slot, group_a):
        cb = chunk_step // n_chunks
        cc = chunk_step % n_chunks
        for p in range(npg):
            page = pt_ref[cb, (n_chunks - 1 - cc) * npg + p]
            keys = pl.ds(p * LANES, LANES)
            if group_a:
                sem = sem_a.at[dst_slot]
                pltpu.make_async_copy(clat_ref.at[page, 0], lat_buf.at[dst_slot, keys, :],
                                      sem).start()
                pltpu.make_async_copy(ckr_ref.at[page, 0], kr_buf.at[dst_slot, :, keys],
                                      sem).start()
                pltpu.make_async_copy(cfk_ref.at[page, 0], fk_buf.at[dst_slot, :, :, keys],
                                      sem).start()
                pltpu.make_async_copy(clf_ref.at[page, 0], lf_buf.at[dst_slot, :, keys],
                                      sem).start()
            else:
                pltpu.make_async_copy(cfv_ref.at[page, 0], fv_buf.at[dst_slot, :, :, keys],
                                      sem_b.at[dst_slot]).start()

    def wait(dst_slot, group_a):
        bufs = (lat_buf, kr_buf, fk_buf, lf_buf) if group_a else (fv_buf,)
        sem = sem_a.at[dst_slot] if group_a else sem_b.at[dst_slot]
        for buf in bufs:
            pltpu.make_async_copy(buf.at[dst_slot], buf.at[dst_slot], sem).wait()

    nxt = jnp.minimum(step + 1, n_steps - 1)

    @pl.when(step == 0)
    def _():
        start(step, slot, True)
        start(step, slot, False)

    start(nxt, 1 - slot, True)
    start(nxt, 1 - slot, False)
    wait(slot, True)

    @pl.when(c == 0)
    def _():
        mm_ref[...] = jnp.full(mm_ref.shape, NEG, F32)
        mf_ref[...] = jnp.full(mf_ref.shape, NEG, F32)
        lm_ref[...] = jnp.zeros(lm_ref.shape, F32)
        lfs_ref[...] = jnp.zeros(lfs_ref.shape, F32)
        accm_ref[...] = jnp.zeros(accm_ref.shape, F32)
        accf_ref[...] = jnp.zeros(accf_ref.shape, F32)
        carry_ref[...] = jnp.zeros(carry_ref.shape, F32)

    def stack(x):
        hi, lo = _split(x)
        return hi, jnp.concatenate([hi, lo], axis=0)

    qa_hi, qa_cat = stack(qa_ref[0])
    qr_hi, qr_cat = stack(qr_ref[0][:, 0:QK_ROPE])
    qf_hi, qf_cat = stack(qf_ref[0])

    def attend(lat, krt, fkt, get_fvt, bias_f, bias_m, ngrp, hooks=()):
        width = lat.shape[0] // ngrp
        groups = []
        for g in range(ngrp):
            ks = slice(g * width, (g + 1) * width)
            lh, ll = _split(lat[ks, :])
            kh, kl = _split(krt[:, ks])
            fh, fl = _split(fkt[:, ks])
            top = _mm_nt(qa_cat, lh) + _mm(qr_cat, kh)
            s_m = top[:nrow] + top[nrow:] + _mm_nt(qa_hi, ll) + _mm(qr_hi, kl)
            top = _mm(qf_cat, fh)
            s_f = (top[:nrow] + top[nrow:] + _mm(qf_hi, fl)
                   + bias_f[:, g * width:(g + 1) * width])
            if bias_m is not None:
                s_m = s_m + bias_m
            groups.append((s_m, s_f, lh, ll))
            if g == 0 and hooks:
                hooks[0]()
        if hooks:
            hooks[1]()
        fvt = get_fvt()
        for g in range(ngrp):
            s_m, s_f, lh, ll = groups[g]
            ks = slice(g * width, (g + 1) * width)
            p_m, a_m = _seg_update(s_m, mm_ref, lm_ref)
            p_f, a_f = _seg_update(s_f, mf_ref, lfs_ref)
            pm_hi, pm_lo = _split(p_m)
            pf_hi, pf_lo = _split(p_f)
            zrow = jnp.zeros(pm_hi.shape, BF16)
            pm_cat = jnp.concatenate([pm_hi, pm_lo], axis=0)
            pm_hi0 = jnp.concatenate([pm_hi, zrow], axis=0)
            pf_cat = jnp.concatenate([pf_hi, pf_lo, zrow, zrow], axis=0)
            pf_hi0 = jnp.concatenate([pf_hi, zrow, zrow, zrow], axis=0)
            vh, vl = _split(fvt[:, ks])
            upd_m = _mm(pm_cat, lh) + _mm(pm_hi0, ll)
            upd_f = _mm_nt(vh, pf_cat) + _mm_nt(vl, pf_hi0)
            accm_ref[...] = jnp.concatenate([a_m, a_m], axis=0) * accm_ref[...] + upd_m
            a_lanes = _col_to_lanes(a_f)
            accf_ref[...] = (a_lanes + pltpu.roll(a_lanes, nrow, 1)) * accf_ref[...] + upd_f
            if g == 0 and hooks:
                hooks[2]()

    sufs = [None] * npg
    carry = carry_ref[...]
    for p in range(npg - 1, -1, -1):
        lf = lf_buf[slot, :, p * LANES:(p + 1) * LANES]
        inc = _rev_cumsum_lanes(lf)
        sufs[p] = (inc - lf + carry) * LOG2E
        carry = carry + inc[:, 0:1]
    carry_ref[...] = jnp.broadcast_to(carry[:, 0:1], carry_ref.shape)
    suf = jnp.concatenate(sufs, axis=1)
    bias_f = jnp.concatenate([suf] * (nrow // FOX_HEADS), axis=0)
    n_keys = npg * LANES

    hooks = (lambda: None, lambda: wait(slot, False), lambda: None)
    attend(lat_buf[slot], kr_buf[slot], fk_buf[slot].reshape(FOX_W, n_keys),
           lambda: fv_buf[slot].reshape(FOX_W, n_keys), bias_f, None, 2, hooks)

    @pl.when(step == n_steps - 1)
    def _():
        wait(1 - slot, True)
        wait(1 - slot, False)

    @pl.when(c == n_chunks - 1)
    def _():
        row = lax.broadcasted_iota(jnp.int32, (nrow, LANES), 0)
        lane = lax.broadcasted_iota(jnp.int32, (nrow, LANES), 1)
        ok = lane <= (row >> 3)
        cum_new = _fwd_cumsum_lanes(lfn_ref[0]) * LOG2E
        b_f = jnp.where(ok, -jnp.concatenate([cum_new] * (nrow // FOX_HEADS), axis=0), NEG)
        b_m = jnp.where(ok, 0.0, NEG)
        attend(latn_ref[0], krn_ref[0], fkn_ref[0], lambda: fvn_ref[0], b_f, b_m, 1)
        accm = accm_ref[...]
        om_ref[0] = (accm[:nrow] + accm[nrow:]) / lm_ref[...]
        accf = accf_ref[...]
        of_ref[0] = (accf + pltpu.roll(accf, LANES - nrow, 1)) / _col_to_lanes(lfs_ref[...], 1.0)


def _sample_attn(page_table, qa, qr, qf, latn, krn, fkn, fvn, lfn,
                 c_lat, c_kr, c_fk, c_fv, c_lf, *, pages):
    nb, n_pages = page_table.shape
    n_chunks = n_pages // pages
    nrow = qa.shape[1]

    def per_b(shape):
        nd = len(shape)
        return pl.BlockSpec((1,) + shape[1:], lambda b, c, pt: (b,) + (0,) * (nd - 1))

    small = [qa, qr, qf, latn, krn, fkn, fvn, lfn]
    caches = [c_lat, c_kr, c_fk, c_fv, c_lf]
    in_specs = [per_b(a.shape) for a in small] + [pl.BlockSpec(memory_space=pl.ANY)] * len(caches)
    operands = small + caches
    n_keys = pages * c_lat.shape[2]
    grid_spec = pltpu.PrefetchScalarGridSpec(
        num_scalar_prefetch=1,
        grid=(nb, n_chunks),
        in_specs=in_specs,
        out_specs=[pl.BlockSpec((1, nrow, KV_LORA), lambda b, c, pt: (b, 0, 0)),
                   pl.BlockSpec((1, FOX_W, LANES), lambda b, c, pt: (b, 0, 0))],
        scratch_shapes=[pltpu.VMEM((nrow, 1), F32), pltpu.VMEM((nrow, 1), F32),
                        pltpu.VMEM((2 * nrow, KV_LORA), F32),
                        pltpu.VMEM((nrow, 1), F32), pltpu.VMEM((nrow, 1), F32),
                        pltpu.VMEM((FOX_W, LANES), F32),
                        pltpu.VMEM((FOX_HEADS, LANES), F32),
                        pltpu.VMEM((2, n_keys, KV_LORA), F32),
                        pltpu.VMEM((2, QK_ROPE, n_keys), F32),
                        pltpu.VMEM((2, FOX_HEADS, FOX_HEAD_DIM, n_keys), F32),
                        pltpu.VMEM((2, FOX_HEADS, FOX_HEAD_DIM, n_keys), F32),
                        pltpu.VMEM((2, FOX_HEADS, n_keys), F32),
                        pltpu.SemaphoreType.DMA((2,)), pltpu.SemaphoreType.DMA((2,))],
    )
    return pl.pallas_call(
        functools.partial(_sample_attn_kernel, pages=pages, n_chunks=n_chunks),
        grid_spec=grid_spec,
        out_shape=[jax.ShapeDtypeStruct((nb, nrow, KV_LORA), F32),
                   jax.ShapeDtypeStruct((nb, FOX_W, LANES), F32)],
        compiler_params=_cparams(("arbitrary", "arbitrary"), VMEM_LIMIT),
        name="sample_attn",
    )(page_table, *operands)


def _hl(w):
    bits = lax.bitcast_convert_type(w, jnp.uint32) & jnp.uint32(0xFFFF0000)
    hi = lax.bitcast_convert_type(bits, F32)
    return hi.astype(BF16), (w - hi).astype(BF16)


def _rope_tables(pos):
    half = QK_ROPE // 2
    inv = ROPE_THETA ** (-jnp.arange(half, dtype=F32) / half)
    ang = pos.astype(F32)[:, None] * inv[None, :]
    cc = jnp.concatenate([jnp.cos(ang), jnp.cos(ang)], axis=1)
    ss = jnp.concatenate([jnp.sin(ang), jnp.sin(ang)], axis=1)
    n = pos.shape[0]
    qs = MLA_SCALE * LOG2E
    ctab = jnp.concatenate([jnp.full((n, KV_LORA), qs, F32), cc * qs,
                            jnp.zeros((n, QHEAD_W - KV_LORA - QK_ROPE), F32)], axis=1)
    stab = jnp.concatenate([jnp.zeros((n, KV_LORA), F32), ss * qs,
                            jnp.zeros((n, QHEAD_W - KV_LORA - QK_ROPE), F32)], axis=1)
    zpad = jnp.zeros((n, LANES - QK_ROPE), F32)
    return ctab, stab, jnp.concatenate([cc, zpad], axis=1), jnp.concatenate([ss, zpad], axis=1)


def _swap_halves(w):
    half = QK_ROPE // 2
    return jnp.concatenate([-w[..., half:], w[..., :half]], axis=-1)


def kernel(x_prompt, x_sample, cache_mla_latent, cache_mla_krope, cache_fox_k, cache_fox_v,
           cache_fox_logf, page_table, c_prompt, c_sample, w_ada, b_ada, g_pre_mix, g_post_mix,
           g_pre_ffn, g_post_ffn, w_in, b_forget, g_q_lora, g_kv_lora, w_uq, w_uk, w_uv, w_o,
           w_group_router, b_group_router, w_expert_router, b_expert_router, w_gate, w_up, w_down):
    assert w_ada.shape[0] == 1, "single-layer trunk"
    batch, seq, d = x_prompt.shape
    dec_batch, n_new, _ = x_sample.shape
    n_pages = page_table.shape[1]
    page = cache_mla_latent.shape[2]
    past_len = n_pages * page
    tp = batch * seq
    ts = dec_batch * n_new

    w_in0 = w_in[0]
    s0, s1, s2, s3, s4, s5 = 256, 384, 416, 928, 1440, 1952
    w_kr = w_in0[:, s1:s2]
    z = lambda n: jnp.zeros((d, n), F32)
    w_all = jnp.concatenate([
        w_in0[:, 0:s0],
        w_in0[:, s0:s1], w_kr, z(96),
        _swap_halves(w_kr), z(96),
        w_in0[:, s2:s3], w_in0[:, s3:s4], w_in0[:, s4:s5],
        w_in0[:, s5:], z(120)], axis=1)
    w_uq0 = w_uq[0]
    w_qabs = _fold(jnp.transpose(w_uq0[:, :, :QK_NOPE], (1, 0, 2)),
                   jnp.transpose(w_uk[0], (0, 2, 1)))
    w_qr = jnp.transpose(w_uq0[:, :, QK_NOPE:], (1, 0, 2))
    zq = lambda n: jnp.zeros((MLA_HEADS, Q_LORA, n), F32)
    w_a = jnp.concatenate([w_qabs, w_qr, zq(96)], axis=2)
    w_b = jnp.concatenate([zq(128), _swap_halves(w_qr), zq(96)], axis=2)
    w_a = jnp.transpose(w_a, (1, 0, 2)).reshape(Q_LORA, MLA_HEADS * QHEAD_W)
    w_b = jnp.transpose(w_b, (1, 0, 2)).reshape(Q_LORA, MLA_HEADS * QHEAD_W)
    w_o0 = w_o[0]
    w_ov = _fold(w_uv[0], w_o0[:MLA_HEADS * V_HEAD].reshape(MLA_HEADS, V_HEAD, d))
    w_ov = w_ov.reshape(MLA_HEADS * KV_LORA, d)
    w_of = w_o0[MLA_HEADS * V_HEAD:]
    w_r = jnp.concatenate([w_group_router[0], w_expert_router[0].reshape(d, N_EXPERTS),
                           jnp.zeros((d, LANES - N_GROUPS - N_EXPERTS), F32)], axis=1)
    b_r = jnp.concatenate([b_group_router[0], b_expert_router[0].reshape(N_EXPERTS),
                           jnp.zeros((LANES - N_GROUPS - N_EXPERTS,), F32)])[None, :]
    bfg = jnp.concatenate([b_forget[0], jnp.zeros((LANES - FOX_HEADS,), F32)])[None, :]
    w_all_hl, w_a_hl, w_b_hl = _hl(w_all), _hl(w_a), _hl(w_b)
    w_ov_hl, w_of_hl, w_r_hl = _hl(w_ov), _hl(w_of), _hl(w_r)
    wg = w_gate[0].reshape(N_EXPERTS, d, D_EXPERT)
    wu = w_up[0].reshape(N_EXPERTS, d, D_EXPERT)
    wd = w_down[0].reshape(N_EXPERTS, D_EXPERT, d)

    c_all = jnp.concatenate([c_prompt, c_sample], axis=0)
    c_all = jnp.pad(c_all, ((0, -c_all.shape[0] % 64), (0, 0)))
    mod = _ada(c_all, w_ada[0], b_ada)
    mod_p = mod[:batch].reshape(batch, 1, 6 * d)
    mod_s = jnp.repeat(mod[batch:batch + dec_batch], n_new, axis=0)

    tm = 512
    xp2 = x_prompt.reshape(tp, d)
    tabs_p = _rope_tables(jnp.arange(seq))
    (q_p, kmla_p, ckv_p, kr_p, fq_p, fk_p, fk16_p, fv_p, fv16_p, lf_p, cum_p) = _proj_prompt(
        xp2, mod_p, g_pre_mix, g_q_lora, g_kv_lora, bfg, tabs_p,
        w_all.astype(BF16), w_a.astype(BF16), w_b.astype(BF16), seq=seq, tm=tm)
    tk = 512
    o_lat_p = _mla_prompt(q_p, kmla_p, batch=batch, seq=seq, tq=512, tk=tk)
    ck4 = cum_p.reshape(batch, FOX_HEADS, seq // tk, tk)
    o_fox_p = _fox_prompt(fq_p, fk16_p, fv16_p, ck4, batch=batch, seq=seq, tq=512, tk=tk)
    x1_p, h2_p, rt_p, cnt_p = _post_prompt(xp2, mod_p, o_lat_p, o_fox_p,
                                           w_ov.astype(BF16), w_of.astype(BF16),
                                           g_post_mix, g_pre_ffn, w_r_hl, b_r, seq=seq, tm=tm)

    tme = 256
    n_asg = 2 * tp
    n_rows = n_asg + N_EXPERTS * tme
    cnt = cnt_p[0, :N_EXPERTS].astype(jnp.int32)
    pend = jnp.cumsum(((cnt + tme - 1) // tme) * tme)
    offs = pend - ((cnt + tme - 1) // tme) * tme
    tile_start = jnp.arange(n_rows // tme, dtype=jnp.int32) * tme
    tile_e = jnp.sum(tile_start[:, None] >= pend[None, :], axis=1, dtype=jnp.int32)
    tile_e = jnp.minimum(tile_e, N_EXPERTS - 1)
    n_used = (pend[-1] // tme).astype(jnp.int32)[None]
    e_ids = rt_p[:, 0:2]
    hit = e_ids[:, :, None] == jnp.arange(N_EXPERTS, dtype=F32)[None, None, :]
    offs_tok = jnp.sum(jnp.where(hit, offs[None, None, :], 0), axis=-1, dtype=jnp.int32)
    pos = (offs_tok + rt_p[:, 4:6].astype(jnp.int32)).reshape(n_asg)
    xs = _dispatch(pos, h2_p, n_rows, tm=256)
    ys = _gmm(tile_e, n_used, xs, wg, wu, wd, tm=tme)
    y_prompt = _combine(pos, x1_p, mod_p, rt_p, g_post_ffn, ys, seq=seq, tm=256)

    xs2 = x_sample.reshape(ts, d)
    pos_s = jnp.tile(past_len + jnp.arange(n_new), dec_batch)
    tabs_s = _rope_tables(pos_s)
    q_s, ckv_s, krp_s, fq_s, fk_s, fv_s, lfp_s = _proj_sample(
        xs2, mod_s, g_pre_mix, g_q_lora, g_kv_lora, bfg, tabs_s, w_all_hl, w_a_hl, w_b_hl)
    q4 = q_s.reshape(dec_batch, n_new, MLA_HEADS, QHEAD_W)
    nrow = n_new * MLA_HEADS
    qa = q4[..., :KV_LORA].reshape(dec_batch, nrow, KV_LORA)
    qr = q4[..., KV_LORA:].reshape(dec_batch, nrow, QHEAD_W - KV_LORA)
    head_mask = (jnp.arange(FOX_W)[None, :] // FOX_HEAD_DIM == jnp.arange(FOX_HEADS)[:, None])
    qf = (fq_s.reshape(dec_batch, n_new, 1, FOX_W) * head_mask[None, None].astype(F32))
    qf = qf.reshape(dec_batch, nrow, FOX_W)
    padk = page - n_new
    latn = jnp.pad(ckv_s.reshape(dec_batch, n_new, KV_LORA), ((0, 0), (0, padk), (0, 0)))
    krn = jnp.pad(jnp.transpose(krp_s[:, :QK_ROPE].reshape(dec_batch, n_new, QK_ROPE), (0, 2, 1)),
                  ((0, 0), (0, 0), (0, padk)))
    fkn = jnp.pad(jnp.transpose(fk_s.reshape(dec_batch, n_new, FOX_W), (0, 2, 1)),
                  ((0, 0), (0, 0), (0, padk)))
    fvn = jnp.pad(jnp.transpose(fv_s.reshape(dec_batch, n_new, FOX_W), (0, 2, 1)),
                  ((0, 0), (0, 0), (0, padk)))
    lfn = jnp.pad(jnp.transpose(lfp_s[:, :FOX_HEADS].reshape(dec_batch, n_new, FOX_HEADS),
                                (0, 2, 1)), ((0, 0), (0, 0), (0, padk)))
    c_lat = cache_mla_latent
    c_kr = jnp.transpose(cache_mla_krope, (0, 1, 3, 2))
    c_fk = jnp.transpose(cache_fox_k, (0, 1, 3, 4, 2))
    c_fv = jnp.transpose(cache_fox_v, (0, 1, 3, 4, 2))
    c_lf = jnp.transpose(cache_fox_logf, (0, 1, 3, 2))
    om, oft = _sample_attn(page_table, qa, qr, qf, latn, krn, fkn, fvn, lfn,
                           c_lat, c_kr, c_fk, c_fv, c_lf, pages=16)
    o_lat_s = om.reshape(dec_batch, n_new, MLA_HEADS * KV_LORA).reshape(ts, MLA_HEADS * KV_LORA)
    oft4 = oft[:, :, :nrow].reshape(dec_batch, FOX_HEADS, FOX_HEAD_DIM, n_new, FOX_HEADS)
    o_fox_s = jnp.einsum('bhdqh->bqhd', oft4).reshape(ts, FOX_W)
    x1_s, h2_s, rt_s = _post_sample(xs2, mod_s, o_lat_s, o_fox_s, w_ov_hl, w_of_hl,
                                    g_post_mix, g_pre_ffn, w_r_hl, b_r)
    y_sample = _moe_sample(h2_s, rt_s, x1_s, mod_s, g_post_ffn, wg, wu, wd)

    return (y_prompt.reshape(batch, seq, d), y_sample.reshape(dec_batch, n_new, d),
            ckv_p.reshape(batch, 1, seq, KV_LORA),
            jnp.transpose(kr_p.reshape(batch, 1, QK_ROPE, seq), (0, 1, 3, 2)),
            jnp.transpose(fk_p.reshape(batch, 1, FOX_HEADS, FOX_HEAD_DIM, seq), (0, 1, 4, 2, 3)),
            jnp.transpose(fv_p.reshape(batch, 1, FOX_HEADS, FOX_HEAD_DIM, seq), (0, 1, 4, 2, 3)),
            jnp.transpose(lf_p.reshape(batch, 1, FOX_HEADS, seq), (0, 1, 3, 2)),
            ckv_s.reshape(dec_batch, 1, n_new, KV_LORA),
            krp_s[:, :QK_ROPE].reshape(dec_batch, 1, n_new, QK_ROPE),
            fk_s.reshape(dec_batch, 1, n_new, FOX_HEADS, FOX_HEAD_DIM),
            fv_s.reshape(dec_batch, 1, n_new, FOX_HEADS, FOX_HEAD_DIM),
            lfp_s[:, :FOX_HEADS].reshape(dec_batch, 1, n_new, FOX_HEADS))
```
